```python
import math
import jax, jax.numpy as jnp
from jax import lax
import numpy as np

D_MODEL = 1024
BATCH = 4
SEQ = 4096
DEPTH = 4

f32 = jnp.float32
N_EVEN = (DEPTH + 1) // 2
N_ODD = DEPTH // 2
D_FF = 4 * D_MODEL
MIX_WIDTH = D_MODEL
CONV_W = 4
NORM_EPS = 1e-6

LRU_WIDTH = MIX_WIDTH // 2
LRU_BLOCKS = 8
LRU_BW = LRU_WIDTH // LRU_BLOCKS
LRU_C = 8.0
LRU_COLS = 2 * LRU_WIDTH

RWKV_WIDTH = MIX_WIDTH // 2
RWKV_HEAD = 64
RWKV_HEADS = RWKV_WIDTH // RWKV_HEAD
RWKV_DECAY_RANK = 64
RWKV_A_RANK = 64
RWKV_GATE_RANK = 128
RWKV_COLS = 3 * RWKV_WIDTH + RWKV_DECAY_RANK + RWKV_A_RANK + RWKV_GATE_RANK
RWKV_LN_EPS = 64e-5
EVEN_IN_COLS = LRU_COLS + RWKV_COLS

GDN_HEADS = 4
GDN_DK = 128
GDN_DV = 128
GDN_WIDTH = GDN_HEADS * GDN_DV
GDN_QKV = GDN_HEADS * (2 * GDN_DK + GDN_DV)
GDN_COLS = GDN_QKV + GDN_WIDTH + 2 * GDN_HEADS
GDN_CHUNK = 64

MLSTM_HEADS = 4
MLSTM_DK = 64
MLSTM_DV = 128
MLSTM_WIDTH = MLSTM_HEADS * MLSTM_DV
MLSTM_QK = 2 * MLSTM_HEADS * MLSTM_DK
MLSTM_COLS = MLSTM_QK + 2 * MLSTM_WIDTH + 2 * MLSTM_HEADS
MLSTM_CHUNK = 64
ODD_IN_COLS = GDN_COLS + MLSTM_COLS

kernel_name = 'hybrid_rglru_rwkv7_gdn_mlstm_trunk'


def rms_norm(x, g, eps=NORM_EPS):
    xf = x.astype(f32)
    y = xf * lax.rsqrt(jnp.mean(jnp.square(xf), axis=-1, keepdims=True) + eps)
    return (y * g.astype(f32)).astype(x.dtype)


def l2_normalize(x, eps=1e-6):
    xf = x.astype(f32)
    return xf * lax.rsqrt(jnp.sum(jnp.square(xf), axis=-1, keepdims=True) + eps)


def causal_dwconv(x, w):
    K, T = w.shape[0], x.shape[1]
    xp = jnp.pad(x, ((0, 0), (K - 1, 0), (0, 0)))
    return sum(xp[:, k:k + T] * w[k] for k in range(K))


def token_shift(x):
    return jnp.pad(x, ((0, 0), (1, 0), (0, 0)))[:, :-1]


def to_chunks(x, c):
    B, T, H = x.shape[:3]
    x = x.reshape((B, T // c, c, H) + x.shape[3:])
    return jnp.moveaxis(x, 3, 1)


def from_chunks(x):
    B, H, N, C, d = x.shape
    return jnp.moveaxis(x, 1, 3).reshape(B, N * C, H, d)


def chunk_major(x):
    return jnp.moveaxis(x, 2, 0)


def _lin_combine(left, right):
    a_l, b_l = left
    a_r, b_r = right
    return a_l * a_r, a_r * b_l + b_r


def griffin_recurrent(p, conv_w, conv_b, wa, ba, wx, bx, lam):
    gate_in, rec_in = p[..., :LRU_WIDTH], p[..., LRU_WIDTH:]
    xc = (causal_dwconv(rec_in, conv_w) + conv_b).astype(f32)
    B, T, _ = xc.shape
    xb = xc.reshape(B, T, LRU_BLOCKS, LRU_BW)
    r_gate = jax.nn.sigmoid(jnp.einsum('btnd,nde->btne', xb, wa).reshape(B, T, LRU_WIDTH) + ba)
    i_gate = jax.nn.sigmoid(jnp.einsum('btnd,nde->btne', xb, wx).reshape(B, T, LRU_WIDTH) + bx)
    log_a = -LRU_C * r_gate * jax.nn.softplus(-lam.astype(f32))
    a = jnp.exp(log_a)
    mult = jnp.sqrt(-jnp.expm1(2.0 * log_a))
    mult = jnp.where(jnp.arange(T)[None, :, None] == 0, 1.0, mult)
    u = i_gate * xc * mult
    _, h = lax.associative_scan(_lin_combine, (a, u), axis=1)
    return (h * jax.nn.gelu(gate_in.astype(f32), approximate=True)).astype(p.dtype)


def rwkv7_mix(p, mu, w0, w2, a0, a2, g2, k_k, k_a, r_k, ln_w, ln_b):
    W, RD, RA = RWKV_WIDTH, RWKV_DECAY_RANK, RWKV_A_RANK
    pf = p.astype(f32)
    pf = pf + mu * (token_shift(pf) - pf)
    r, k, v = pf[..., :W], pf[..., W:2 * W], pf[..., 2 * W:3 * W]
    xw = pf[..., 3 * W:3 * W + RD]
    xa = pf[..., 3 * W + RD:3 * W + RD + RA]
    xg = pf[..., 3 * W + RD + RA:]
    w_log = -jax.nn.softplus(-(w0 + jnp.tanh(xw) @ w2)) - 0.5
    decay = jnp.exp(-jnp.exp(w_log))
    a = jax.nn.sigmoid(a0 + xa @ a2)
    g = jax.nn.sigmoid(xg) @ g2
    B, T, _ = pf.shape
    heads = lambda t: t.reshape(B, T, RWKV_HEADS, RWKV_HEAD)
    kk = l2_normalize(heads(k * k_k))
    k = k * (1.0 + (a - 1.0) * k_a)
    r_h, k_h, v_h, a_h, w_h = heads(r), heads(k), heads(v), heads(a), heads(decay)

    def step(S, inp):
        rt, wt, kt, vt, kkt, at = inp
        sk = jnp.einsum('bhvk,bhk->bhv', S, kkt)
        S = (S * wt[:, :, None, :] - sk[..., None] * (kkt * at)[:, :, None, :]
             + vt[..., None] * kt[:, :, None, :])
        return S, jnp.einsum('bhvk,bhk->bhv', S, rt)

    tm = lambda t: jnp.moveaxis(t, 1, 0)
    S0 = jnp.zeros((B, RWKV_HEADS, RWKV_HEAD, RWKV_HEAD), f32)
    _, y = lax.scan(step, S0, (tm(r_h), tm(w_h), tm(k_h), tm(v_h), tm(kk), tm(a_h)))
    y = jnp.moveaxis(y, 0, 1)
    mean = jnp.mean(y, axis=-1, keepdims=True)
    var = jnp.mean(jnp.square(y - mean), axis=-1, keepdims=True)
    y = ((y - mean) * lax.rsqrt(var + RWKV_LN_EPS)).reshape(B, T, W) * ln_w + ln_b
    bonus = jnp.sum(r_h * k_h * r_k, axis=-1, keepdims=True) * v_h
    y = (y + bonus.reshape(B, T, W)) * g
    return y.astype(p.dtype)


def gated_delta_chunked(q, k, v, g, beta):
    B, T, H, DK = q.shape
    DV = v.shape[-1]
    C = GDN_CHUNK
    q = q * DK ** -0.5
    qc, kc, vc = to_chunks(q, C), to_chunks(k, C), to_chunks(v, C)
    gc = jnp.cumsum(to_chunks(g, C), axis=-1)
    bc = to_chunks(beta, C)
    causal = jnp.tril(jnp.ones((C, C), bool))
    strict = jnp.tril(jnp.ones((C, C), bool), -1)
    decay = jnp.exp(jnp.where(causal, gc[..., :, None] - gc[..., None, :], -jnp.inf))
    kb = kc * bc[..., None]
    A = jnp.where(strict, jnp.einsum('bhnik,bhnjk->bhnij', kb, kc) * decay, 0.0)
    u = lax.linalg.triangular_solve(A, vc * bc[..., None], left_side=True, lower=True, unit_diagonal=True)
    w = lax.linalg.triangular_solve(A, kb * jnp.exp(gc)[..., None], left_side=True, lower=True, unit_diagonal=True)
    qk = jnp.einsum('bhnik,bhnjk->bhnij', qc, kc) * decay
    q_dec = qc * jnp.exp(gc)[..., None]
    k_dec = kc * jnp.exp(gc[..., -1:] - gc)[..., None]
    g_last = jnp.exp(gc[..., -1])

    def step(S, inp):
        q_d, k_d, u_c, w_c, qk_c, gl = inp
        v_new = u_c - jnp.einsum('bhck,bhkv->bhcv', w_c, S)
        o = jnp.einsum('bhck,bhkv->bhcv', q_d, S) + jnp.einsum('bhij,bhjv->bhiv', qk_c, v_new)
        S = S * gl[..., None, None] + jnp.einsum('bhck,bhcv->bhkv', k_d, v_new)
        return S, o

    S0 = jnp.zeros((B, H, DK, DV), f32)
    _, o = lax.scan(step, S0, (chunk_major(q_dec), chunk_major(k_dec), chunk_major(u),
                               chunk_major(w), chunk_major(qk), chunk_major(g_last)))
    return from_chunks(jnp.moveaxis(o, 0, 2))


def gdn_mix(p, conv_w, a_log, dt_bias, norm_g):
    B, T, _ = p.shape
    H = GDN_HEADS
    hk = H * GDN_DK
    qkv = jax.nn.silu(causal_dwconv(p[..., :GDN_QKV], conv_w))
    q = l2_normalize(qkv[..., :hk].reshape(B, T, H, GDN_DK))
    k = l2_normalize(qkv[..., hk:2 * hk].reshape(B, T, H, GDN_DK))
    v = qkv[..., 2 * hk:].reshape(B, T, H, GDN_DV).astype(f32)
    o0 = GDN_QKV
    z = p[..., o0:o0 + GDN_WIDTH].reshape(B, T, H, GDN_DV).astype(f32)
    b_raw = p[..., o0 + GDN_WIDTH:o0 + GDN_WIDTH + H].astype(f32)
    a_raw = p[..., o0 + GDN_WIDTH + H:].astype(f32)
    beta = jax.nn.sigmoid(b_raw)
    g = -jnp.exp(a_log.astype(f32)) * jax.nn.softplus(a_raw + dt_bias)
    o = gated_delta_chunked(q, k, v, g, beta)
    o = rms_norm(o, norm_g) * jax.nn.silu(z)
    return o.reshape(B, T, GDN_WIDTH).astype(p.dtype)


def mlstm_chunked(q, k, v, i_pre, f_pre):
    B, T, H, DK = q.shape
    DV = v.shape[-1]
    C = MLSTM_CHUNK
    q = q * DK ** -0.5
    qc, kc, vc = to_chunks(q, C), to_chunks(k, C), to_chunks(v, C)
    b = jnp.cumsum(to_chunks(jax.nn.log_sigmoid(f_pre), C), axis=-1)
    ic = to_chunks(i_pre, C)
    causal = jnp.tril(jnp.ones((C, C), bool))
    D = jnp.where(causal, b[..., :, None] - b[..., None, :] + ic[..., None, :], -jnp.inf)
    d_max = jnp.max(D, axis=-1)
    to_end = b[..., -1:] - b + ic
    qk = jnp.einsum('bhnik,bhnjk->bhnij', qc, kc)

    def step(carry, inp):
        Cs, ns, m = carry
        q_c, k_c, v_c, b_c, D_c, dmax_c, te_c, qk_c = inp
        inter = b_c + m[..., None]
        m_row = jnp.maximum(inter, dmax_c)
        w_inter = jnp.exp(inter - m_row)
        Wm = jnp.exp(D_c - m_row[..., None]) * qk_c
        num = (w_inter[..., None] * jnp.einsum('bhck,bhkv->bhcv', q_c, Cs)
               + jnp.einsum('bhij,bhjv->bhiv', Wm, v_c))
        den = w_inter * jnp.einsum('bhck,bhk->bhc', q_c, ns) + jnp.sum(Wm, axis=-1)
        h = num / jnp.maximum(jnp.abs(den), jnp.exp(-m_row))[..., None]
        m_new = jnp.maximum(b_c[..., -1] + m, jnp.max(te_c, axis=-1))
        decay_prev = jnp.exp(b_c[..., -1] + m - m_new)
        wk = k_c * jnp.exp(te_c - m_new[..., None])[..., None]
        Cs = decay_prev[..., None, None] * Cs + jnp.einsum('bhck,bhcv->bhkv', wk, v_c)
        ns = decay_prev[..., None] * ns + jnp.sum(wk, axis=-2)
        return (Cs, ns, m_new), h

    init = (jnp.zeros((B, H, DK, DV), f32), jnp.zeros((B, H, DK), f32), jnp.zeros((B, H), f32))
    _, h = lax.scan(step, init, (chunk_major(qc), chunk_major(kc), chunk_major(vc), chunk_major(b),
                                 chunk_major(D), chunk_major(d_max), chunk_major(to_end), chunk_major(qk)))
    return from_chunks(jnp.moveaxis(h, 0, 2))


def mlstm_mix(p, conv_w, ig_b, fg_b, norm_g):
    B, T, _ = p.shape
    H, W = MLSTM_HEADS, MLSTM_WIDTH
    hk = H * MLSTM_DK
    qk = jax.nn.silu(causal_dwconv(p[..., :MLSTM_QK], conv_w))
    q = qk[..., :hk].reshape(B, T, H, MLSTM_DK).astype(f32)
    k = qk[..., hk:].reshape(B, T, H, MLSTM_DK).astype(f32)
    o0 = MLSTM_QK
    v = p[..., o0:o0 + W].reshape(B, T, H, MLSTM_DV).astype(f32)
    og = p[..., o0 + W:o0 + 2 * W].astype(f32)
    i_pre = p[..., o0 + 2 * W:o0 + 2 * W + H].astype(f32) + ig_b
    f_pre = p[..., o0 + 2 * W + H:].astype(f32) + fg_b
    h = mlstm_chunked(q, k, v, i_pre, f_pre)
    h = rms_norm(h, norm_g).reshape(B, T, W) * jax.nn.sigmoid(og)
    return h.astype(p.dtype)


def setup_inputs(seed: int = 0) -> dict:
    key = jax.random.key(seed)
    ks = iter(jax.random.split(key, 48))

    def nrm(shape, scale):
        return jax.random.normal(next(ks), shape, f32) * scale

    def uni(shape, lo, hi):
        return jax.random.uniform(next(ks), shape, f32, lo, hi)

    E, O, D = N_EVEN, N_ODD, D_MODEL
    x = nrm((BATCH, SEQ, D), 1.0)
    norm_mix_g = 1.0 + nrm((DEPTH, D), 0.02)
    norm_mlp_g = 1.0 + nrm((DEPTH, D), 0.02)
    mlp_up = nrm((DEPTH, D, D_FF), D ** -0.5)
    mlp_down = nrm((DEPTH, D_FF, D), D_FF ** -0.5)
    final_g = 1.0 + nrm((D,), 0.02)
    ev_w_in = nrm((E, D, EVEN_IN_COLS), D ** -0.5)
    ev_w_out = nrm((E, MIX_WIDTH, D), MIX_WIDTH ** -0.5)
    lru_conv_w = nrm((E, CONV_W, LRU_WIDTH), CONV_W ** -0.5)
    lru_conv_b = nrm((E, LRU_WIDTH), 0.02)
    lru_wa = nrm((E, LRU_BLOCKS, LRU_BW, LRU_BW), LRU_BW ** -0.5)
    lru_ba = nrm((E, LRU_WIDTH), 0.02)
    lru_wx = nrm((E, LRU_BLOCKS, LRU_BW, LRU_BW), LRU_BW ** -0.5)
    lru_bx = nrm((E, LRU_WIDTH), 0.02)
    a_base = uni((E, LRU_WIDTH), 0.9, 0.999) ** (1.0 / LRU_C)
    lru_lambda = jnp.log(a_base) - jnp.log1p(-a_base)
    rwkv_mu = uni((E, RWKV_COLS), 0.0, 1.0)
    rwkv_w0 = uni((E, RWKV_WIDTH), -6.0, -1.0)
    rwkv_w2 = nrm((E, RWKV_DECAY_RANK, RWKV_WIDTH), 0.1 * RWKV_DECAY_RANK ** -0.5)
    rwkv_a0 = nrm((E, RWKV_WIDTH), 0.1)
    rwkv_a2 = nrm((E, RWKV_A_RANK, RWKV_WIDTH), RWKV_A_RANK ** -0.5)
    rwkv_g2 = nrm((E, RWKV_GATE_RANK, RWKV_WIDTH), RWKV_GATE_RANK ** -0.5)
    rwkv_kk = 0.85 + nrm((E, RWKV_WIDTH), 0.05)
    rwkv_ka = 1.0 + nrm((E, RWKV_WIDTH), 0.05)
    rwkv_rk = nrm((E, RWKV_HEADS, RWKV_HEAD), 0.1)
    rwkv_lnw = 1.0 + nrm((E, RWKV_WIDTH), 0.02)
    rwkv_lnb = nrm((E, RWKV_WIDTH), 0.02)
    od_w_in = nrm((O, D, ODD_IN_COLS), D ** -0.5)
    od_w_out = nrm((O, MIX_WIDTH, D), MIX_WIDTH ** -0.5)
    gdn_conv_w = nrm((O, CONV_W, GDN_QKV), CONV_W ** -0.5)
    gdn_a_log = jnp.log(uni((O, GDN_HEADS), 1.0, 16.0))
    dt = jnp.exp(uni((O, GDN_HEADS), math.log(1e-3), math.log(1e-1)))
    gdn_dt_bias = dt + jnp.log(-jnp.expm1(-dt))
    gdn_norm_g = 1.0 + nrm((O, GDN_DV), 0.02)
    mlstm_conv_w = nrm((O, CONV_W, MLSTM_QK), CONV_W ** -0.5)
    mlstm_ig_b = nrm((O, MLSTM_HEADS), 0.1)
    mlstm_fg_b = jnp.linspace(3.0, 6.0, MLSTM_HEADS, dtype=f32)[None, :] + nrm((O, MLSTM_HEADS), 0.1)
    mlstm_norm_g = 1.0 + nrm((O, MLSTM_DV), 0.02)
    return {
        'x': x, 'norm_mix_g': norm_mix_g, 'norm_mlp_g': norm_mlp_g,
        'mlp_up': mlp_up, 'mlp_down': mlp_down, 'final_g': final_g,
        'ev_w_in': ev_w_in, 'ev_w_out': ev_w_out,
        'lru_conv_w': lru_conv_w, 'lru_conv_b': lru_conv_b,
        'lru_wa': lru_wa, 'lru_ba': lru_ba, 'lru_wx': lru_wx, 'lru_bx': lru_bx,
        'lru_lambda': lru_lambda,
        'rwkv_mu': rwkv_mu, 'rwkv_w0': rwkv_w0, 'rwkv_w2': rwkv_w2,
        'rwkv_a0': rwkv_a0, 'rwkv_a2': rwkv_a2, 'rwkv_g2': rwkv_g2,
        'rwkv_kk': rwkv_kk, 'rwkv_ka': rwkv_ka, 'rwkv_rk': rwkv_rk,
        'rwkv_lnw': rwkv_lnw, 'rwkv_lnb': rwkv_lnb,
        'od_w_in': od_w_in, 'od_w_out': od_w_out,
        'gdn_conv_w': gdn_conv_w, 'gdn_a_log': gdn_a_log, 'gdn_dt_bias': gdn_dt_bias,
        'gdn_norm_g': gdn_norm_g,
        'mlstm_conv_w': mlstm_conv_w, 'mlstm_ig_b': mlstm_ig_b, 'mlstm_fg_b': mlstm_fg_b,
        'mlstm_norm_g': mlstm_norm_g,
    }


def reference(x, norm_mix_g, norm_mlp_g, mlp_up, mlp_down, final_g,
              ev_w_in, ev_w_out, lru_conv_w, lru_conv_b, lru_wa, lru_ba, lru_wx, lru_bx,
              lru_lambda, rwkv_mu, rwkv_w0, rwkv_w2, rwkv_a0, rwkv_a2, rwkv_g2,
              rwkv_kk, rwkv_ka, rwkv_rk, rwkv_lnw, rwkv_lnb,
              od_w_in, od_w_out, gdn_conv_w, gdn_a_log, gdn_dt_bias, gdn_norm_g,
              mlstm_conv_w, mlstm_ig_b, mlstm_fg_b, mlstm_norm_g):
    for l in range(DEPTH):
        h = rms_norm(x, norm_mix_g[l])
        if l % 2 == 0:
            e = l // 2
            p = h @ ev_w_in[e]
            ya = griffin_recurrent(p[..., :LRU_COLS], lru_conv_w[e], lru_conv_b[e],
                                   lru_wa[e], lru_ba[e], lru_wx[e], lru_bx[e], lru_lambda[e])
            yb = rwkv7_mix(p[..., LRU_COLS:], rwkv_mu[e], rwkv_w0[e], rwkv_w2[e], rwkv_a0[e],
                           rwkv_a2[e], rwkv_g2[e], rwkv_kk[e], rwkv_ka[e], rwkv_rk[e],
                           rwkv_lnw[e], rwkv_lnb[e])
            y = jnp.concatenate([ya, yb], axis=-1) @ ev_w_out[e]
        else:
            o = l // 2
            p = h @ od_w_in[o]
            yc = gdn_mix(p[..., :GDN_COLS], gdn_conv_w[o], gdn_a_log[o], gdn_dt_bias[o], gdn_norm_g[o])
            yd = mlstm_mix(p[..., GDN_COLS:], mlstm_conv_w[o], mlstm_ig_b[o], mlstm_fg_b[o], mlstm_norm_g[o])
            y = jnp.concatenate([yc, yd], axis=-1) @ od_w_out[o]
        x = x + y
        h = rms_norm(x, norm_mlp_g[l])
        x = x + jnp.square(jax.nn.relu(h @ mlp_up[l])) @ mlp_down[l]
    return rms_norm(x, final_g)
```

```python
import functools
import math

import jax
import jax.numpy as jnp
from jax import lax
from jax.experimental import pallas as pl
from jax.experimental.pallas import tpu as pltpu

f32 = jnp.float32
bf16 = jnp.bfloat16

LANES = 128
SUBLANES = 8
VMEM_LIMIT = 56 * 1024 * 1024

NORM_EPS = 1e-6
CONV_W = 4
LRU_C = 8.0
RWKV_HEAD = 64
RWKV_LN_EPS = 64e-5
CHUNK = 64

ROW_BLOCK = 512
SEQ_BLOCK = 512


def _cparams(sem):
    return pltpu.CompilerParams(dimension_semantics=sem, vmem_limit_bytes=VMEM_LIMIT)


def _const_spec(shape):
    nd = len(shape)
    return pl.BlockSpec(shape, lambda *_: (0,) * nd)


def _rms(x, g):
    return x * lax.rsqrt(jnp.mean(x * x, axis=-1, keepdims=True) + NORM_EPS) * g


def _softplus(x):
    return jnp.maximum(x, 0.0) + jnp.log1p(jnp.exp(-jnp.abs(x)))


def _sigmoid(x):
    return 1.0 / (1.0 + jnp.exp(-x))


def _norm_proj_body(widths, x_ref, g_ref, w_ref, *o_refs):
    h = _rms(x_ref[...], g_ref[...]).astype(bf16)
    c0 = 0
    for o_ref, wd in zip(o_refs, widths):
        o_ref[...] = jnp.dot(h, w_ref[:, c0:c0 + wd], preferred_element_type=f32).astype(o_ref.dtype)
        c0 += wd


def norm_proj(x, g, w, widths):
    n, d = x.shape
    tm = min(ROW_BLOCK, n)
    return pl.pallas_call(
        functools.partial(_norm_proj_body, widths),
        grid=(n // tm,),
        in_specs=[pl.BlockSpec((tm, d), lambda i: (i, 0)), _const_spec((1, d)), _const_spec(w.shape)],
        out_specs=[pl.BlockSpec((tm, wd), lambda i: (i, 0)) for wd in widths],
        out_shape=[jax.ShapeDtypeStruct((n, wd), f32) for wd in widths],
        compiler_params=_cparams(("parallel",)),
        name="norm_proj",
    )(x, g.reshape(1, d), w)


FF_BLOCK = 1024


def _out_mlp_body(final, ya_ref, yb_ref, x_ref, wo_ref, g_ref, wu_ref, wd_ref, gf_ref, o_ref):
    half = ya_ref.shape[1]
    y = jnp.dot(ya_ref[...], wo_ref[0:half, :], preferred_element_type=f32)
    y = y + jnp.dot(yb_ref[...], wo_ref[half:, :], preferred_element_type=f32)
    x1 = x_ref[...] + y
    h = _rms(x1, g_ref[...]).astype(bf16)
    acc = x1
    for c in range(wu_ref.shape[1] // FF_BLOCK):
        u = jnp.dot(h, wu_ref[:, c * FF_BLOCK:(c + 1) * FF_BLOCK], preferred_element_type=f32)
        u = jnp.square(jnp.maximum(u, 0.0)).astype(bf16)
        acc = acc + jnp.dot(u, wd_ref[c * FF_BLOCK:(c + 1) * FF_BLOCK, :], preferred_element_type=f32)
    if final:
        acc = _rms(acc, gf_ref[...])
    o_ref[...] = acc


def out_mlp(ya, yb, x, wo, g, wu, wd, gf, final):
    n, d = x.shape
    half = ya.shape[1]
    tm = min(ROW_BLOCK, n)
    row = lambda i: (i, 0)
    return pl.pallas_call(
        functools.partial(_out_mlp_body, final),
        grid=(n // tm,),
        in_specs=[pl.BlockSpec((tm, half), row), pl.BlockSpec((tm, half), row), pl.BlockSpec((tm, d), row),
                  _const_spec(wo.shape), _const_spec((1, d)), _const_spec(wu.shape), _const_spec(wd.shape),
                  _const_spec((1, d))],
        out_specs=pl.BlockSpec((tm, d), row),
        out_shape=jax.ShapeDtypeStruct((n, d), f32),
        compiler_params=_cparams(("parallel",)),
        name="out_mlp",
    )(ya, yb, x, wo, g.reshape(1, d), wu, wd, gf.reshape(1, d))


def _shift_rows(x, prev, k):
    rolled = pltpu.roll(x, k, axis=0)
    head = pltpu.roll(prev, k, axis=0)
    rows = lax.broadcasted_iota(jnp.int32, (SUBLANES, x.shape[1]), 0)
    top = jnp.where(rows < k, head, rolled[0:SUBLANES])
    return jnp.concatenate([top, rolled[SUBLANES:]], axis=0)


def _causal_conv(x, prev, w):
    y = x * w[CONV_W - 1:CONV_W]
    for k in range(1, CONV_W):
        y = y + _shift_rows(x, prev, k) * w[CONV_W - 1 - k:CONV_W - k]
    return y


def _scan_shift(x, s, fill):
    n, c = x.shape
    if s % SUBLANES == 0:
        return jnp.concatenate([jnp.full((s, c), fill, x.dtype), x[:n - s]], axis=0)
    rolled = pltpu.roll(x, s, axis=0)
    rows = lax.broadcasted_iota(jnp.int32, (SUBLANES, c), 0)
    top = jnp.where(rows < s, fill, rolled[0:SUBLANES])
    return jnp.concatenate([top, rolled[SUBLANES:]], axis=0)


def _linear_scan(a, u):
    s = 1
    while s < a.shape[0]:
        u = a * _scan_shift(u, s, 0.0) + u
        a = a * _scan_shift(a, s, 1.0)
        s *= 2
    return a, u


def _lru_body(p_ref, cw_ref, cb_ref, wg_ref, bg_ref, lam_ref, o_ref, xprev_ref, h_ref):
    i = pl.program_id(1)
    w = p_ref.shape[1] // 2

    @pl.when(i == 0)
    def _():
        xprev_ref[...] = jnp.zeros_like(xprev_ref)
        h_ref[...] = jnp.zeros_like(h_ref)

    gate_in = p_ref[:, 0:w]
    rec = p_ref[:, w:]
    n = rec.shape[0]
    xc = _causal_conv(rec, xprev_ref[...], cw_ref[...]) + cb_ref[...]
    xprev_ref[...] = rec[n - SUBLANES:]
    gates = jnp.dot(xc.astype(bf16), wg_ref[...], preferred_element_type=f32) + bg_ref[...]
    r_gate = _sigmoid(gates[:, 0:w])
    i_gate = _sigmoid(gates[:, w:])
    log_a = (-LRU_C) * r_gate * _softplus(-lam_ref[...])
    a = jnp.exp(log_a)
    mult = jnp.sqrt(1.0 - jnp.exp(2.0 * log_a))
    rows = lax.broadcasted_iota(jnp.int32, (n, w), 0)
    mult = jnp.where(jnp.logical_and(rows == 0, i == 0), 1.0, mult)
    u = i_gate * xc * mult
    a_cum, h = _linear_scan(a, u)
    h = h + a_cum * h_ref[...]
    h_ref[...] = h[n - 1:n]
    o_ref[...] = (h * jax.nn.gelu(gate_in, approximate=True)).astype(o_ref.dtype)


def _block_diag(wb):
    nb, d, e = wb.shape
    eye = jnp.eye(nb, dtype=wb.dtype)
    return (wb[:, :, None, :] * eye[:, None, :, None]).reshape(nb * d, nb * e)


def lru_mix(p, batch, conv_w, conv_b, wa, ba, wx, bx, lam):
    n, c2 = p.shape
    w = c2 // 2
    t = n // batch
    tb = min(SEQ_BLOCK, t)
    nt = t // tb
    wg = jnp.concatenate([_block_diag(wa), _block_diag(wx)], axis=1).astype(bf16)
    bg = jnp.concatenate([ba, bx]).reshape(1, 2 * w)
    return pl.pallas_call(
        _lru_body,
        grid=(batch, nt),
        in_specs=[pl.BlockSpec((tb, c2), lambda b, i: (b * nt + i, 0)), _const_spec((CONV_W, w)),
                  _const_spec((1, w)), _const_spec((w, 2 * w)), _const_spec((1, 2 * w)), _const_spec((1, w))],
        out_specs=pl.BlockSpec((tb, w), lambda b, i: (b * nt + i, 0)),
        out_shape=jax.ShapeDtypeStruct((n, w), bf16),
        scratch_shapes=[pltpu.VMEM((SUBLANES, w), f32), pltpu.VMEM((1, w), f32)],
        compiler_params=_cparams(("parallel", "arbitrary")),
        name="lru_mix",
    )(p, conv_w, conv_b.reshape(1, w), wg, bg, lam.reshape(1, w))


HIGHEST = lax.Precision.HIGHEST


def _dot(a, b):
    return jnp.dot(a.astype(bf16), b.astype(bf16), preferred_element_type=f32)


def _dot_nt(a, b):
    return lax.dot_general(a.astype(bf16), b.astype(bf16), (((1,), (1,)), ((), ())), preferred_element_type=f32)


def _dot_tn(a, b):
    return lax.dot_general(a.astype(bf16), b.astype(bf16), (((0,), (0,)), ((), ())), preferred_element_type=f32)


def _dot_split(x, e):
    hi = x.astype(bf16)
    lo = (x - hi.astype(f32)).astype(bf16)
    return jnp.dot(hi, e, preferred_element_type=f32) + jnp.dot(lo, e, preferred_element_type=f32)


def _iota(shape, axis):
    return lax.broadcasted_iota(jnp.int32, shape, axis)


def _tri_incl():
    return (_iota((CHUNK, CHUNK), 0) >= _iota((CHUNK, CHUNK), 1)).astype(f32)


def _chunk_cumsum(x):
    tri = _tri_incl()
    cs, ce = [], []
    for c in range(x.shape[0] // CHUNK):
        cs_c = jnp.dot(tri, x[c * CHUNK:(c + 1) * CHUNK], precision=HIGHEST, preferred_element_type=f32)
        cs.append(cs_c)
        ce.append(jnp.broadcast_to(cs_c[CHUNK - 1:CHUNK], cs_c.shape))
    return jnp.concatenate(cs, axis=0), jnp.concatenate(ce, axis=0)


def _head_pair_ones(head):
    return (_iota((LANES, LANES), 0) // head == _iota((LANES, LANES), 1) // head).astype(bf16)


def _segsum(x, e):
    return jnp.concatenate([_dot_split(x[:, j:j + LANES], e) for j in range(0, x.shape[1], LANES)], axis=1)


def _stack_heads(x):
    lo = _iota(x.shape, 1) < RWKV_HEAD
    return jnp.concatenate([jnp.where(lo, x, 0.0), jnp.where(lo, 0.0, x)], axis=0)


def _neumann_inverse(nmat):
    eye = (_iota(nmat.shape, 0) == _iota(nmat.shape, 1) % CHUNK).astype(f32)
    t = eye + nmat
    p = _dot(nmat, _stack_heads(nmat))
    k = 2
    while 2 * k < CHUNK:
        x = _dot(jnp.concatenate([t, p], axis=0), _stack_heads(p))
        t = t + x[0:CHUNK]
        p = x[CHUNK:]
        k *= 2
    return t + _dot(t, _stack_heads(p))


RWKV_BLOCK = 256


def _rwkv_body(p_ref, mu_ref, w0_ref, w2_ref, a0_ref, a2_ref, g2_ref, kk_ref, ka_ref, rk_ref, lnw_ref, lnb_ref,
               o_ref, prev_ref, s_ref, at_s, rt_s, bt_s, kt_s, be_s, ke_s, gc_s, v_s, y_s):
    i = pl.program_id(1)
    n = p_ref.shape[0]
    w = o_ref.shape[1]
    npair = w // LANES

    @pl.when(i == 0)
    def _():
        prev_ref[...] = jnp.zeros_like(prev_ref)
        s_ref[...] = jnp.zeros_like(s_ref)

    p = p_ref[...]
    pf = p + mu_ref[...] * (_shift_rows(p, prev_ref[...], 1) - p)
    prev_ref[...] = p[n - SUBLANES:]
    r, k, v = pf[:, 0:w], pf[:, w:2 * w], pf[:, 2 * w:3 * w]
    lowrank = pf[:, 3 * w:3 * w + LANES]
    xg = pf[:, 3 * w + LANES:]
    w_log = -_softplus(-(w0_ref[...] + _dot(jnp.tanh(lowrank), w2_ref[...]))) - 0.5
    lw = -jnp.exp(w_log)
    a = _sigmoid(a0_ref[...] + _dot(lowrank, a2_ref[...]))
    g = _dot(_sigmoid(xg), g2_ref[...])
    ones = _head_pair_ones(RWKV_HEAD)
    kk = k * kk_ref[...]
    kk = kk * lax.rsqrt(_segsum(kk * kk, ones) + 1e-6)
    k = k * (1.0 + (a - 1.0) * ka_ref[...])
    bp = -(kk * a)
    cs, ce = _chunk_cumsum(lw)
    inv_g = jnp.exp(-cs)
    to_end = jnp.exp(ce - cs)
    at_s[...] = kk * jnp.exp(cs - lw)
    rt_s[...] = r * jnp.exp(cs)
    bt_s[...] = bp * inv_g
    kt_s[...] = k * inv_g
    be_s[...] = bp * to_end
    ke_s[...] = k * to_end
    gc_s[...] = jnp.exp(ce)
    v_s[...] = v

    tt = _iota((CHUNK, LANES), 0)
    ss = _iota((CHUNK, LANES), 1) % CHUNK
    strict = tt > ss
    incl = tt >= ss
    diag_blocks = (_iota((LANES, LANES), 0) < RWKV_HEAD) == (_iota((LANES, LANES), 1) < RWKV_HEAD)

    def chunk_step(c, carry):
        rows = pl.ds(pl.multiple_of(c * CHUNK, CHUNK), CHUNK)
        for j in range(npair):
            cols = slice(j * LANES, (j + 1) * LANES)
            ap, rp, bt, kt = at_s[rows, cols], rt_s[rows, cols], bt_s[rows, cols], kt_s[rows, cols]
            vp = v_s[rows, cols]
            gm = _dot_nt(jnp.concatenate([ap, rp], axis=0),
                         jnp.concatenate([_stack_heads(bt), _stack_heads(kt)], axis=0))
            nmat = jnp.where(strict, gm[0:CHUNK, 0:LANES], 0.0)
            a_ak = jnp.where(strict, gm[0:CHUNK, LANES:], 0.0)
            m_rb = jnp.where(incl, gm[CHUNK:, 0:LANES], 0.0)
            m_rk = jnp.where(incl, gm[CHUNK:, LANES:], 0.0)
            tinv = _neumann_inverse(nmat)
            akv = _dot(a_ak, _stack_heads(vp))
            wu = _dot(tinv, jnp.concatenate([_stack_heads(ap), _stack_heads(akv)], axis=1))
            s = s_ref[j]
            x = _dot_nt(jnp.concatenate([wu[:, 0:LANES], rp], axis=0), s)
            u = x[0:CHUNK] + wu[:, LANES:]
            y = x[CHUNK:] + _dot(m_rk, _stack_heads(vp)) + _dot(m_rb, _stack_heads(u))
            upd = _dot_tn(jnp.concatenate([u, vp], axis=0),
                          jnp.concatenate([be_s[rows, cols], ke_s[rows, cols]], axis=0))
            gc = gc_s[pl.ds(pl.multiple_of(c * CHUNK, CHUNK), 1), cols]
            s_ref[j] = s * gc + jnp.where(diag_blocks, upd, 0.0)
            y_s[rows, cols] = y
        return carry

    lax.fori_loop(0, n // CHUNK, chunk_step, 0)

    y = y_s[...]
    inv_head = 1.0 / RWKV_HEAD
    mean = _segsum(y, ones) * inv_head
    yc = y - mean
    var = _segsum(yc * yc, ones) * inv_head
    yn = yc * lax.rsqrt(var + RWKV_LN_EPS) * lnw_ref[...] + lnb_ref[...]
    bonus = _segsum(r * k * rk_ref[...], ones) * v
    o_ref[...] = ((yn + bonus) * g).astype(o_ref.dtype)


def rwkv_mix(p, batch, mu, w0, w2, a0, a2, g2, k_k, k_a, r_k, ln_w, ln_b):
    n, cols = p.shape
    w = w0.shape[0]
    t = n // batch
    tb = min(RWKV_BLOCK, t)
    nt = t // tb
    rank = w2.shape[0]
    zeros = jnp.zeros((LANES - rank, w), f32)
    w2p = jnp.concatenate([w2, zeros], axis=0).astype(bf16)
    a2p = jnp.concatenate([zeros, a2], axis=0).astype(bf16)
    row = lambda v: v.reshape(1, -1)
    consts = [row(mu), row(w0), w2p, row(a0), a2p, g2.astype(bf16), row(k_k), row(k_a), row(r_k), row(ln_w),
              row(ln_b)]
    seq = lambda b, i: (b * nt + i, 0)
    blk = lambda: pltpu.VMEM((tb, w), f32)
    return pl.pallas_call(
        _rwkv_body,
        grid=(batch, nt),
        in_specs=[pl.BlockSpec((tb, cols), seq)] + [_const_spec(c.shape) for c in consts],
        out_specs=pl.BlockSpec((tb, w), seq),
        out_shape=jax.ShapeDtypeStruct((n, w), bf16),
        scratch_shapes=[pltpu.VMEM((SUBLANES, cols), f32), pltpu.VMEM((w // LANES, LANES, LANES), f32)]
                       + [blk() for _ in range(9)],
        compiler_params=_cparams(("parallel", "arbitrary")),
        name="rwkv_mix",
    )(p, *consts)


ODD_BLOCK = 256
GATE_LANES = LANES


def _silu(x):
    return x * _sigmoid(x)


def _expand_gate(gates, first, nheads):
    sel = (_iota((GATE_LANES, nheads * LANES), 0) - first == _iota((GATE_LANES, nheads * LANES), 1) // LANES)
    return _dot_split(gates, sel.astype(bf16))


def _pair_cols(x_ref, rows, pair):
    c = 2 * pair * LANES
    return x_ref[rows, c:c + LANES], x_ref[rows, c + LANES:c + 2 * LANES]


def _pair_select(x0, x1):
    return jnp.where(_iota(x0.shape, 1) < CHUNK, x0, x1)


def _pair_row_form(x0, x1):
    return jnp.transpose(jnp.concatenate([x0, x1], axis=0))[0:CHUNK]


def _pair_block_rhs(x0, x1):
    return jnp.concatenate([jnp.concatenate([x0, jnp.zeros_like(x0)], axis=1),
                            jnp.concatenate([jnp.zeros_like(x1), x1], axis=1)], axis=0)


def _head_rms(o, g, ones):
    return o * lax.rsqrt(_segsum(o * o, ones) * (1.0 / LANES) + NORM_EPS) * g


def _gdn_body(p_ref, z_ref, gate_ref, cw_ref, alog_ref, dtb_ref, ng_ref, o_ref,
              prev_ref, s_ref, q_s, k_s, kb_s, vb_s, qd_s, kbg_s, kd_s, gc_s, gl_s, o_s):
    i = pl.program_id(1)
    n = p_ref.shape[0]
    w = o_ref.shape[1]
    nheads = w // LANES

    @pl.when(i == 0)
    def _():
        prev_ref[...] = jnp.zeros_like(prev_ref)
        s_ref[...] = jnp.zeros_like(s_ref)

    p = p_ref[...]
    qkv = _silu(_causal_conv(p, prev_ref[...], cw_ref[...]))
    prev_ref[...] = p[n - SUBLANES:]
    ones = _head_pair_ones(LANES)
    q, k, v = qkv[:, 0:w], qkv[:, w:2 * w], qkv[:, 2 * w:]
    q = q * lax.rsqrt(_segsum(q * q, ones) + 1e-6) * (LANES ** -0.5)
    k = k * lax.rsqrt(_segsum(k * k, ones) + 1e-6)
    gates = gate_ref[...]
    beta = _sigmoid(_expand_gate(gates, 0, nheads))
    g = -jnp.exp(alog_ref[...]) * _softplus(_expand_gate(gates, nheads, nheads) + dtb_ref[...])
    gc, ge = _chunk_cumsum(g)
    eg = jnp.exp(gc)
    kb = k * beta
    q_s[...] = q
    k_s[...] = k
    kb_s[...] = kb
    vb_s[...] = v * beta
    qd_s[...] = q * eg
    kbg_s[...] = kb * eg
    kd_s[...] = k * jnp.exp(ge - gc)
    gc_s[...] = gc
    gl_s[...] = jnp.exp(ge)

    tt = _iota((CHUNK, LANES), 0)
    ss = _iota((CHUNK, LANES), 1) % CHUNK
    strict = tt > ss
    incl = tt >= ss

    def chunk_step(c, carry):
        r0 = pl.multiple_of(c * CHUNK, CHUNK)
        rows = pl.ds(r0, CHUNK)
        for pair in range(nheads // 2):
            k0, k1 = _pair_cols(k_s, rows, pair)
            kb0, kb1 = _pair_cols(kb_s, rows, pair)
            q0, q1 = _pair_cols(q_s, rows, pair)
            x = _dot_nt(jnp.concatenate([kb0, kb1, q0, q1], axis=0), jnp.concatenate([k0, k1], axis=0))
            g0, g1 = _pair_cols(gc_s, rows, pair)
            dec = jnp.exp(jnp.minimum(_pair_select(g0, g1) - _pair_row_form(g0, g1), 0.0))
            amat = jnp.where(strict, _pair_select(x[0:CHUNK], x[CHUNK:2 * CHUNK]) * dec, 0.0)
            qk = jnp.where(incl, _pair_select(x[2 * CHUNK:3 * CHUNK], x[3 * CHUNK:]) * dec, 0.0)
            tinv = _neumann_inverse(-amat)
            vb0, vb1 = _pair_cols(vb_s, rows, pair)
            kg0, kg1 = _pair_cols(kbg_s, rows, pair)
            uw = _dot(tinv, _pair_block_rhs(jnp.concatenate([vb0, kg0], axis=1),
                                            jnp.concatenate([vb1, kg1], axis=1)))
            qd = _pair_cols(qd_s, rows, pair)
            kd = _pair_cols(kd_s, rows, pair)
            gl = _pair_cols(gl_s, pl.ds(r0, 1), pair)
            vn, qs = [], []
            for h in range(2):
                s = s_ref[2 * pair + h]
                u = uw[:, 2 * h * LANES:(2 * h + 1) * LANES]
                wmat = uw[:, (2 * h + 1) * LANES:(2 * h + 2) * LANES]
                x2 = _dot(jnp.concatenate([wmat, qd[h]], axis=0), s)
                vn.append(u - x2[0:CHUNK])
                qs.append(x2[CHUNK:])
                s_ref[2 * pair + h] = s * gl[h] + _dot_tn(kd[h], vn[h])
            o = _dot(qk, _pair_block_rhs(vn[0], vn[1]))
            c0 = 2 * pair * LANES
            o_s[rows, c0:c0 + LANES] = qs[0] + o[:, 0:LANES]
            o_s[rows, c0 + LANES:c0 + 2 * LANES] = qs[1] + o[:, LANES:]
        return carry

    lax.fori_loop(0, n // CHUNK, chunk_step, 0)
    o_ref[...] = (_head_rms(o_s[...], ng_ref[...], ones) * _silu(z_ref[...])).astype(o_ref.dtype)


def gdn_mix(p_qkv, p_z, p_gates, batch, conv_w, a_log, dt_bias, norm_g):
    n, cols = p_qkv.shape
    w = p_z.shape[1]
    nheads = w // LANES
    t = n // batch
    tb = min(ODD_BLOCK, t)
    nt = t // tb
    per_head = lambda v: jnp.repeat(v, LANES).reshape(1, w)
    consts = [conv_w, per_head(a_log), per_head(dt_bias), jnp.tile(norm_g, nheads).reshape(1, w)]
    seq = lambda b, i: (b * nt + i, 0)
    blk = lambda: pltpu.VMEM((tb, w), f32)
    return pl.pallas_call(
        _gdn_body,
        grid=(batch, nt),
        in_specs=[pl.BlockSpec((tb, cols), seq), pl.BlockSpec((tb, w), seq), pl.BlockSpec((tb, GATE_LANES), seq)]
                 + [_const_spec(c.shape) for c in consts],
        out_specs=pl.BlockSpec((tb, w), seq),
        out_shape=jax.ShapeDtypeStruct((n, w), bf16),
        scratch_shapes=[pltpu.VMEM((SUBLANES, cols), f32), pltpu.VMEM((nheads, LANES, LANES), f32)]
                       + [blk() for _ in range(10)],
        compiler_params=_cparams(("parallel", "arbitrary")),
        name="gdn_mix",
    )(p_qkv, p_z, p_gates, *consts)


def _cummax_rows(x):
    out = []
    for c in range(x.shape[0] // CHUNK):
        y = x[c * CHUNK:(c + 1) * CHUNK]
        s = 1
        while s < CHUNK:
            y = jnp.maximum(y, _scan_shift(y, s, -jnp.inf))
            s *= 2
        out.append(y)
    return jnp.concatenate(out, axis=0)


def _mlstm_body(pqk_ref, v_ref, og_ref, gate_ref, cw_ref, igb_ref, fgb_ref, ng_ref, o_ref,
                prev_ref, s_ref, m_ref, q_s, k_s, b_s, be_s, cm_s, cc_s, te_s, h_s):
    i = pl.program_id(1)
    n = pqk_ref.shape[0]
    w = o_ref.shape[1]
    nheads = w // LANES
    hq = pqk_ref.shape[1] // 2

    @pl.when(i == 0)
    def _():
        prev_ref[...] = jnp.zeros_like(prev_ref)
        s_ref[...] = jnp.zeros_like(s_ref)
        m_ref[...] = jnp.zeros_like(m_ref)

    p = pqk_ref[...]
    qk = _silu(_causal_conv(p, prev_ref[...], cw_ref[...]))
    prev_ref[...] = p[n - SUBLANES:]
    q_s[...] = qk[:, 0:hq] * ((hq // nheads) ** -0.5)
    k_s[...] = qk[:, hq:]
    gates = gate_ref[...]
    i_pre = _expand_gate(gates, 2 * nheads, nheads) + igb_ref[...]
    f_pre = _expand_gate(gates, 3 * nheads, nheads) + fgb_ref[...]
    b, b_end = _chunk_cumsum(-_softplus(-f_pre))
    cc = i_pre - b
    b_s[...] = b
    cc_s[...] = cc
    cm_s[...] = _cummax_rows(cc)
    te_s[...] = b_end + cc
    be_s[...] = b_end

    incl = _iota((CHUNK, LANES), 0) >= _iota((CHUNK, LANES), 1) % CHUNK
    rows128 = _iota((LANES, 4 * LANES), 0) < CHUNK
    lanes512 = _iota((LANES, 4 * LANES), 1) < 2 * LANES
    diag_blocks = rows128 == lanes512
    ones_v = jnp.ones((CHUNK, LANES), f32)

    def chunk_step(c, carry):
        r0 = pl.multiple_of(c * CHUNK, CHUNK)
        rows = pl.ds(r0, CHUNK)
        first = pl.ds(r0, 1)
        for pair in range(nheads // 2):
            cols = slice(pair * LANES, (pair + 1) * LANES)
            qp, kp = q_s[rows, cols], k_s[rows, cols]
            v0, v1 = _pair_cols(v_ref, rows, pair)
            vext = jnp.concatenate([v0, ones_v, v1, ones_v], axis=1)
            m = _pair_cols(m_ref, slice(0, 1), pair)
            b = _pair_cols(b_s, rows, pair)
            cm = _pair_cols(cm_s, rows, pair)
            cc = _pair_cols(cc_s, rows, pair)
            te = _pair_cols(te_s, rows, pair)
            b_last = _pair_cols(be_s, first, pair)
            mx = [jnp.maximum(m[h], cm[h]) for h in range(2)]
            w_inter = [jnp.exp(m[h] - mx[h]) for h in range(2)]
            wm = jnp.exp(jnp.minimum(_pair_row_form(cc[0], cc[1]) - _pair_select(mx[0], mx[1]), 0.0))
            wm = jnp.where(incl, wm * _dot_nt(qp, _stack_heads(kp)), 0.0)
            s = s_ref[pair]
            inter = _dot(qp, s)
            intra = _dot(wm, jnp.where(diag_blocks, jnp.concatenate([vext, vext], axis=0), 0.0))
            m_new = [jnp.maximum(b_last[h] + m[h], jnp.max(te[h], axis=0, keepdims=True)) for h in range(2)]
            dprev = [jnp.exp(b_last[h] + m[h] - m_new[h]) for h in range(2)]
            wk = kp * jnp.exp(_pair_select(te[0], te[1]) - _pair_select(m_new[0], m_new[1]))
            upd = jnp.where(diag_blocks, _dot_tn(wk, vext), 0.0)
            s_ref[pair] = s * jnp.concatenate([dprev[0], dprev[0], dprev[1], dprev[1]], axis=1) + upd
            for h in range(2):
                c0 = 2 * h * LANES
                num = w_inter[h] * inter[:, c0:c0 + LANES] + intra[:, c0:c0 + LANES]
                den = w_inter[h] * inter[:, c0 + LANES:c0 + 2 * LANES] + intra[:, c0 + LANES:c0 + 2 * LANES]
                hh = num / jnp.maximum(jnp.abs(den), jnp.exp(-(b[h] + mx[h])))
                h_s[rows, (2 * pair + h) * LANES:(2 * pair + h + 1) * LANES] = hh
                m_ref[0:1, (2 * pair + h) * LANES:(2 * pair + h + 1) * LANES] = m_new[h]
        return carry

    lax.fori_loop(0, n // CHUNK, chunk_step, 0)
    ones = _head_pair_ones(LANES)
    o_ref[...] = (_head_rms(h_s[...], ng_ref[...], ones) * _sigmoid(og_ref[...])).astype(o_ref.dtype)


def mlstm_mix(p_qk, p_v, p_og, p_gates, batch, conv_w, ig_b, fg_b, norm_g):
    n, cols = p_qk.shape
    w = p_v.shape[1]
    nheads = w // LANES
    t = n // batch
    tb = min(ODD_BLOCK, t)
    nt = t // tb
    per_head = lambda v: jnp.repeat(v, LANES).reshape(1, w)
    consts = [conv_w, per_head(ig_b), per_head(fg_b), jnp.tile(norm_g, nheads).reshape(1, w)]
    seq = lambda b, i: (b * nt + i, 0)
    blk = lambda: pltpu.VMEM((tb, w), f32)
    return pl.pallas_call(
        _mlstm_body,
        grid=(batch, nt),
        in_specs=[pl.BlockSpec((tb, cols), seq), pl.BlockSpec((tb, w), seq), pl.BlockSpec((tb, w), seq),
                  pl.BlockSpec((tb, GATE_LANES), seq)] + [_const_spec(c.shape) for c in consts],
        out_specs=pl.BlockSpec((tb, w), seq),
        out_shape=jax.ShapeDtypeStruct((n, w), bf16),
        scratch_shapes=[pltpu.VMEM((SUBLANES, cols), f32), pltpu.VMEM((nheads // 2, LANES, 4 * LANES), f32),
                        pltpu.VMEM((SUBLANES, w), f32), pltpu.VMEM((tb, cols // 2), f32),
                        pltpu.VMEM((tb, cols // 2), f32)] + [blk() for _ in range(6)],
        compiler_params=_cparams(("parallel", "arbitrary")),
        name="mlstm_mix",
    )(p_qk, p_v, p_og, p_gates, *consts)


def kernel(x, norm_mix_g, norm_mlp_g, mlp_up, mlp_down, final_g, ev_w_in, ev_w_out, lru_conv_w, lru_conv_b, lru_wa,
           lru_ba, lru_wx, lru_bx, lru_lambda, rwkv_mu, rwkv_w0, rwkv_w2, rwkv_a0, rwkv_a2, rwkv_g2, rwkv_kk,
           rwkv_ka, rwkv_rk, rwkv_lnw, rwkv_lnb, od_w_in, od_w_out, gdn_conv_w, gdn_a_log, gdn_dt_bias, gdn_norm_g,
           mlstm_conv_w, mlstm_ig_b, mlstm_fg_b, mlstm_norm_g):
    batch, seq, d = x.shape
    depth = norm_mix_g.shape[0]
    xs = x.reshape(batch * seq, d)
    for l in range(depth):
        if l % 2 == 0:
            e = l // 2
            lru_cols = 2 * lru_lambda.shape[1]
            p_lru, p_rwkv = norm_proj(xs, norm_mix_g[l], ev_w_in[e].astype(bf16),
                                      (lru_cols, ev_w_in.shape[2] - lru_cols))
            ya = lru_mix(p_lru, batch, lru_conv_w[e], lru_conv_b[e], lru_wa[e], lru_ba[e], lru_wx[e], lru_bx[e],
                         lru_lambda[e])
            yb = rwkv_mix(p_rwkv, batch, rwkv_mu[e], rwkv_w0[e], rwkv_w2[e], rwkv_a0[e], rwkv_a2[e], rwkv_g2[e],
                          rwkv_kk[e], rwkv_ka[e], rwkv_rk[e], rwkv_lnw[e], rwkv_lnb[e])
            w_out = ev_w_out[e]
        else:
            o = l // 2
            w_in, widths = _odd_in_weights(od_w_in[o], gdn_conv_w.shape[2], gdn_norm_g.shape[1] * gdn_a_log.shape[1],
                                           mlstm_conv_w.shape[2], mlstm_norm_g.shape[1] * mlstm_ig_b.shape[1],
                                           gdn_a_log.shape[1], mlstm_ig_b.shape[1])
            p_qkv, p_z, p_mqk, p_mv, p_mog, p_gates = norm_proj(xs, norm_mix_g[l], w_in, widths)
            ya = gdn_mix(p_qkv, p_z, p_gates, batch, gdn_conv_w[o], gdn_a_log[o], gdn_dt_bias[o], gdn_norm_g[o])
            yb = mlstm_mix(p_mqk, p_mv, p_mog, p_gates, batch, mlstm_conv_w[o], mlstm_ig_b[o], mlstm_fg_b[o],
                           mlstm_norm_g[o])
            w_out = od_w_out[o]
        xs = out_mlp(ya, yb, xs, w_out.astype(bf16), norm_mlp_g[l], mlp_up[l].astype(bf16),
                     mlp_down[l].astype(bf16), final_g, l == depth - 1)
    return xs.reshape(batch, seq, d)


def _odd_in_weights(w, gdn_qkv, gdn_w, mlstm_qk, mlstm_w, gdn_heads, mlstm_heads):
    c = 0
    parts = {}
    for name, width in (("qkv", gdn_qkv), ("z", gdn_w), ("b", gdn_heads), ("a", gdn_heads), ("mqk", mlstm_qk),
                        ("mv", mlstm_w), ("mog", mlstm_w), ("i", mlstm_heads), ("f", mlstm_heads)):
        parts[name] = w[:, c:c + width]
        c += width
    ngate = 2 * gdn_heads + 2 * mlstm_heads
    gates = jnp.concatenate([parts["b"], parts["a"], parts["i"], parts["f"],
                             jnp.zeros((w.shape[0], GATE_LANES - ngate), w.dtype)], axis=1)
    w_in = jnp.concatenate([parts["qkv"], parts["z"], parts["mqk"], parts["mv"], parts["mog"], gates], axis=1)
    return w_in.astype(bf16), (gdn_qkv, gdn_w, mlstm_qk, mlstm_w, mlstm_w, GATE_LANES)
```

```python
import functools
import math

import jax
import jax.numpy as jnp
from jax import lax
from jax.experimental import pallas as pl
from jax.experimental.pallas import tpu as pltpu

f32 = jnp.float32
bf16 = jnp.bfloat16

LANES = 128
SUBLANES = 8
VMEM_LIMIT = 56 * 1024 * 1024

NORM_EPS = 1e-6
CONV_W = 4
LRU_C = 8.0
RWKV_HEAD = 64
RWKV_LN_EPS = 64e-5
CHUNK = 64

ROW_BLOCK = 512
SEQ_BLOCK = 512


def _cparams(sem):
    return pltpu.CompilerParams(dimension_semantics=sem, vmem_limit_bytes=VMEM_LIMIT)


def _const_spec(shape):
    nd = len(shape)
    return pl.BlockSpec(shape, lambda *_: (0,) * nd)


def _rms(x, g):
    return x * lax.rsqrt(jnp.mean(x * x, axis=-1, keepdims=True) + NORM_EPS) * g


def _softplus(x):
    return jnp.maximum(x, 0.0) + jnp.log1p(jnp.exp(-jnp.abs(x)))


def _sigmoid(x):
    return 1.0 / (1.0 + jnp.exp(-x))


def _norm_proj_body(widths, x_ref, g_ref, w_ref, *o_refs):
    h = _rms(x_ref[...], g_ref[...]).astype(bf16)
    c0 = 0
    for o_ref, wd in zip(o_refs, widths):
        o_ref[...] = jnp.dot(h, w_ref[:, c0:c0 + wd], preferred_element_type=f32).astype(o_ref.dtype)
        c0 += wd


def norm_proj(x, g, w, widths):
    n, d = x.shape
    tm = min(ROW_BLOCK, n)
    return pl.pallas_call(
        functools.partial(_norm_proj_body, widths),
        grid=(n // tm,),
        in_specs=[pl.BlockSpec((tm, d), lambda i: (i, 0)), _const_spec((1, d)), _const_spec(w.shape)],
        out_specs=[pl.BlockSpec((tm, wd), lambda i: (i, 0)) for wd in widths],
        out_shape=[jax.ShapeDtypeStruct((n, wd), f32) for wd in widths],
        compiler_params=_cparams(("parallel",)),
        name="norm_proj",
    )(x, g.reshape(1, d), w)


FF_BLOCK = 1024


def _out_mlp_body(final, ya_ref, yb_ref, x_ref, wo_ref, g_ref, wu_ref, wd_ref, gf_ref, o_ref):
    half = ya_ref.shape[1]
    y = jnp.dot(ya_ref[...], wo_ref[0:half, :], preferred_element_type=f32)
    y = y + jnp.dot(yb_ref[...], wo_ref[half:, :], preferred_element_type=f32)
    x1 = x_ref[...] + y
    h = _rms(x1, g_ref[...]).astype(bf16)
    acc = x1
    for c in range(wu_ref.shape[1] // FF_BLOCK):
        u = jnp.dot(h, wu_ref[:, c * FF_BLOCK:(c + 1) * FF_BLOCK], preferred_element_type=f32)
        u = jnp.square(jnp.maximum(u, 0.0)).astype(bf16)
        acc = acc + jnp.dot(u, wd_ref[c * FF_BLOCK:(c + 1) * FF_BLOCK, :], preferred_element_type=f32)
    if final:
        acc = _rms(acc, gf_ref[...])
    o_ref[...] = acc


def out_mlp(ya, yb, x, wo, g, wu, wd, gf, final):
    n, d = x.shape
    half = ya.shape[1]
    tm = min(ROW_BLOCK, n)
    row = lambda i: (i, 0)
    return pl.pallas_call(
        functools.partial(_out_mlp_body, final),
        grid=(n // tm,),
        in_specs=[pl.BlockSpec((tm, half), row), pl.BlockSpec((tm, half), row), pl.BlockSpec((tm, d), row),
                  _const_spec(wo.shape), _const_spec((1, d)), _const_spec(wu.shape), _const_spec(wd.shape),
                  _const_spec((1, d))],
        out_specs=pl.BlockSpec((tm, d), row),
        out_shape=jax.ShapeDtypeStruct((n, d), f32),
        compiler_params=_cparams(("parallel",)),
        name="out_mlp",
    )(ya, yb, x, wo, g.reshape(1, d), wu, wd, gf.reshape(1, d))


def _shift_rows(x, prev, k):
    rolled = pltpu.roll(x, k, axis=0)
    head = pltpu.roll(prev, k, axis=0)
    rows = lax.broadcasted_iota(jnp.int32, (SUBLANES, x.shape[1]), 0)
    top = jnp.where(rows < k, head, rolled[0:SUBLANES])
    return jnp.concatenate([top, rolled[SUBLANES:]], axis=0)


def _causal_conv(x, prev, w):
    y = x * w[CONV_W - 1:CONV_W]
    for k in range(1, CONV_W):
        y = y + _shift_rows(x, prev, k) * w[CONV_W - 1 - k:CONV_W - k]
    return y


def _scan_shift(x, s, fill):
    n, c = x.shape
    if s % SUBLANES == 0:
        return jnp.concatenate([jnp.full((s, c), fill, x.dtype), x[:n - s]], axis=0)
    rolled = pltpu.roll(x, s, axis=0)
    rows = lax.broadcasted_iota(jnp.int32, (SUBLANES, c), 0)
    top = jnp.where(rows < s, fill, rolled[0:SUBLANES])
    return jnp.concatenate([top, rolled[SUBLANES:]], axis=0)


def _linear_scan(a, u):
    s = 1
    while s < a.shape[0]:
        u = a * _scan_shift(u, s, 0.0) + u
        a = a * _scan_shift(a, s, 1.0)
        s *= 2
    return a, u


def _lru_body(p_ref, cw_ref, cb_ref, wg_ref, bg_ref, lam_ref, o_ref, xprev_ref, h_ref):
    i = pl.program_id(1)
    w = p_ref.shape[1] // 2

    @pl.when(i == 0)
    def _():
        xprev_ref[...] = jnp.zeros_like(xprev_ref)
        h_ref[...] = jnp.zeros_like(h_ref)

    gate_in = p_ref[:, 0:w]
    rec = p_ref[:, w:]
    n = rec.shape[0]
    xc = _causal_conv(rec, xprev_ref[...], cw_ref[...]) + cb_ref[...]
    xprev_ref[...] = rec[n - SUBLANES:]
    gates = jnp.dot(xc.astype(bf16), wg_ref[...], preferred_element_type=f32) + bg_ref[...]
    r_gate = _sigmoid(gates[:, 0:w])
    i_gate = _sigmoid(gates[:, w:])
    log_a = (-LRU_C) * r_gate * _softplus(-lam_ref[...])
    a = jnp.exp(log_a)
    mult = jnp.sqrt(1.0 - jnp.exp(2.0 * log_a))
    rows = lax.broadcasted_iota(jnp.int32, (n, w), 0)
    mult = jnp.where(jnp.logical_and(rows == 0, i == 0), 1.0, mult)
    u = i_gate * xc * mult
    a_cum, h = _linear_scan(a, u)
    h = h + a_cum * h_ref[...]
    h_ref[...] = h[n - 1:n]
    o_ref[...] = (h * jax.nn.gelu(gate_in, approximate=True)).astype(o_ref.dtype)


def _block_diag(wb):
    nb, d, e = wb.shape
    eye = jnp.eye(nb, dtype=wb.dtype)
    return (wb[:, :, None, :] * eye[:, None, :, None]).reshape(nb * d, nb * e)


def lru_mix(p, batch, conv_w, conv_b, wa, ba, wx, bx, lam):
    n, c2 = p.shape
    w = c2 // 2
    t = n // batch
    tb = min(SEQ_BLOCK, t)
    nt = t // tb
    wg = jnp.concatenate([_block_diag(wa), _block_diag(wx)], axis=1).astype(bf16)
    bg = jnp.concatenate([ba, bx]).reshape(1, 2 * w)
    return pl.pallas_call(
        _lru_body,
        grid=(batch, nt),
        in_specs=[pl.BlockSpec((tb, c2), lambda b, i: (b * nt + i, 0)), _const_spec((CONV_W, w)),
                  _const_spec((1, w)), _const_spec((w, 2 * w)), _const_spec((1, 2 * w)), _const_spec((1, w))],
        out_specs=pl.BlockSpec((tb, w), lambda b, i: (b * nt + i, 0)),
        out_shape=jax.ShapeDtypeStruct((n, w), bf16),
        scratch_shapes=[pltpu.VMEM((SUBLANES, w), f32), pltpu.VMEM((1, w), f32)],
        compiler_params=_cparams(("parallel", "arbitrary")),
        name="lru_mix",
    )(p, conv_w, conv_b.reshape(1, w), wg, bg, lam.reshape(1, w))


HIGHEST = lax.Precision.HIGHEST


def _dot(a, b):
    return jnp.dot(a.astype(bf16), b.astype(bf16), preferred_element_type=f32)


def _dot_nt(a, b):
    return lax.dot_general(a.astype(bf16), b.astype(bf16), (((1,), (1,)), ((), ())), preferred_element_type=f32)


def _dot_tn(a, b):
    return lax.dot_general(a.astype(bf16), b.astype(bf16), (((0,), (0,)), ((), ())), preferred_element_type=f32)


def _dot_split(x, e):
    hi = x.astype(bf16)
    lo = (x - hi.astype(f32)).astype(bf16)
    return jnp.dot(hi, e, preferred_element_type=f32) + jnp.dot(lo, e, preferred_element_type=f32)


def _iota(shape, axis):
    return lax.broadcasted_iota(jnp.int32, shape, axis)


def _tri_incl():
    return (_iota((CHUNK, CHUNK), 0) >= _iota((CHUNK, CHUNK), 1)).astype(f32)


def _chunk_cumsum(x):
    tri = _tri_incl()
    cs, ce = [], []
    for c in range(x.shape[0] // CHUNK):
        cs_c = jnp.dot(tri, x[c * CHUNK:(c + 1) * CHUNK], precision=HIGHEST, preferred_element_type=f32)
        cs.append(cs_c)
        ce.append(jnp.broadcast_to(cs_c[CHUNK - 1:CHUNK], cs_c.shape))
    return jnp.concatenate(cs, axis=0), jnp.concatenate(ce, axis=0)


def _head_pair_ones(head):
    return (_iota((LANES, LANES), 0) // head == _iota((LANES, LANES), 1) // head).astype(bf16)


def _segsum(x, e):
    return jnp.concatenate([_dot_split(x[:, j:j + LANES], e) for j in range(0, x.shape[1], LANES)], axis=1)


def _stack_heads(x):
    lo = _iota(x.shape, 1) < RWKV_HEAD
    return jnp.concatenate([jnp.where(lo, x, 0.0), jnp.where(lo, 0.0, x)], axis=0)


def _unit_lower_inverse(amats):
    shape = amats[0].shape
    ii = _iota(shape, 0)
    jj = _iota(shape, 1) % CHUNK
    eye = (ii == jj).astype(f32)
    first = jnp.logical_and(ii // 2 == jj // 2, ii > jj)
    ts = [eye - jnp.where(first, a, 0.0) for a in amats]
    b = 2
    while b < CHUNK:
        off = jnp.logical_and(ii // (2 * b) == jj // (2 * b), ii // b > jj // b)
        xs = [_dot(jnp.where(off, a, 0.0), _stack_heads(t)) for a, t in zip(amats, ts)]
        ts = [t - _dot(t, _stack_heads(x)) for t, x in zip(ts, xs)]
        b *= 2
    return ts


RWKV_BLOCK = 256
RWKV_GROUP = 2


def _rwkv_body(p_ref, mu_ref, w0_ref, w2_ref, a0_ref, a2_ref, g2_ref, kk_ref, ka_ref, rk_ref, lnw_ref, lnb_ref,
               o_ref, prev_ref, s_ref, at_s, rt_s, bt_s, kt_s, be_s, ke_s, gc_s, v_s, y_s, wh_s, uh_s, mrb_s):
    i = pl.program_id(1)
    n = p_ref.shape[0]
    w = o_ref.shape[1]
    npair = w // LANES

    @pl.when(i == 0)
    def _():
        prev_ref[...] = jnp.zeros_like(prev_ref)
        s_ref[...] = jnp.zeros_like(s_ref)

    p = p_ref[...]
    pf = p + mu_ref[...] * (_shift_rows(p, prev_ref[...], 1) - p)
    prev_ref[...] = p[n - SUBLANES:]
    r, k, v = pf[:, 0:w], pf[:, w:2 * w], pf[:, 2 * w:3 * w]
    lowrank = pf[:, 3 * w:3 * w + LANES]
    xg = pf[:, 3 * w + LANES:]
    w_log = -_softplus(-(w0_ref[...] + _dot(jnp.tanh(lowrank), w2_ref[...]))) - 0.5
    lw = -jnp.exp(w_log)
    a = _sigmoid(a0_ref[...] + _dot(lowrank, a2_ref[...]))
    g = _dot(_sigmoid(xg), g2_ref[...])
    ones = _head_pair_ones(RWKV_HEAD)
    kk = k * kk_ref[...]
    kk = kk * lax.rsqrt(_segsum(kk * kk, ones) + 1e-6)
    k = k * (1.0 + (a - 1.0) * ka_ref[...])
    bp = -(kk * a)
    cs, ce = _chunk_cumsum(lw)
    inv_g = jnp.exp(-cs)
    to_end = jnp.exp(ce - cs)
    at_s[...] = kk * jnp.exp(cs - lw)
    rt_s[...] = r * jnp.exp(cs)
    bt_s[...] = bp * inv_g
    kt_s[...] = k * inv_g
    be_s[...] = bp * to_end
    ke_s[...] = k * to_end
    gc_s[...] = jnp.exp(ce)
    v_s[...] = v

    tt = _iota((CHUNK, LANES), 0)
    ss = _iota((CHUNK, LANES), 1) % CHUNK
    strict = tt > ss
    incl = tt >= ss
    diag_blocks = (_iota((LANES, LANES), 0) < RWKV_HEAD) == (_iota((LANES, LANES), 1) < RWKV_HEAD)

    pair_cols = [slice(j * LANES, (j + 1) * LANES) for j in range(npair)]
    for c0 in range(0, n // CHUNK, RWKV_GROUP):
        units = [(slice(c * CHUNK, (c + 1) * CHUNK), cols) for c in range(c0, c0 + RWKV_GROUP) for cols in pair_cols]
        gms = [_dot_nt(jnp.concatenate([at_s[u], rt_s[u]], axis=0),
                       jnp.concatenate([_stack_heads(bt_s[u]), _stack_heads(kt_s[u])], axis=0)) for u in units]
        for u, gm in zip(units, gms):
            mrb_s[u] = jnp.where(incl, gm[CHUNK:, 0:LANES], 0.0)
        vsts = [_stack_heads(v_s[u]) for u in units]
        akvs = [_dot(jnp.where(strict, gm[0:CHUNK, LANES:], 0.0), vst) for gm, vst in zip(gms, vsts)]
        yhs = [_dot(jnp.where(incl, gm[CHUNK:, LANES:], 0.0), vst) for gm, vst in zip(gms, vsts)]
        for u, yh in zip(units, yhs):
            y_s[u] = yh
        tinvs = _unit_lower_inverse([jnp.where(strict, -gm[0:CHUNK, 0:LANES], 0.0) for gm in gms])
        wus = [_dot(t, jnp.concatenate([_stack_heads(at_s[u]), _stack_heads(akv)], axis=1))
               for u, t, akv in zip(units, tinvs, akvs)]
        for u, wu in zip(units, wus):
            wh_s[u] = wu[:, 0:LANES]
            uh_s[u] = wu[:, LANES:]

    def chunk_step(c, carry):
        r0 = pl.multiple_of(c * CHUNK, CHUNK)
        rows = pl.ds(r0, CHUNK)
        ss = [s_ref[j] for j in range(npair)]
        xs = [_dot_nt(jnp.concatenate([wh_s[rows, cols], rt_s[rows, cols]], axis=0), s)
              for cols, s in zip(pair_cols, ss)]
        us = [x[0:CHUNK] + uh_s[rows, cols] for cols, x in zip(pair_cols, xs)]
        upds = [_dot_tn(jnp.concatenate([u, v_s[rows, cols]], axis=0),
                        jnp.concatenate([be_s[rows, cols], ke_s[rows, cols]], axis=0))
                for cols, u in zip(pair_cols, us)]
        ys = [_dot(mrb_s[rows, cols], _stack_heads(u)) for cols, u in zip(pair_cols, us)]
        for j, cols in enumerate(pair_cols):
            s_ref[j] = ss[j] * gc_s[pl.ds(r0, 1), cols] + jnp.where(diag_blocks, upds[j], 0.0)
            y_s[rows, cols] = y_s[rows, cols] + xs[j][CHUNK:] + ys[j]
        return carry

    lax.fori_loop(0, n // CHUNK, chunk_step, 0)

    y = y_s[...]
    inv_head = 1.0 / RWKV_HEAD
    mean = _segsum(y, ones) * inv_head
    yc = y - mean
    var = _segsum(yc * yc, ones) * inv_head
    yn = yc * lax.rsqrt(var + RWKV_LN_EPS) * lnw_ref[...] + lnb_ref[...]
    bonus = _segsum(r * k * rk_ref[...], ones) * v
    o_ref[...] = ((yn + bonus) * g).astype(o_ref.dtype)


def rwkv_mix(p, batch, mu, w0, w2, a0, a2, g2, k_k, k_a, r_k, ln_w, ln_b):
    n, cols = p.shape
    w = w0.shape[0]
    t = n // batch
    tb = min(RWKV_BLOCK, t)
    nt = t // tb
    rank = w2.shape[0]
    zeros = jnp.zeros((LANES - rank, w), f32)
    w2p = jnp.concatenate([w2, zeros], axis=0).astype(bf16)
    a2p = jnp.concatenate([zeros, a2], axis=0).astype(bf16)
    row = lambda v: v.reshape(1, -1)
    consts = [row(mu), row(w0), w2p, row(a0), a2p, g2.astype(bf16), row(k_k), row(k_a), row(r_k), row(ln_w),
              row(ln_b)]
    seq = lambda b, i: (b * nt + i, 0)
    blk = lambda: pltpu.VMEM((tb, w), f32)
    return pl.pallas_call(
        _rwkv_body,
        grid=(batch, nt),
        in_specs=[pl.BlockSpec((tb, cols), seq)] + [_const_spec(c.shape) for c in consts],
        out_specs=pl.BlockSpec((tb, w), seq),
        out_shape=jax.ShapeDtypeStruct((n, w), bf16),
        scratch_shapes=[pltpu.VMEM((SUBLANES, cols), f32), pltpu.VMEM((w // LANES, LANES, LANES), f32)]
                       + [blk() for _ in range(12)],
        compiler_params=_cparams(("parallel", "arbitrary")),
        name="rwkv_mix",
    )(p, *consts)


ODD_BLOCK = 256
ODD_GROUP = 4
GATE_LANES = LANES


def _silu(x):
    return x * _sigmoid(x)


def _expand_gate(gates, first, nheads):
    sel = (_iota((GATE_LANES, nheads * LANES), 0) - first == _iota((GATE_LANES, nheads * LANES), 1) // LANES)
    return _dot_split(gates, sel.astype(bf16))


def _pair_cols(x_ref, rows, pair):
    c = 2 * pair * LANES
    return x_ref[rows, c:c + LANES], x_ref[rows, c + LANES:c + 2 * LANES]


def _pair_select(x0, x1):
    return jnp.where(_iota(x0.shape, 1) < CHUNK, x0, x1)


def _pair_row_form(x0, x1):
    return jnp.transpose(jnp.concatenate([x0, x1], axis=0))[0:CHUNK]


def _pair_block_rhs(x0, x1):
    return jnp.concatenate([jnp.concatenate([x0, jnp.zeros_like(x0)], axis=1),
                            jnp.concatenate([jnp.zeros_like(x1), x1], axis=1)], axis=0)


def _head_rms(o, g, ones):
    return o * lax.rsqrt(_segsum(o * o, ones) * (1.0 / LANES) + NORM_EPS) * g


def _gdn_body(p_ref, z_ref, gate_ref, cw_ref, alog_ref, dtb_ref, ng_ref, o_ref,
              prev_ref, s_ref, q_s, k_s, kb_s, vb_s, qd_s, kbg_s, kd_s, gc_s, gl_s, o_s, u_s, w_s, qk_s):
    i = pl.program_id(1)
    n = p_ref.shape[0]
    w = o_ref.shape[1]
    nheads = w // LANES

    @pl.when(i == 0)
    def _():
        prev_ref[...] = jnp.zeros_like(prev_ref)
        s_ref[...] = jnp.zeros_like(s_ref)

    p = p_ref[...]
    qkv = _silu(_causal_conv(p, prev_ref[...], cw_ref[...]))
    prev_ref[...] = p[n - SUBLANES:]
    ones = _head_pair_ones(LANES)
    q, k, v = qkv[:, 0:w], qkv[:, w:2 * w], qkv[:, 2 * w:]
    q = q * lax.rsqrt(_segsum(q * q, ones) + 1e-6) * (LANES ** -0.5)
    k = k * lax.rsqrt(_segsum(k * k, ones) + 1e-6)
    gates = gate_ref[...]
    beta = _sigmoid(_expand_gate(gates, 0, nheads))
    g = -jnp.exp(alog_ref[...]) * _softplus(_expand_gate(gates, nheads, nheads) + dtb_ref[...])
    gc, ge = _chunk_cumsum(g)
    eg = jnp.exp(gc)
    kb = k * beta
    q_s[...] = q
    k_s[...] = k
    kb_s[...] = kb
    vb_s[...] = v * beta
    qd_s[...] = q * eg
    kbg_s[...] = kb * eg
    kd_s[...] = k * jnp.exp(ge - gc)
    gc_s[...] = gc
    gl_s[...] = jnp.exp(ge)

    tt = _iota((CHUNK, LANES), 0)
    ss = _iota((CHUNK, LANES), 1) % CHUNK
    strict = tt > ss
    incl = tt >= ss

    for c0 in range(0, n // CHUNK, ODD_GROUP):
        units = [(slice(c * CHUNK, (c + 1) * CHUNK), pair) for c in range(c0, c0 + ODD_GROUP)
                 for pair in range(nheads // 2)]
        xs = [_dot_nt(jnp.concatenate(_pair_cols(kb_s, rows, pair) + _pair_cols(q_s, rows, pair), axis=0),
                      jnp.concatenate(_pair_cols(k_s, rows, pair), axis=0)) for rows, pair in units]
        amats = []
        for (rows, pair), x in zip(units, xs):
            g0, g1 = _pair_cols(gc_s, rows, pair)
            dec = jnp.exp(jnp.minimum(_pair_select(g0, g1) - _pair_row_form(g0, g1), 0.0))
            amats.append(jnp.where(strict, _pair_select(x[0:CHUNK], x[CHUNK:2 * CHUNK]) * dec, 0.0))
            qk_s[rows, pair * LANES:(pair + 1) * LANES] = jnp.where(
                incl, _pair_select(x[2 * CHUNK:3 * CHUNK], x[3 * CHUNK:]) * dec, 0.0)
        tinvs = _unit_lower_inverse(amats)
        uws = []
        for (rows, pair), tinv in zip(units, tinvs):
            vb0, vb1 = _pair_cols(vb_s, rows, pair)
            kg0, kg1 = _pair_cols(kbg_s, rows, pair)
            uws.append(_dot(tinv, _pair_block_rhs(jnp.concatenate([vb0, kg0], axis=1),
                                                  jnp.concatenate([vb1, kg1], axis=1))))
        for (rows, pair), uw in zip(units, uws):
            for h in range(2):
                cols = slice((2 * pair + h) * LANES, (2 * pair + h + 1) * LANES)
                u_s[rows, cols] = uw[:, 2 * h * LANES:(2 * h + 1) * LANES]
                w_s[rows, cols] = uw[:, (2 * h + 1) * LANES:(2 * h + 2) * LANES]

    head_cols = [slice(h * LANES, (h + 1) * LANES) for h in range(nheads)]

    def chunk_step(c, carry):
        r0 = pl.multiple_of(c * CHUNK, CHUNK)
        rows = pl.ds(r0, CHUNK)
        ss = [s_ref[h] for h in range(nheads)]
        x2s = [_dot(jnp.concatenate([w_s[rows, cols], qd_s[rows, cols]], axis=0), s)
               for cols, s in zip(head_cols, ss)]
        vns = [u_s[rows, cols] - x2[0:CHUNK] for cols, x2 in zip(head_cols, x2s)]
        upds = [_dot_tn(kd_s[rows, cols], vn) for cols, vn in zip(head_cols, vns)]
        os = [_dot(qk_s[rows, pair * LANES:(pair + 1) * LANES], _pair_block_rhs(vns[2 * pair], vns[2 * pair + 1]))
              for pair in range(nheads // 2)]
        for h, cols in enumerate(head_cols):
            s_ref[h] = ss[h] * gl_s[pl.ds(r0, 1), cols] + upds[h]
            o_s[rows, cols] = x2s[h][CHUNK:] + os[h // 2][:, (h % 2) * LANES:(h % 2 + 1) * LANES]
        return carry

    lax.fori_loop(0, n // CHUNK, chunk_step, 0)
    o_ref[...] = (_head_rms(o_s[...], ng_ref[...], ones) * _silu(z_ref[...])).astype(o_ref.dtype)


def gdn_mix(p_qkv, p_z, p_gates, batch, conv_w, a_log, dt_bias, norm_g):
    n, cols = p_qkv.shape
    w = p_z.shape[1]
    nheads = w // LANES
    t = n // batch
    tb = min(ODD_BLOCK, t)
    nt = t // tb
    per_head = lambda v: jnp.repeat(v, LANES).reshape(1, w)
    consts = [conv_w, per_head(a_log), per_head(dt_bias), jnp.tile(norm_g, nheads).reshape(1, w)]
    seq = lambda b, i: (b * nt + i, 0)
    blk = lambda: pltpu.VMEM((tb, w), f32)
    return pl.pallas_call(
        _gdn_body,
        grid=(batch, nt),
        in_specs=[pl.BlockSpec((tb, cols), seq), pl.BlockSpec((tb, w), seq), pl.BlockSpec((tb, GATE_LANES), seq)]
                 + [_const_spec(c.shape) for c in consts],
        out_specs=pl.BlockSpec((tb, w), seq),
        out_shape=jax.ShapeDtypeStruct((n, w), bf16),
        scratch_shapes=[pltpu.VMEM((SUBLANES, cols), f32), pltpu.VMEM((nheads, LANES, LANES), f32)]
                       + [blk() for _ in range(12)] + [pltpu.VMEM((tb, w // 2), f32)],
        compiler_params=_cparams(("parallel", "arbitrary")),
        name="gdn_mix",
    )(p_qkv, p_z, p_gates, *consts)


def _cummax_rows(x):
    out = []
    for c in range(x.shape[0] // CHUNK):
        y = x[c * CHUNK:(c + 1) * CHUNK]
        s = 1
        while s < CHUNK:
            y = jnp.maximum(y, _scan_shift(y, s, -jnp.inf))
            s *= 2
        out.append(y)
    return jnp.concatenate(out, axis=0)


def _mlstm_body(pqk_ref, v_ref, og_ref, gate_ref, cw_ref, igb_ref, fgb_ref, ng_ref, o_ref,
                prev_ref, s_ref, m_ref, q_s, k_s, cc_s, mx_s, wi_s, ws_s, ed_s, h_s, dp_s, intra_s, upd_s):
    i = pl.program_id(1)
    n = pqk_ref.shape[0]
    w = o_ref.shape[1]
    nheads = w // LANES
    hq = pqk_ref.shape[1] // 2

    @pl.when(i == 0)
    def _():
        prev_ref[...] = jnp.zeros_like(prev_ref)
        s_ref[...] = jnp.zeros_like(s_ref)
        m_ref[...] = jnp.zeros_like(m_ref)

    p = pqk_ref[...]
    qk = _silu(_causal_conv(p, prev_ref[...], cw_ref[...]))
    prev_ref[...] = p[n - SUBLANES:]
    q_s[...] = qk[:, 0:hq] * ((hq // nheads) ** -0.5)
    k_s[...] = qk[:, hq:]
    gates = gate_ref[...]
    i_pre = _expand_gate(gates, 2 * nheads, nheads) + igb_ref[...]
    f_pre = _expand_gate(gates, 3 * nheads, nheads) + fgb_ref[...]
    b, b_end = _chunk_cumsum(-_softplus(-f_pre))
    cc = i_pre - b
    cm = _cummax_rows(cc)
    te = b_end + cc
    nc = n // CHUNK
    m = m_ref[0:1, :]
    mx, w_inter, wk_scale = [], [], []
    for c in range(nc):
        rows = slice(c * CHUNK, (c + 1) * CHUNK)
        b_last = b_end[c * CHUNK:c * CHUNK + 1]
        m_new = jnp.maximum(b_last + m, jnp.max(te[rows], axis=0, keepdims=True))
        mx_c = jnp.maximum(m, cm[rows])
        mx.append(mx_c)
        w_inter.append(jnp.exp(m - mx_c))
        wk_scale.append(jnp.exp(te[rows] - m_new))
        dp_s[c * SUBLANES:(c + 1) * SUBLANES, :] = jnp.broadcast_to(jnp.exp(b_last + m - m_new), (SUBLANES, w))
        m = m_new
    m_ref[0:1, :] = m
    mx = jnp.concatenate(mx, axis=0)
    cc_s[...] = cc
    mx_s[...] = mx
    wi_s[...] = jnp.concatenate(w_inter, axis=0)
    ws_s[...] = jnp.concatenate(wk_scale, axis=0)
    ed_s[...] = jnp.exp(-(b + mx))

    incl = _iota((CHUNK, LANES), 0) >= _iota((CHUNK, LANES), 1) % CHUNK
    rows128 = _iota((LANES, 4 * LANES), 0) < CHUNK
    lanes512 = _iota((LANES, 4 * LANES), 1) < 2 * LANES
    diag_blocks = rows128 == lanes512
    ones_v = jnp.ones((CHUNK, LANES), f32)
    npair = nheads // 2
    pair_cols = [slice(pair * LANES, (pair + 1) * LANES) for pair in range(npair)]

    def vext(rows, pair):
        v0, v1 = _pair_cols(v_ref, rows, pair)
        return jnp.concatenate([v0, ones_v, v1, ones_v], axis=1)

    units = [(c, slice(c * CHUNK, (c + 1) * CHUNK), pair) for c in range(nc) for pair in range(npair)]
    qks = [_dot_nt(q_s[rows, pair_cols[pair]], _stack_heads(k_s[rows, pair_cols[pair]])) for _, rows, pair in units]
    wms = []
    for (_, rows, pair), qk_u in zip(units, qks):
        expo = _pair_row_form(*_pair_cols(cc_s, rows, pair)) - _pair_select(*_pair_cols(mx_s, rows, pair))
        wms.append(jnp.where(incl, jnp.exp(jnp.minimum(expo, 0.0)) * qk_u, 0.0))
    for (_, rows, pair), wm in zip(units, wms):
        ve = vext(rows, pair)
        intra_s[rows, 4 * pair * LANES:4 * (pair + 1) * LANES] = _dot(
            wm, jnp.where(diag_blocks, jnp.concatenate([ve, ve], axis=0), 0.0))
    for c, rows, pair in units:
        wk = k_s[rows, pair_cols[pair]] * _pair_select(*_pair_cols(ws_s, rows, pair))
        upd_s[c, pair] = jnp.where(diag_blocks, _dot_tn(wk, vext(rows, pair)), 0.0)

    def chunk_step(c, carry):
        r0 = pl.multiple_of(c * CHUNK, CHUNK)
        rows = pl.ds(r0, CHUNK)
        ss = [s_ref[pair] for pair in range(npair)]
        inters = [_dot(q_s[rows, pair_cols[pair]], ss[pair]) for pair in range(npair)]
        for pair in range(npair):
            dp = _pair_cols(dp_s, pl.ds(pl.multiple_of(c * SUBLANES, SUBLANES), 1), pair)
            s_ref[pair] = ss[pair] * jnp.concatenate([dp[0], dp[0], dp[1], dp[1]], axis=1) + upd_s[c, pair]
            for h in range(2):
                cols = slice((2 * pair + h) * LANES, (2 * pair + h + 1) * LANES)
                c0 = 2 * h * LANES
                i0 = 4 * pair * LANES + c0
                wi = wi_s[rows, cols]
                num = wi * inters[pair][:, c0:c0 + LANES] + intra_s[rows, i0:i0 + LANES]
                den = wi * inters[pair][:, c0 + LANES:c0 + 2 * LANES] + intra_s[rows, i0 + LANES:i0 + 2 * LANES]
                h_s[rows, cols] = num / jnp.maximum(jnp.abs(den), ed_s[rows, cols])
        return carry

    lax.fori_loop(0, nc, chunk_step, 0)
    ones = _head_pair_ones(LANES)
    o_ref[...] = (_head_rms(h_s[...], ng_ref[...], ones) * _sigmoid(og_ref[...])).astype(o_ref.dtype)


def mlstm_mix(p_qk, p_v, p_og, p_gates, batch, conv_w, ig_b, fg_b, norm_g):
    n, cols = p_qk.shape
    w = p_v.shape[1]
    nheads = w // LANES
    t = n // batch
    tb = min(ODD_BLOCK, t)
    nt = t // tb
    per_head = lambda v: jnp.repeat(v, LANES).reshape(1, w)
    consts = [conv_w, per_head(ig_b), per_head(fg_b), jnp.tile(norm_g, nheads).reshape(1, w)]
    seq = lambda b, i: (b * nt + i, 0)
    blk = lambda: pltpu.VMEM((tb, w), f32)
    return pl.pallas_call(
        _mlstm_body,
        grid=(batch, nt),
        in_specs=[pl.BlockSpec((tb, cols), seq), pl.BlockSpec((tb, w), seq), pl.BlockSpec((tb, w), seq),
                  pl.BlockSpec((tb, GATE_LANES), seq)] + [_const_spec(c.shape) for c in consts],
        out_specs=pl.BlockSpec((tb, w), seq),
        out_shape=jax.ShapeDtypeStruct((n, w), bf16),
        scratch_shapes=[pltpu.VMEM((SUBLANES, cols), f32), pltpu.VMEM((nheads // 2, LANES, 4 * LANES), f32),
                        pltpu.VMEM((SUBLANES, w), f32), pltpu.VMEM((tb, cols // 2), f32),
                        pltpu.VMEM((tb, cols // 2), f32)] + [blk() for _ in range(6)]
                       + [pltpu.VMEM((tb // CHUNK * SUBLANES, w), f32), pltpu.VMEM((tb, 2 * w), f32),
                          pltpu.VMEM((tb // CHUNK, nheads // 2, LANES, 4 * LANES), f32)],
        compiler_params=_cparams(("parallel", "arbitrary")),
        name="mlstm_mix",
    )(p_qk, p_v, p_og, p_gates, *consts)


def kernel(x, norm_mix_g, norm_mlp_g, mlp_up, mlp_down, final_g, ev_w_in, ev_w_out, lru_conv_w, lru_conv_b, lru_wa,
           lru_ba, lru_wx, lru_bx, lru_lambda, rwkv_mu, rwkv_w0, rwkv_w2, rwkv_a0, rwkv_a2, rwkv_g2, rwkv_kk,
           rwkv_ka, rwkv_rk, rwkv_lnw, rwkv_lnb, od_w_in, od_w_out, gdn_conv_w, gdn_a_log, gdn_dt_bias, gdn_norm_g,
           mlstm_conv_w, mlstm_ig_b, mlstm_fg_b, mlstm_norm_g):
    batch, seq, d = x.shape
    depth = norm_mix_g.shape[0]
    xs = x.reshape(batch * seq, d)
    for l in range(depth):
        if l % 2 == 0:
            e = l // 2
            lru_cols = 2 * lru_lambda.shape[1]
            p_lru, p_rwkv = norm_proj(xs, norm_mix_g[l], ev_w_in[e].astype(bf16),
                                      (lru_cols, ev_w_in.shape[2] - lru_cols))
            ya = lru_mix(p_lru, batch, lru_conv_w[e], lru_conv_b[e], lru_wa[e], lru_ba[e], lru_wx[e], lru_bx[e],
                         lru_lambda[e])
            yb = rwkv_mix(p_rwkv, batch, rwkv_mu[e], rwkv_w0[e], rwkv_w2[e], rwkv_a0[e], rwkv_a2[e], rwkv_g2[e],
                          rwkv_kk[e], rwkv_ka[e], rwkv_rk[e], rwkv_lnw[e], rwkv_lnb[e])
            w_out = ev_w_out[e]
        else:
            o = l // 2
            w_in, widths = _odd_in_weights(od_w_in[o], gdn_conv_w.shape[2], gdn_norm_g.shape[1] * gdn_a_log.shape[1],
                                           mlstm_conv_w.shape[2], mlstm_norm_g.shape[1] * mlstm_ig_b.shape[1],
                                           gdn_a_log.shape[1], mlstm_ig_b.shape[1])
            p_qkv, p_z, p_mqk, p_mv, p_mog, p_gates = norm_proj(xs, norm_mix_g[l], w_in, widths)
            ya = gdn_mix(p_qkv, p_z, p_gates, batch, gdn_conv_w[o], gdn_a_log[o], gdn_dt_bias[o], gdn_norm_g[o])
            yb = mlstm_mix(p_mqk, p_mv, p_mog, p_gates, batch, mlstm_conv_w[o], mlstm_ig_b[o], mlstm_fg_b[o],
                           mlstm_norm_g[o])
            w_out = od_w_out[o]
        xs = out_mlp(ya, yb, xs, w_out.astype(bf16), norm_mlp_g[l], mlp_up[l].astype(bf16),
                     mlp_down[l].astype(bf16), final_g, l == depth - 1)
    return xs.reshape(batch, seq, d)


def _odd_in_weights(w, gdn_qkv, gdn_w, mlstm_qk, mlstm_w, gdn_heads, mlstm_heads):
    c = 0
    parts = {}
    for name, width in (("qkv", gdn_qkv), ("z", gdn_w), ("b", gdn_heads), ("a", gdn_heads), ("mqk", mlstm_qk),
                        ("mv", mlstm_w), ("mog", mlstm_w), ("i", mlstm_heads), ("f", mlstm_heads)):
        parts[name] = w[:, c:c + width]
        c += width
    ngate = 2 * gdn_heads + 2 * mlstm_heads
    gates = jnp.concatenate([parts["b"], parts["a"], parts["i"], parts["f"],
                             jnp.zeros((w.shape[0], GATE_LANES - ngate), w.dtype)], axis=1)
    w_in = jnp.concatenate([parts["qkv"], parts["z"], parts["mqk"], parts["mv"], parts["mog"], gates], axis=1)
    return w_in.astype(bf16), (gdn_qkv, gdn_w, mlstm_qk, mlstm_w, mlstm_w, GATE_LANES)
```

```python
import functools
import math

import jax
import jax.numpy as jnp
from jax import lax
from jax.experimental import pallas as pl
from jax.experimental.pallas import tpu as pltpu

f32 = jnp.float32
bf16 = jnp.bfloat16

LANES = 128
SUBLANES = 8
VMEM_LIMIT = 56 * 1024 * 1024

NORM_EPS = 1e-6
CONV_W = 4
LRU_C = 8.0
RWKV_HEAD = 64
RWKV_LN_EPS = 64e-5
CHUNK = 64

ROW_BLOCK = 512
SEQ_BLOCK = 512


def _cparams(sem):
    return pltpu.CompilerParams(dimension_semantics=sem, vmem_limit_bytes=VMEM_LIMIT)


def _const_spec(shape):
    nd = len(shape)
    return pl.BlockSpec(shape, lambda *_: (0,) * nd)


def _rms(x, g):
    return x * lax.rsqrt(jnp.mean(x * x, axis=-1, keepdims=True) + NORM_EPS) * g


def _softplus(x):
    return jnp.maximum(x, 0.0) + jnp.log(1.0 + jnp.exp(-jnp.abs(x)))


def _sigmoid(x):
    return 0.5 * jnp.tanh(0.5 * x) + 0.5


def _norm_proj_body(widths, x_ref, g_ref, w_ref, *o_refs):
    h = _rms(x_ref[...], g_ref[...]).astype(bf16)
    c0 = 0
    for o_ref, wd in zip(o_refs, widths):
        o_ref[...] = jnp.dot(h, w_ref[:, c0:c0 + wd], preferred_element_type=f32).astype(o_ref.dtype)
        c0 += wd


def norm_proj(x, g, w, widths):
    n, d = x.shape
    tm = min(ROW_BLOCK, n)
    return pl.pallas_call(
        functools.partial(_norm_proj_body, widths),
        grid=(n // tm,),
        in_specs=[pl.BlockSpec((tm, d), lambda i: (i, 0)), _const_spec((1, d)), _const_spec(w.shape)],
        out_specs=[pl.BlockSpec((tm, wd), lambda i: (i, 0)) for wd in widths],
        out_shape=[jax.ShapeDtypeStruct((n, wd), f32) for wd in widths],
        compiler_params=_cparams(("parallel",)),
        name="norm_proj",
    )(x, g.reshape(1, d), w)


FF_BLOCK = 1024


def _out_mlp_body(final, ya_ref, yb_ref, x_ref, wo_ref, g_ref, wu_ref, wd_ref, gf_ref, o_ref):
    half = ya_ref.shape[1]
    y = jnp.dot(ya_ref[...], wo_ref[0:half, :], preferred_element_type=f32)
    y = y + jnp.dot(yb_ref[...], wo_ref[half:, :], preferred_element_type=f32)
    x1 = x_ref[...] + y
    h = _rms(x1, g_ref[...]).astype(bf16)
    acc = x1
    for c in range(wu_ref.shape[1] // FF_BLOCK):
        u = jnp.dot(h, wu_ref[:, c * FF_BLOCK:(c + 1) * FF_BLOCK], preferred_element_type=f32)
        u = jnp.square(jnp.maximum(u, 0.0)).astype(bf16)
        acc = acc + jnp.dot(u, wd_ref[c * FF_BLOCK:(c + 1) * FF_BLOCK, :], preferred_element_type=f32)
    if final:
        acc = _rms(acc, gf_ref[...])
    o_ref[...] = acc


def out_mlp(ya, yb, x, wo, g, wu, wd, gf, final):
    n, d = x.shape
    half = ya.shape[1]
    tm = min(ROW_BLOCK, n)
    row = lambda i: (i, 0)
    return pl.pallas_call(
        functools.partial(_out_mlp_body, final),
        grid=(n // tm,),
        in_specs=[pl.BlockSpec((tm, half), row), pl.BlockSpec((tm, half), row), pl.BlockSpec((tm, d), row),
                  _const_spec(wo.shape), _const_spec((1, d)), _const_spec(wu.shape), _const_spec(wd.shape),
                  _const_spec((1, d))],
        out_specs=pl.BlockSpec((tm, d), row),
        out_shape=jax.ShapeDtypeStruct((n, d), f32),
        compiler_params=_cparams(("parallel",)),
        name="out_mlp",
    )(ya, yb, x, wo, g.reshape(1, d), wu, wd, gf.reshape(1, d))


def _shift_rows(x, prev, k):
    rolled = pltpu.roll(x, k, axis=0)
    head = pltpu.roll(prev, k, axis=0)
    rows = lax.broadcasted_iota(jnp.int32, (SUBLANES, x.shape[1]), 0)
    top = jnp.where(rows < k, head, rolled[0:SUBLANES])
    return jnp.concatenate([top, rolled[SUBLANES:]], axis=0)


def _causal_conv(x, prev, w):
    y = x * w[CONV_W - 1:CONV_W]
    for k in range(1, CONV_W):
        y = y + _shift_rows(x, prev, k) * w[CONV_W - 1 - k:CONV_W - k]
    return y


def _scan_shift(x, s, fill):
    n, c = x.shape
    if s % SUBLANES == 0:
        return jnp.concatenate([jnp.full((s, c), fill, x.dtype), x[:n - s]], axis=0)
    rolled = pltpu.roll(x, s, axis=0)
    rows = lax.broadcasted_iota(jnp.int32, (SUBLANES, c), 0)
    top = jnp.where(rows < s, fill, rolled[0:SUBLANES])
    return jnp.concatenate([top, rolled[SUBLANES:]], axis=0)


def _linear_scan(a, u):
    s = 1
    while s < a.shape[0]:
        u = a * _scan_shift(u, s, 0.0) + u
        a = a * _scan_shift(a, s, 1.0)
        s *= 2
    return a, u


def _lru_body(p_ref, cw_ref, cb_ref, wg_ref, bg_ref, lam_ref, o_ref, xprev_ref, h_ref):
    i = pl.program_id(1)
    w = p_ref.shape[1] // 2

    @pl.when(i == 0)
    def _():
        xprev_ref[...] = jnp.zeros_like(xprev_ref)
        h_ref[...] = jnp.zeros_like(h_ref)

    gate_in = p_ref[:, 0:w]
    rec = p_ref[:, w:]
    n = rec.shape[0]
    xc = _causal_conv(rec, xprev_ref[...], cw_ref[...]) + cb_ref[...]
    xprev_ref[...] = rec[n - SUBLANES:]
    gates = jnp.dot(xc.astype(bf16), wg_ref[...], preferred_element_type=f32) + bg_ref[...]
    r_gate = _sigmoid(gates[:, 0:w])
    i_gate = _sigmoid(gates[:, w:])
    log_a = (-LRU_C) * r_gate * _softplus(-lam_ref[...])
    a = jnp.exp(log_a)
    mult = jnp.sqrt(1.0 - jnp.exp(2.0 * log_a))
    rows = lax.broadcasted_iota(jnp.int32, (n, w), 0)
    mult = jnp.where(jnp.logical_and(rows == 0, i == 0), 1.0, mult)
    u = i_gate * xc * mult
    a_cum, h = _linear_scan(a, u)
    h = h + a_cum * h_ref[...]
    h_ref[...] = h[n - 1:n]
    o_ref[...] = (h * jax.nn.gelu(gate_in, approximate=True)).astype(o_ref.dtype)


def _block_diag(wb):
    nb, d, e = wb.shape
    eye = jnp.eye(nb, dtype=wb.dtype)
    return (wb[:, :, None, :] * eye[:, None, :, None]).reshape(nb * d, nb * e)


def lru_mix(p, batch, conv_w, conv_b, wa, ba, wx, bx, lam):
    n, c2 = p.shape
    w = c2 // 2
    t = n // batch
    tb = min(SEQ_BLOCK, t)
    nt = t // tb
    wg = jnp.concatenate([_block_diag(wa), _block_diag(wx)], axis=1).astype(bf16)
    bg = jnp.concatenate([ba, bx]).reshape(1, 2 * w)
    return pl.pallas_call(
        _lru_body,
        grid=(batch, nt),
        in_specs=[pl.BlockSpec((tb, c2), lambda b, i: (b * nt + i, 0)), _const_spec((CONV_W, w)),
                  _const_spec((1, w)), _const_spec((w, 2 * w)), _const_spec((1, 2 * w)), _const_spec((1, w))],
        out_specs=pl.BlockSpec((tb, w), lambda b, i: (b * nt + i, 0)),
        out_shape=jax.ShapeDtypeStruct((n, w), bf16),
        scratch_shapes=[pltpu.VMEM((SUBLANES, w), f32), pltpu.VMEM((1, w), f32)],
        compiler_params=_cparams(("parallel", "arbitrary")),
        name="lru_mix",
    )(p, conv_w, conv_b.reshape(1, w), wg, bg, lam.reshape(1, w))


def _dot(a, b):
    return jnp.dot(a.astype(bf16), b.astype(bf16), preferred_element_type=f32)


def _dot_nt(a, b):
    return lax.dot_general(a.astype(bf16), b.astype(bf16), (((1,), (1,)), ((), ())), preferred_element_type=f32)


def _dot_tn(a, b):
    return lax.dot_general(a.astype(bf16), b.astype(bf16), (((0,), (0,)), ((), ())), preferred_element_type=f32)


def _dot_split(x, e, terms=2):
    out = None
    for _ in range(terms):
        hi = x.astype(bf16)
        x = x - hi.astype(f32)
        part = jnp.dot(hi, e, preferred_element_type=f32)
        out = part if out is None else out + part
    return out


def _iota(shape, axis):
    return lax.broadcasted_iota(jnp.int32, shape, axis)


def _tri_incl():
    return (_iota((CHUNK, CHUNK), 0) >= _iota((CHUNK, CHUNK), 1)).astype(bf16)


def _chunk_cumsum(x):
    tri = _tri_incl()
    cs, ce = [], []
    for c in range(x.shape[0] // CHUNK):
        rest = x[c * CHUNK:(c + 1) * CHUNK]
        cs_c = None
        for _ in range(3):
            hi = rest.astype(bf16)
            rest = rest - hi.astype(f32)
            part = jnp.dot(tri, hi, preferred_element_type=f32)
            cs_c = part if cs_c is None else cs_c + part
        cs.append(cs_c)
        ce.append(jnp.broadcast_to(cs_c[CHUNK - 1:CHUNK], cs_c.shape))
    return jnp.concatenate(cs, axis=0), jnp.concatenate(ce, axis=0)


def _head_pair_ones(head):
    return (_iota((LANES, LANES), 0) // head == _iota((LANES, LANES), 1) // head).astype(bf16)


def _segsum(x, e):
    return jnp.concatenate([_dot_split(x[:, j:j + LANES], e) for j in range(0, x.shape[1], LANES)], axis=1)


def _stack_heads(x):
    lo = _iota(x.shape, 1) < RWKV_HEAD
    return jnp.concatenate([jnp.where(lo, x, 0.0), jnp.where(lo, 0.0, x)], axis=0)


def _unit_lower_inverse(amats):
    shape = amats[0].shape
    ii = _iota(shape, 0)
    jj = _iota(shape, 1) % CHUNK
    eye = (ii == jj).astype(f32)
    first = jnp.logical_and(ii // 2 == jj // 2, ii > jj)
    ts = [eye - jnp.where(first, a, 0.0) for a in amats]
    abf = [a.astype(bf16) for a in amats]
    b = 2
    while b < CHUNK:
        off = jnp.logical_and(ii // (2 * b) == jj // (2 * b), ii // b > jj // b)
        xs = [_dot(a, _stack_heads(t)) for a, t in zip(abf, ts)]
        ts = [t - jnp.where(off, _dot(t, _stack_heads(x)), 0.0) for t, x in zip(ts, xs)]
        b *= 2
    return ts


RWKV_BLOCK = 128
UNIT_GROUP = 32


def _load_rows(ref):
    return jnp.concatenate([ref[b] for b in range(ref.shape[0])], axis=0)


def _store_rows(ref, x):
    tb = ref.shape[1]
    for b in range(ref.shape[0]):
        ref[b] = x[b * tb:(b + 1) * tb].astype(ref.dtype)


def _per_batch(fn, x, prev_ref):
    nb = prev_ref.shape[0]
    tb = x.shape[0] // nb
    out = []
    for b in range(nb):
        xb = x[b * tb:(b + 1) * tb]
        out.append(fn(xb, prev_ref[b]))
        prev_ref[b] = xb[tb - SUBLANES:]
    return jnp.concatenate(out, axis=0)


def _rwkv_body(p_ref, mu_ref, w0_ref, w2_ref, a0_ref, a2_ref, g2_ref, kk_ref, ka_ref, rk_ref, lnw_ref, lnb_ref,
               o_ref, prev_ref, s_ref, at_s, rt_s, bt_s, kt_s, gc_s, v_s, y_s, wh_s, uh_s, mrb_s):
    nb, tb = p_ref.shape[0], p_ref.shape[1]
    n = nb * tb
    w = o_ref.shape[2]
    npair = w // LANES

    @pl.when(pl.program_id(0) == 0)
    def _():
        prev_ref[...] = jnp.zeros_like(prev_ref)
        s_ref[...] = jnp.zeros_like(s_ref)

    p = _load_rows(p_ref)
    pf = p + mu_ref[...] * (_per_batch(lambda x, prev: _shift_rows(x, prev, 1), p, prev_ref) - p)
    r, k, v = pf[:, 0:w], pf[:, w:2 * w], pf[:, 2 * w:3 * w]
    lowrank = pf[:, 3 * w:3 * w + LANES]
    xg = pf[:, 3 * w + LANES:]
    w_log = -_softplus(-(w0_ref[...] + _dot(jnp.tanh(lowrank), w2_ref[...]))) - 0.5
    lw = -jnp.exp(w_log)
    a = _sigmoid(a0_ref[...] + _dot(lowrank, a2_ref[...]))
    g = _dot(_sigmoid(xg), g2_ref[...])
    ones = _head_pair_ones(RWKV_HEAD)
    kk = k * kk_ref[...]
    kk = kk * lax.rsqrt(_segsum(kk * kk, ones) + 1e-6)
    k = k * (1.0 + (a - 1.0) * ka_ref[...])
    cs, ce = _chunk_cumsum(lw)
    inv_g = jnp.exp(-cs)
    at_s[...] = kk * jnp.exp(cs - lw)
    rt_s[...] = r * jnp.exp(cs)
    bt_s[...] = -(kk * a) * inv_g
    kt_s[...] = k * inv_g
    gc_s[...] = jnp.exp(ce)
    v_s[...] = v

    tt = _iota((CHUNK, LANES), 0)
    ss = _iota((CHUNK, LANES), 1) % CHUNK
    strict = tt > ss
    incl = tt >= ss
    diag_blocks = (_iota((LANES, LANES), 0) < RWKV_HEAD) == (_iota((LANES, LANES), 1) < RWKV_HEAD)

    pair_cols = [slice(j * LANES, (j + 1) * LANES) for j in range(npair)]
    all_units = [(slice(c * CHUNK, (c + 1) * CHUNK), cols) for c in range(n // CHUNK) for cols in pair_cols]
    for u0 in range(0, len(all_units), UNIT_GROUP):
        units = all_units[u0:u0 + UNIT_GROUP]
        gms =[_dot_nt(jnp.concatenate([at_s[u], rt_s[u]], axis=0),
                       jnp.concatenate([_stack_heads(bt_s[u]), _stack_heads(kt_s[u])], axis=0)) for u in units]
        for u, gm in zip(units, gms):
            mrb_s[u] = jnp.where(incl, gm[CHUNK:, 0:LANES], 0.0)
        vsts = [_stack_heads(v_s[u]) for u in units]
        akvs = [_dot(jnp.where(strict, gm[0:CHUNK, LANES:], 0.0), vst) for gm, vst in zip(gms, vsts)]
        yhs = [_dot(jnp.where(incl, gm[CHUNK:, LANES:], 0.0), vst) for gm, vst in zip(gms, vsts)]
        for u, yh in zip(units, yhs):
            y_s[u] = yh
        tinvs = _unit_lower_inverse([jnp.where(strict, -gm[0:CHUNK, 0:LANES], 0.0) for gm in gms])
        wus = [_dot(t, jnp.concatenate([_stack_heads(at_s[u]), _stack_heads(akv)], axis=1))
               for u, t, akv in zip(units, tinvs, akvs)]
        for u, wu in zip(units, wus):
            wh_s[u] = wu[:, 0:LANES]
            uh_s[u] = wu[:, LANES:]

    def chunk_step(c, carry):
        starts = [pl.multiple_of(c * CHUNK + b * tb, CHUNK) for b in range(nb)]
        chains = [(b * npair + j, pl.ds(starts[b], CHUNK), cols) for b in range(nb) for j, cols in enumerate(pair_cols)]
        ss = [s_ref[si] for si, _, _ in chains]
        xs = [_dot_nt(jnp.concatenate([wh_s[rows, cols], rt_s[rows, cols]], axis=0), s)
              for (_, rows, cols), s in zip(chains, ss)]
        us = [x[0:CHUNK] + uh_s[rows, cols] for (_, rows, cols), x in zip(chains, xs)]
        upds = [_dot_tn(jnp.concatenate([u, v_s[rows, cols]], axis=0),
                        jnp.concatenate([bt_s[rows, cols], kt_s[rows, cols]], axis=0))
                for (_, rows, cols), u in zip(chains, us)]
        ys = [_dot(mrb_s[rows, cols], _stack_heads(u)) for (_, rows, cols), u in zip(chains, us)]
        for (si, rows, cols), s, x, upd, y in zip(chains, ss, xs, upds, ys):
            s_ref[si] = (s + jnp.where(diag_blocks, upd, 0.0)) * gc_s[pl.ds(starts[si // npair], 1), cols]
            y_s[rows, cols] = y_s[rows, cols] + x[CHUNK:] + y
        return carry

    lax.fori_loop(0, tb // CHUNK, chunk_step, 0)

    y = y_s[...]
    inv_head = 1.0 / RWKV_HEAD
    mean = _segsum(y, ones) * inv_head
    yc = y - mean
    var = _segsum(yc * yc, ones) * inv_head
    yn = yc * lax.rsqrt(var + RWKV_LN_EPS) * lnw_ref[...] + lnb_ref[...]
    bonus = _segsum(r * k * rk_ref[...], ones) * v
    _store_rows(o_ref, (yn + bonus) * g)


def _seq_specs(batch, tb, widths):
    return [pl.BlockSpec((batch, tb, wd), lambda i: (0, i, 0)) for wd in widths]


def rwkv_mix(p, batch, mu, w0, w2, a0, a2, g2, k_k, k_a, r_k, ln_w, ln_b):
    n, cols = p.shape
    w = w0.shape[0]
    t = n // batch
    tb = min(RWKV_BLOCK, t)
    rank = w2.shape[0]
    zeros = jnp.zeros((LANES - rank, w), f32)
    w2p = jnp.concatenate([w2, zeros], axis=0).astype(bf16)
    a2p = jnp.concatenate([zeros, a2], axis=0).astype(bf16)
    row = lambda v: v.reshape(1, -1)
    consts = [row(mu), row(w0), w2p, row(a0), a2p, g2.astype(bf16), row(k_k), row(k_a), row(r_k), row(ln_w),
              row(ln_b)]
    blk = lambda: pltpu.VMEM((batch * tb, w), f32)
    out = pl.pallas_call(
        _rwkv_body,
        grid=(t // tb,),
        in_specs=_seq_specs(batch, tb, [cols]) + [_const_spec(c.shape) for c in consts],
        out_specs=_seq_specs(batch, tb, [w])[0],
        out_shape=jax.ShapeDtypeStruct((batch, t, w), bf16),
        scratch_shapes=[pltpu.VMEM((batch, SUBLANES, cols), f32),
                        pltpu.VMEM((batch * w // LANES, LANES, LANES), f32)] + [blk() for _ in range(10)],
        compiler_params=_cparams(("arbitrary",)),
        name="rwkv_mix",
    )(p.reshape(batch, t, cols), *consts)
    return out.reshape(n, w)


ODD_BLOCK = 128
GATE_LANES = LANES


def _silu(x):
    return x * _sigmoid(x)


def _expand_gate(gates, first, nheads, terms=2):
    sel = (_iota((GATE_LANES, nheads * LANES), 0) - first == _iota((GATE_LANES, nheads * LANES), 1) // LANES)
    return _dot_split(gates, sel.astype(bf16), terms)


def _gate_row(v, first):
    return jnp.zeros((1, GATE_LANES), f32).at[0, first:first + v.shape[0]].set(v)


def _pair_cols(x_ref, rows, pair):
    c = 2 * pair * LANES
    return x_ref[rows, c:c + LANES], x_ref[rows, c + LANES:c + 2 * LANES]


def _pair_select(x0, x1):
    return jnp.where(_iota(x0.shape, 1) < CHUNK, x0, x1)


def _pair_row_form(x0, x1):
    return jnp.transpose(jnp.concatenate([x0, x1], axis=0))[0:CHUNK]


def _pair_block_rhs(x0, x1):
    return jnp.concatenate([jnp.concatenate([x0, jnp.zeros_like(x0)], axis=1),
                            jnp.concatenate([jnp.zeros_like(x1), x1], axis=1)], axis=0)


def _head_rms(o, g, ones):
    return o * lax.rsqrt(_segsum(o * o, ones) * (1.0 / LANES) + NORM_EPS) * g


def _gdn_body(p_ref, z_ref, gate_ref, cw_ref, alog_ref, dtb_ref, ng_ref, o_ref,
              prev_ref, s_ref, q_s, k_s, kb_s, vb_s, qd_s, kbg_s, kd_s, gc_s, gl_s, o_s, u_s, w_s, qk_s):
    nb, tb = p_ref.shape[0], p_ref.shape[1]
    n = nb * tb
    w = o_ref.shape[2]
    nheads = w // LANES

    @pl.when(pl.program_id(0) == 0)
    def _():
        prev_ref[...] = jnp.zeros_like(prev_ref)
        s_ref[...] = jnp.zeros_like(s_ref)

    qkv = _silu(_per_batch(lambda x, prev: _causal_conv(x, prev, cw_ref[...]), _load_rows(p_ref), prev_ref))
    ones = _head_pair_ones(LANES)
    q, k, v = qkv[:, 0:w], qkv[:, w:2 * w], qkv[:, 2 * w:]
    q = q * lax.rsqrt(_segsum(q * q, ones) + 1e-6) * (LANES ** -0.5)
    k = k * lax.rsqrt(_segsum(k * k, ones) + 1e-6)
    gates = _load_rows(gate_ref)
    beta = _expand_gate(_sigmoid(gates), 0, nheads)
    gc_c, ge_c = _chunk_cumsum(-jnp.exp(alog_ref[...]) * _softplus(gates + dtb_ref[...]))
    eg = _expand_gate(jnp.exp(gc_c), nheads, nheads)
    kb = k * beta
    q_s[...] = q
    k_s[...] = k
    kb_s[...] = kb
    vb_s[...] = v * beta
    qd_s[...] = q * eg
    kbg_s[...] = kb * eg
    kd_s[...] = k * _expand_gate(jnp.exp(ge_c - gc_c), nheads, nheads)
    gc_s[...] = _expand_gate(gc_c, nheads, nheads, terms=3)
    gl_s[...] = _expand_gate(jnp.exp(ge_c), nheads, nheads)

    tt = _iota((CHUNK, LANES), 0)
    ss = _iota((CHUNK, LANES), 1) % CHUNK
    strict = tt > ss
    incl = tt >= ss

    all_units = [(slice(c * CHUNK, (c + 1) * CHUNK), pair) for c in range(n // CHUNK) for pair in range(nheads // 2)]
    for u0 in range(0, len(all_units), UNIT_GROUP):
        units = all_units[u0:u0 + UNIT_GROUP]
        xs = [_dot_nt(jnp.concatenate(_pair_cols(kb_s, rows, pair) + _pair_cols(q_s, rows, pair), axis=0),
                      jnp.concatenate(_pair_cols(k_s, rows, pair), axis=0)) for rows, pair in units]
        amats = []
        for (rows, pair), x in zip(units, xs):
            g0, g1 = _pair_cols(gc_s, rows, pair)
            dec = jnp.exp(jnp.minimum(_pair_select(g0, g1) - _pair_row_form(g0, g1), 0.0))
            amats.append(jnp.where(strict, _pair_select(x[0:CHUNK], x[CHUNK:2 * CHUNK]) * dec, 0.0))
            qk_s[rows, pair * LANES:(pair + 1) * LANES] = jnp.where(
                incl, _pair_select(x[2 * CHUNK:3 * CHUNK], x[3 * CHUNK:]) * dec, 0.0)
        tinvs = _unit_lower_inverse(amats)
        uws = []
        for (rows, pair), tinv in zip(units, tinvs):
            vb0, vb1 = _pair_cols(vb_s, rows, pair)
            kg0, kg1 = _pair_cols(kbg_s, rows, pair)
            uws.append(_dot(tinv, _pair_block_rhs(jnp.concatenate([vb0, kg0], axis=1),
                                                  jnp.concatenate([vb1, kg1], axis=1))))
        for (rows, pair), uw in zip(units, uws):
            for h in range(2):
                cols = slice((2 * pair + h) * LANES, (2 * pair + h + 1) * LANES)
                u_s[rows, cols] = uw[:, 2 * h * LANES:(2 * h + 1) * LANES]
                w_s[rows, cols] = uw[:, (2 * h + 1) * LANES:(2 * h + 2) * LANES]

    head_cols = [slice(h * LANES, (h + 1) * LANES) for h in range(nheads)]

    def chunk_step(c, carry):
        starts = [pl.multiple_of(c * CHUNK + b * tb, CHUNK) for b in range(nb)]
        chains = [(b, h, pl.ds(starts[b], CHUNK), cols) for b in range(nb) for h, cols in enumerate(head_cols)]
        ss = [s_ref[b * nheads + h] for b, h, _, _ in chains]
        x2s = [_dot(jnp.concatenate([w_s[rows, cols], qd_s[rows, cols]], axis=0), s)
               for (_, _, rows, cols), s in zip(chains, ss)]
        vns = [u_s[rows, cols] - x2[0:CHUNK] for (_, _, rows, cols), x2 in zip(chains, x2s)]
        upds = [_dot_tn(kd_s[rows, cols], vn) for (_, _, rows, cols), vn in zip(chains, vns)]
        os = [_dot(qk_s[pl.ds(starts[b], CHUNK), pair * LANES:(pair + 1) * LANES],
                   _pair_block_rhs(vns[b * nheads + 2 * pair], vns[b * nheads + 2 * pair + 1]))
              for b in range(nb) for pair in range(nheads // 2)]
        for ci, (b, h, rows, cols) in enumerate(chains):
            s_ref[ci] = ss[ci] * gl_s[pl.ds(starts[b], 1), cols] + upds[ci]
            o_s[rows, cols] = x2s[ci][CHUNK:] + os[ci // 2][:, (h % 2) * LANES:(h % 2 + 1) * LANES]
        return carry

    lax.fori_loop(0, tb // CHUNK, chunk_step, 0)
    _store_rows(o_ref, _head_rms(o_s[...], ng_ref[...], ones) * _silu(_load_rows(z_ref)))


def gdn_mix(p_qkv, p_z, p_gates, batch, conv_w, a_log, dt_bias, norm_g):
    n, cols = p_qkv.shape
    w = p_z.shape[1]
    nheads = w // LANES
    t = n // batch
    tb = min(ODD_BLOCK, t)
    consts = [conv_w, _gate_row(a_log, nheads), _gate_row(dt_bias, nheads), jnp.tile(norm_g, nheads).reshape(1, w)]
    blk = lambda: pltpu.VMEM((batch * tb, w), f32)
    out = pl.pallas_call(
        _gdn_body,
        grid=(t // tb,),
        in_specs=_seq_specs(batch, tb, [cols, w, GATE_LANES]) + [_const_spec(c.shape) for c in consts],
        out_specs=_seq_specs(batch, tb, [w])[0],
        out_shape=jax.ShapeDtypeStruct((batch, t, w), bf16),
        scratch_shapes=[pltpu.VMEM((batch, SUBLANES, cols), f32), pltpu.VMEM((batch * nheads, LANES, LANES), f32)]
                       + [blk() for _ in range(12)] + [pltpu.VMEM((batch * tb, w // 2), f32)],
        compiler_params=_cparams(("arbitrary",)),
        name="gdn_mix",
    )(p_qkv.reshape(batch, t, cols), p_z.reshape(batch, t, w), p_gates.reshape(batch, t, GATE_LANES), *consts)
    return out.reshape(n, w)


def _cummax_rows(x):
    out = []
    for c in range(x.shape[0] // CHUNK):
        y = x[c * CHUNK:(c + 1) * CHUNK]
        s = 1
        while s < CHUNK:
            y = jnp.maximum(y, _scan_shift(y, s, -jnp.inf))
            s *= 2
        out.append(y)
    return jnp.concatenate(out, axis=0)


def _mlstm_body(pqk_ref, v_ref, og_ref, gate_ref, cw_ref, igb_ref, fgb_ref, ng_ref, o_ref,
                prev_ref, s_ref, m_ref, q_s, k_s, cc_s, mx_s, wi_s, ws_s, ed_s, h_s, v_s, dp_s, intra_s, upd_s):
    nb, tb = pqk_ref.shape[0], pqk_ref.shape[1]
    n = nb * tb
    w = o_ref.shape[2]
    nheads = w // LANES
    hq = pqk_ref.shape[2] // 2

    @pl.when(pl.program_id(0) == 0)
    def _():
        prev_ref[...] = jnp.zeros_like(prev_ref)
        s_ref[...] = jnp.zeros_like(s_ref)
        m_ref[...] = jnp.zeros_like(m_ref)

    qk = _silu(_per_batch(lambda x, prev: _causal_conv(x, prev, cw_ref[...]), _load_rows(pqk_ref), prev_ref))
    q_s[...] = qk[:, 0:hq] * ((hq // nheads) ** -0.5)
    k_s[...] = qk[:, hq:]
    v_s[...] = _load_rows(v_ref)
    gates = _load_rows(gate_ref)
    first = 2 * nheads
    i_pre = gates + igb_ref[...]
    f_pre = pltpu.roll(gates, GATE_LANES - nheads, axis=1) + fgb_ref[...]
    b, b_end = _chunk_cumsum(-_softplus(-f_pre))
    cc = i_pre - b
    cm = _cummax_rows(cc)
    te = b_end + cc
    nc = n // CHUNK
    ncb = tb // CHUNK
    mx, w_inter, wk_scale, dprev = [], [], [], []
    for c in range(nc):
        bi = c // ncb
        m = m_ref[bi * SUBLANES:bi * SUBLANES + 1, :] if c % ncb == 0 else m
        rows = slice(c * CHUNK, (c + 1) * CHUNK)
        b_last = b_end[c * CHUNK:c * CHUNK + 1]
        m_new = jnp.maximum(b_last + m, jnp.max(te[rows], axis=0, keepdims=True))
        mx_c = jnp.maximum(m, cm[rows])
        mx.append(mx_c)
        w_inter.append(jnp.exp(m - mx_c))
        wk_scale.append(jnp.exp(te[rows] - m_new))
        dprev.append(jnp.broadcast_to(jnp.exp(b_last + m - m_new), (SUBLANES, GATE_LANES)))
        m = m_new
        if c % ncb == ncb - 1:
            m_ref[bi * SUBLANES:bi * SUBLANES + 1, :] = m
    mx = jnp.concatenate(mx, axis=0)
    cc_s[...] = _expand_gate(cc, first, nheads, terms=3)
    mx_s[...] = _expand_gate(mx, first, nheads, terms=3)
    wi_s[...] = _expand_gate(jnp.concatenate(w_inter, axis=0), first, nheads)
    ws_s[...] = _expand_gate(jnp.concatenate(wk_scale, axis=0), first, nheads)
    ed_s[...] = jnp.exp(-_expand_gate(b + mx, first, nheads, terms=3))
    dp_s[...] = _expand_gate(jnp.concatenate(dprev, axis=0), first, nheads)

    incl = _iota((CHUNK, LANES), 0) >= _iota((CHUNK, LANES), 1) % CHUNK
    rows128 = _iota((LANES, 4 * LANES), 0) < CHUNK
    lanes512 = _iota((LANES, 4 * LANES), 1) < 2 * LANES
    diag_blocks = rows128 == lanes512
    ones_v = jnp.ones((CHUNK, LANES), f32)
    npair = nheads // 2
    pair_cols = [slice(pair * LANES, (pair + 1) * LANES) for pair in range(npair)]

    def vext(rows, pair):
        v0, v1 = _pair_cols(v_s, rows, pair)
        return jnp.concatenate([v0, ones_v, v1, ones_v], axis=1)

    units = [(c, slice(c * CHUNK, (c + 1) * CHUNK), pair) for c in range(nc) for pair in range(npair)]
    qks = [_dot_nt(q_s[rows, pair_cols[pair]], _stack_heads(k_s[rows, pair_cols[pair]])) for _, rows, pair in units]
    wms = []
    for (_, rows, pair), qk_u in zip(units, qks):
        expo = _pair_row_form(*_pair_cols(cc_s, rows, pair)) - _pair_select(*_pair_cols(mx_s, rows, pair))
        wms.append(jnp.where(incl, jnp.exp(jnp.minimum(expo, 0.0)) * qk_u, 0.0))
    for (_, rows, pair), wm in zip(units, wms):
        ve = vext(rows, pair)
        intra_s[rows, 4 * pair * LANES:4 * (pair + 1) * LANES] = _dot(
            wm, jnp.where(diag_blocks, jnp.concatenate([ve, ve], axis=0), 0.0))
    for c, rows, pair in units:
        wk = k_s[rows, pair_cols[pair]] * _pair_select(*_pair_cols(ws_s, rows, pair))
        upd_s[c, pair] = jnp.where(diag_blocks, _dot_tn(wk, vext(rows, pair)), 0.0)

    def chunk_step(c, carry):
        chains = [(b, pair) for b in range(nb) for pair in range(npair)]
        ss = [s_ref[b * npair + pair] for b, pair in chains]
        inters = [_dot(q_s[pl.ds(pl.multiple_of(c * CHUNK + b * tb, CHUNK), CHUNK), pair_cols[pair]], s)
                  for (b, pair), s in zip(chains, ss)]
        for (b, pair), s, inter in zip(chains, ss, inters):
            g = c + b * ncb
            rows = pl.ds(pl.multiple_of(c * CHUNK + b * tb, CHUNK), CHUNK)
            dp = _pair_cols(dp_s, pl.ds(pl.multiple_of(g * SUBLANES, SUBLANES), 1), pair)
            s_ref[b * npair + pair] = s * jnp.concatenate([dp[0], dp[0], dp[1], dp[1]], axis=1) + upd_s[g, pair]
            for h in range(2):
                cols = slice((2 * pair + h) * LANES, (2 * pair + h + 1) * LANES)
                c0 = 2 * h * LANES
                i0 = 4 * pair * LANES + c0
                wi = wi_s[rows, cols]
                num = wi * inter[:, c0:c0 + LANES] + intra_s[rows, i0:i0 + LANES]
                den = wi * inter[:, c0 + LANES:c0 + 2 * LANES] + intra_s[rows, i0 + LANES:i0 + 2 * LANES]
                h_s[rows, cols] = num / jnp.maximum(jnp.abs(den), ed_s[rows, cols])
        return carry

    lax.fori_loop(0, ncb, chunk_step, 0)
    ones = _head_pair_ones(LANES)
    _store_rows(o_ref, _head_rms(h_s[...], ng_ref[...], ones) * _sigmoid(_load_rows(og_ref)))


def mlstm_mix(p_qk, p_v, p_og, p_gates, batch, conv_w, ig_b, fg_b, norm_g):
    n, cols = p_qk.shape
    w = p_v.shape[1]
    nheads = w // LANES
    t = n // batch
    tb = min(ODD_BLOCK, t)
    rows = batch * tb
    consts = [conv_w, _gate_row(ig_b, 2 * nheads), _gate_row(fg_b, 2 * nheads),
              jnp.tile(norm_g, nheads).reshape(1, w)]
    blk = lambda: pltpu.VMEM((rows, w), f32)
    out = pl.pallas_call(
        _mlstm_body,
        grid=(t // tb,),
        in_specs=_seq_specs(batch, tb, [cols, w, w, GATE_LANES]) + [_const_spec(c.shape) for c in consts],
        out_specs=_seq_specs(batch, tb, [w])[0],
        out_shape=jax.ShapeDtypeStruct((batch, t, w), bf16),
        scratch_shapes=[pltpu.VMEM((batch, SUBLANES, cols), f32),
                        pltpu.VMEM((batch * nheads // 2, LANES, 4 * LANES), f32),
                        pltpu.VMEM((batch * SUBLANES, GATE_LANES), f32), pltpu.VMEM((rows, cols // 2), f32),
                        pltpu.VMEM((rows, cols // 2), f32)] + [blk() for _ in range(7)]
                       + [pltpu.VMEM((rows // CHUNK * SUBLANES, w), f32), pltpu.VMEM((rows, 2 * w), f32),
                          pltpu.VMEM((rows // CHUNK, nheads // 2, LANES, 4 * LANES), f32)],
        compiler_params=_cparams(("arbitrary",)),
        name="mlstm_mix",
    )(p_qk.reshape(batch, t, cols), p_v.reshape(batch, t, w), p_og.reshape(batch, t, w),
      p_gates.reshape(batch, t, GATE_LANES), *consts)
    return out.reshape(n, w)


def kernel(x, norm_mix_g, norm_mlp_g, mlp_up, mlp_down, final_g, ev_w_in, ev_w_out, lru_conv_w, lru_conv_b, lru_wa,
           lru_ba, lru_wx, lru_bx, lru_lambda, rwkv_mu, rwkv_w0, rwkv_w2, rwkv_a0, rwkv_a2, rwkv_g2, rwkv_kk,
           rwkv_ka, rwkv_rk, rwkv_lnw, rwkv_lnb, od_w_in, od_w_out, gdn_conv_w, gdn_a_log, gdn_dt_bias, gdn_norm_g,
           mlstm_conv_w, mlstm_ig_b, mlstm_fg_b, mlstm_norm_g):
    batch, seq, d = x.shape
    depth = norm_mix_g.shape[0]
    xs = x.reshape(batch * seq, d)
    for l in range(depth):
        if l % 2 == 0:
            e = l // 2
            lru_cols = 2 * lru_lambda.shape[1]
            p_lru, p_rwkv = norm_proj(xs, norm_mix_g[l], ev_w_in[e].astype(bf16),
                                      (lru_cols, ev_w_in.shape[2] - lru_cols))
            ya = lru_mix(p_lru, batch, lru_conv_w[e], lru_conv_b[e], lru_wa[e], lru_ba[e], lru_wx[e], lru_bx[e],
                         lru_lambda[e])
            yb = rwkv_mix(p_rwkv, batch, rwkv_mu[e], rwkv_w0[e], rwkv_w2[e], rwkv_a0[e], rwkv_a2[e], rwkv_g2[e],
                          rwkv_kk[e], rwkv_ka[e], rwkv_rk[e], rwkv_lnw[e], rwkv_lnb[e])
            w_out = ev_w_out[e]
        else:
            o = l // 2
            w_in, widths = _odd_in_weights(od_w_in[o], gdn_conv_w.shape[2], gdn_norm_g.shape[1] * gdn_a_log.shape[1],
                                           mlstm_conv_w.shape[2], mlstm_norm_g.shape[1] * mlstm_ig_b.shape[1],
                                           gdn_a_log.shape[1], mlstm_ig_b.shape[1])
            p_qkv, p_z, p_mqk, p_mv, p_mog, p_gates = norm_proj(xs, norm_mix_g[l], w_in, widths)
            ya = gdn_mix(p_qkv, p_z, p_gates, batch, gdn_conv_w[o], gdn_a_log[o], gdn_dt_bias[o], gdn_norm_g[o])
            yb = mlstm_mix(p_mqk, p_mv, p_mog, p_gates, batch, mlstm_conv_w[o], mlstm_ig_b[o], mlstm_fg_b[o],
                           mlstm_norm_g[o])
            w_out = od_w_out[o]
        xs = out_mlp(ya, yb, xs, w_out.astype(bf16), norm_mlp_g[l], mlp_up[l].astype(bf16),
                     mlp_down[l].astype(bf16), final_g, l == depth - 1)
    return xs.reshape(batch, seq, d)


def _odd_in_weights(w, gdn_qkv, gdn_w, mlstm_qk, mlstm_w, gdn_heads, mlstm_heads):
    c = 0
    parts = {}
    for name, width in (("qkv", gdn_qkv), ("z", gdn_w), ("b", gdn_heads), ("a", gdn_heads), ("mqk", mlstm_qk),
                        ("mv", mlstm_w), ("mog", mlstm_w), ("i", mlstm_heads), ("f", mlstm_heads)):
        parts[name] = w[:, c:c + width]
        c += width
    ngate = 2 * gdn_heads + 2 * mlstm_heads
    gates = jnp.concatenate([parts["b"], parts["a"], parts["i"], parts["f"],
                             jnp.zeros((w.shape[0], GATE_LANES - ngate), w.dtype)], axis=1)
    w_in = jnp.concatenate([parts["qkv"], parts["z"], parts["mqk"], parts["mv"], parts["mog"], gates], axis=1)
    return w_in.astype(bf16), (gdn_qkv, gdn_w, mlstm_qk, mlstm_w, mlstm_w, GATE_LANES)
```

```python
import functools
import math

import jax
import jax.numpy as jnp
from jax import lax
from jax.experimental import pallas as pl
from jax.experimental.pallas import tpu as pltpu

f32 = jnp.float32
bf16 = jnp.bfloat16

LANES = 128
SUBLANES = 8
VMEM_LIMIT = 56 * 1024 * 1024

NORM_EPS = 1e-6
CONV_W = 4
LRU_C = 8.0
RWKV_HEAD = 64
RWKV_LN_EPS = 64e-5
CHUNK = 64

ROW_BLOCK = 512
SEQ_BLOCK = 128


def _cparams(sem):
    return pltpu.CompilerParams(dimension_semantics=sem, vmem_limit_bytes=VMEM_LIMIT)


def _const_spec(shape):
    nd = len(shape)
    return pl.BlockSpec(shape, lambda *_: (0,) * nd)


def _rms(x, g):
    return x * lax.rsqrt(jnp.mean(x * x, axis=-1, keepdims=True) + NORM_EPS) * g


def _softplus(x):
    return jnp.maximum(x, 0.0) + jnp.log(1.0 + jnp.exp(-jnp.abs(x)))


def _sigmoid(x):
    return 0.5 * jnp.tanh(0.5 * x) + 0.5


def _norm_proj_body(widths, x_ref, g_ref, *refs):
    w_refs, o_refs = refs[:len(widths)], list(refs[len(widths):])
    h = _rms(x_ref[...], g_ref[...]).astype(bf16)
    for w_ref, group in zip(w_refs, widths):
        c0 = 0
        for wd in group:
            o_ref = o_refs.pop(0)
            o_ref[...] = jnp.dot(h, w_ref[:, c0:c0 + wd], preferred_element_type=f32).astype(o_ref.dtype)
            c0 += wd


def norm_proj(x, g, weights, widths):
    n, d = x.shape
    tm = min(ROW_BLOCK, n)
    flat = [wd for group in widths for wd in group]
    return pl.pallas_call(
        functools.partial(_norm_proj_body, widths),
        grid=(n // tm,),
        in_specs=[pl.BlockSpec((tm, d), lambda i: (i, 0)), _const_spec((1, d))]
                 + [pl.BlockSpec((d, cols), lambda i: (0, 0)) for _, cols in weights],
        out_specs=[pl.BlockSpec((tm, wd), lambda i: (i, 0)) for wd in flat],
        out_shape=[jax.ShapeDtypeStruct((n, wd), f32) for wd in flat],
        compiler_params=_cparams(("parallel",)),
        name="norm_proj",
    )(x, g.reshape(1, d), *[w for w, _ in weights])


FF_BLOCK = 1024


def _out_mlp_body(final, ya_ref, yb_ref, x_ref, wo_ref, g_ref, wu_ref, wd_ref, gf_ref, o_ref):
    half = ya_ref.shape[1]
    y = jnp.dot(ya_ref[...], wo_ref[0:half, :], preferred_element_type=f32)
    y = y + jnp.dot(yb_ref[...], wo_ref[half:, :], preferred_element_type=f32)
    x1 = x_ref[...] + y
    h = _rms(x1, g_ref[...]).astype(bf16)
    acc = x1
    for c in range(wu_ref.shape[1] // FF_BLOCK):
        u = jnp.dot(h, wu_ref[:, c * FF_BLOCK:(c + 1) * FF_BLOCK], preferred_element_type=f32)
        u = jnp.square(jnp.maximum(u, 0.0)).astype(bf16)
        acc = acc + jnp.dot(u, wd_ref[c * FF_BLOCK:(c + 1) * FF_BLOCK, :], preferred_element_type=f32)
    if final:
        acc = _rms(acc, gf_ref[...])
    o_ref[...] = acc


def out_mlp(ya, yb, x, wo, g, wu, wd, gf, final):
    n, d = x.shape
    half = ya.shape[1]
    tm = min(ROW_BLOCK, n)
    row = lambda i: (i, 0)
    return pl.pallas_call(
        functools.partial(_out_mlp_body, final),
        grid=(n // tm,),
        in_specs=[pl.BlockSpec((tm, half), row), pl.BlockSpec((tm, half), row), pl.BlockSpec((tm, d), row),
                  _const_spec(wo.shape), _const_spec((1, d)), _const_spec(wu.shape), _const_spec(wd.shape),
                  _const_spec((1, d))],
        out_specs=pl.BlockSpec((tm, d), row),
        out_shape=jax.ShapeDtypeStruct((n, d), f32),
        compiler_params=_cparams(("parallel",)),
        name="out_mlp",
    )(ya, yb, x, wo, g.reshape(1, d), wu, wd, gf.reshape(1, d))


def _seq_specs(batch, tb, widths):
    return [pl.BlockSpec((batch, tb, wd), lambda i: (0, i, 0)) for wd in widths]


def _load_rows(ref):
    return jnp.concatenate([ref[b] for b in range(ref.shape[0])], axis=0)


def _store_rows(ref, x):
    tb = ref.shape[1]
    for b in range(ref.shape[0]):
        ref[b] = x[b * tb:(b + 1) * tb].astype(ref.dtype)


def _per_batch(fn, x, prev_ref):
    nb = prev_ref.shape[0]
    tb = x.shape[0] // nb
    out = []
    for b in range(nb):
        xb = x[b * tb:(b + 1) * tb]
        out.append(fn(xb, prev_ref[b]))
        prev_ref[b] = xb[tb - SUBLANES:]
    return jnp.concatenate(out, axis=0)


def _shift_rows(x, prev, k):
    rolled = pltpu.roll(x, k, axis=0)
    head = pltpu.roll(prev, k, axis=0)
    rows = lax.broadcasted_iota(jnp.int32, (SUBLANES, x.shape[1]), 0)
    top = jnp.where(rows < k, head, rolled[0:SUBLANES])
    return jnp.concatenate([top, rolled[SUBLANES:]], axis=0)


def _causal_conv(x, prev, w):
    y = x * w[CONV_W - 1:CONV_W]
    for k in range(1, CONV_W):
        y = y + _shift_rows(x, prev, k) * w[CONV_W - 1 - k:CONV_W - k]
    return y


def _scan_shift(x, s, fill):
    n, c = x.shape
    if s % SUBLANES == 0:
        return jnp.concatenate([jnp.full((s, c), fill, x.dtype), x[:n - s]], axis=0)
    rolled = pltpu.roll(x, s, axis=0)
    rows = lax.broadcasted_iota(jnp.int32, (SUBLANES, c), 0)
    top = jnp.where(rows < s, fill, rolled[0:SUBLANES])
    return jnp.concatenate([top, rolled[SUBLANES:]], axis=0)


def _linear_scan(a, u):
    s = 1
    while s < a.shape[0]:
        u = a * _scan_shift(u, s, 0.0) + u
        a = a * _scan_shift(a, s, 1.0)
        s *= 2
    return a, u


def _lru_body(p_ref, cw_ref, cb_ref, wg_ref, bg_ref, lam_ref, o_ref, xprev_ref, h_ref):
    nb, tb = p_ref.shape[0], p_ref.shape[1]
    w = o_ref.shape[2]
    step = pl.program_id(0)

    @pl.when(step == 0)
    def _():
        xprev_ref[...] = jnp.zeros_like(xprev_ref)
        h_ref[...] = jnp.zeros_like(h_ref)

    gate_in = jnp.concatenate([p_ref[b, :, 0:w] for b in range(nb)], axis=0)
    rec = jnp.concatenate([p_ref[b, :, w:] for b in range(nb)], axis=0)
    xc = _per_batch(lambda x, prev: _causal_conv(x, prev, cw_ref[...]), rec, xprev_ref) + cb_ref[...]
    gates = jnp.dot(xc.astype(bf16), wg_ref[...], preferred_element_type=f32) + bg_ref[...]
    r_gate = _sigmoid(gates[:, 0:w])
    i_gate = _sigmoid(gates[:, w:])
    log_a = (-LRU_C) * r_gate * _softplus(-lam_ref[...])
    a = jnp.exp(log_a)
    mult = jnp.sqrt(1.0 - jnp.exp(2.0 * log_a))
    rows = lax.broadcasted_iota(jnp.int32, (nb * tb, w), 0)
    mult = jnp.where(jnp.logical_and(rows % tb == 0, step == 0), 1.0, mult)
    u = i_gate * xc * mult
    hs = []
    for b in range(nb):
        a_cum, h = _linear_scan(a[b * tb:(b + 1) * tb], u[b * tb:(b + 1) * tb])
        h = h + a_cum * h_ref[b * SUBLANES:b * SUBLANES + 1, :]
        h_ref[b * SUBLANES:b * SUBLANES + 1, :] = h[tb - 1:tb]
        hs.append(h)
    _store_rows(o_ref, jnp.concatenate(hs, axis=0) * jax.nn.gelu(gate_in, approximate=True))


def _block_diag(wb):
    nb, d, e = wb.shape
    eye = jnp.eye(nb, dtype=wb.dtype)
    return (wb[:, :, None, :] * eye[:, None, :, None]).reshape(nb * d, nb * e)


def lru_mix(p, batch, conv_w, conv_b, wa, ba, wx, bx, lam):
    n, c2 = p.shape
    w = c2 // 2
    t = n // batch
    tb = min(SEQ_BLOCK, t)
    wg = jnp.concatenate([_block_diag(wa), _block_diag(wx)], axis=1).astype(bf16)
    bg = jnp.concatenate([ba, bx]).reshape(1, 2 * w)
    out = pl.pallas_call(
        _lru_body,
        grid=(t // tb,),
        in_specs=_seq_specs(batch, tb, [c2]) + [_const_spec((CONV_W, w)), _const_spec((1, w)),
                                                _const_spec((w, 2 * w)), _const_spec((1, 2 * w)), _const_spec((1, w))],
        out_specs=_seq_specs(batch, tb, [w])[0],
        out_shape=jax.ShapeDtypeStruct((batch, t, w), bf16),
        scratch_shapes=[pltpu.VMEM((batch, SUBLANES, w), f32), pltpu.VMEM((batch * SUBLANES, w), f32)],
        compiler_params=_cparams(("arbitrary",)),
        name="lru_mix",
    )(p.reshape(batch, t, c2), conv_w, conv_b.reshape(1, w), wg, bg, lam.reshape(1, w))
    return out.reshape(n, w)


UNIT_GROUP = 32


def _dot(a, b):
    return jnp.dot(a.astype(bf16), b.astype(bf16), preferred_element_type=f32)


def _dot_nt(a, b):
    return lax.dot_general(a.astype(bf16), b.astype(bf16), (((1,), (1,)), ((), ())), preferred_element_type=f32)


def _dot_tn(a, b):
    return lax.dot_general(a.astype(bf16), b.astype(bf16), (((0,), (0,)), ((), ())), preferred_element_type=f32)


def _dot_split(x, e, terms=2):
    out = None
    for _ in range(terms):
        hi = x.astype(bf16)
        x = x - hi.astype(f32)
        part = jnp.dot(hi, e, preferred_element_type=f32)
        out = part if out is None else out + part
    return out


def _iota(shape, axis):
    return lax.broadcasted_iota(jnp.int32, shape, axis)


def _tri_incl():
    return (_iota((CHUNK, CHUNK), 0) >= _iota((CHUNK, CHUNK), 1)).astype(bf16)


def _chunk_cumsum(x):
    tri = _tri_incl()
    cs, ce = [], []
    for c in range(x.shape[0] // CHUNK):
        rest = x[c * CHUNK:(c + 1) * CHUNK]
        cs_c = None
        for _ in range(3):
            hi = rest.astype(bf16)
            rest = rest - hi.astype(f32)
            part = jnp.dot(tri, hi, preferred_element_type=f32)
            cs_c = part if cs_c is None else cs_c + part
        cs.append(cs_c)
        ce.append(jnp.broadcast_to(cs_c[CHUNK - 1:CHUNK], cs_c.shape))
    return jnp.concatenate(cs, axis=0), jnp.concatenate(ce, axis=0)


def _head_pair_ones(head):
    return (_iota((LANES, LANES), 0) // head == _iota((LANES, LANES), 1) // head).astype(bf16)


def _segsum(x, e):
    return jnp.concatenate([_dot_split(x[:, j:j + LANES], e) for j in range(0, x.shape[1], LANES)], axis=1)


def _stack_heads(x):
    lo = _iota(x.shape, 1) < RWKV_HEAD
    return jnp.concatenate([jnp.where(lo, x, 0.0), jnp.where(lo, 0.0, x)], axis=0)


def _unit_lower_inverse(amats):
    shape = amats[0].shape
    ii = _iota(shape, 0)
    jj = _iota(shape, 1) % CHUNK
    eye = (ii == jj).astype(f32)
    first = jnp.logical_and(ii // 2 == jj // 2, ii > jj)
    ts = [eye - jnp.where(first, a, 0.0) for a in amats]
    abf = [a.astype(bf16) for a in amats]
    b = 2
    while b < CHUNK:
        off = jnp.logical_and(ii // (2 * b) == jj // (2 * b), ii // b > jj // b)
        xs = [_dot(a, _stack_heads(t)) for a, t in zip(abf, ts)]
        ts = [t - jnp.where(off, _dot(t, _stack_heads(x)), 0.0) for t, x in zip(ts, xs)]
        b *= 2
    return ts


def _rwkv_body(p_ref, mu_ref, w0_ref, w2_ref, a0_ref, a2_ref, g2_ref, kk_ref, ka_ref, rk_ref, lnw_ref, lnb_ref,
               o_ref, prev_ref, s_ref, at_s, rt_s, bt_s, kt_s, gc_s, v_s, y_s, wh_s, uh_s, mrb_s):
    nb, tb = p_ref.shape[0], p_ref.shape[1]
    n = nb * tb
    w = o_ref.shape[2]
    npair = w // LANES

    @pl.when(pl.program_id(0) == 0)
    def _():
        prev_ref[...] = jnp.zeros_like(prev_ref)
        s_ref[...] = jnp.zeros_like(s_ref)

    p = _load_rows(p_ref)
    pf = p + mu_ref[...] * (_per_batch(lambda x, prev: _shift_rows(x, prev, 1), p, prev_ref) - p)
    r, k, v = pf[:, 0:w], pf[:, w:2 * w], pf[:, 2 * w:3 * w]
    lowrank = pf[:, 3 * w:3 * w + LANES]
    xg = pf[:, 3 * w + LANES:]
    lw = (-math.exp(-0.5)) * _sigmoid(w0_ref[...] + _dot(jnp.tanh(lowrank), w2_ref[...]))
    a = _sigmoid(a0_ref[...] + _dot(lowrank, a2_ref[...]))
    g = _dot(_sigmoid(xg), g2_ref[...])
    ones = _head_pair_ones(RWKV_HEAD)
    kk = k * kk_ref[...]
    kk = kk * lax.rsqrt(_segsum(kk * kk, ones) + 1e-6)
    k = k * (1.0 + (a - 1.0) * ka_ref[...])
    cs, ce = _chunk_cumsum(lw)
    inv_g = jnp.exp(-cs)
    at_s[...] = kk * jnp.exp(cs - lw)
    rt_s[...] = r * jnp.exp(cs)
    bt_s[...] = -(kk * a) * inv_g
    kt_s[...] = k * inv_g
    gc_s[...] = jnp.exp(ce)
    v_s[...] = v

    tt = _iota((CHUNK, LANES), 0)
    ss = _iota((CHUNK, LANES), 1) % CHUNK
    strict = tt > ss
    incl = tt >= ss
    diag_blocks = (_iota((LANES, LANES), 0) < RWKV_HEAD) == (_iota((LANES, LANES), 1) < RWKV_HEAD)

    pair_cols = [slice(j * LANES, (j + 1) * LANES) for j in range(npair)]
    all_units = [(slice(c * CHUNK, (c + 1) * CHUNK), cols) for c in range(n // CHUNK) for cols in pair_cols]
    for u0 in range(0, len(all_units), UNIT_GROUP):
        units = all_units[u0:u0 + UNIT_GROUP]
        gms = [_dot_nt(jnp.concatenate([at_s[u], rt_s[u]], axis=0),
                       jnp.concatenate([_stack_heads(bt_s[u]), _stack_heads(kt_s[u])], axis=0)) for u in units]
        for u, gm in zip(units, gms):
            mrb_s[u] = jnp.where(incl, gm[CHUNK:, 0:LANES], 0.0)
        vsts = [_stack_heads(v_s[u]) for u in units]
        akvs = [_dot(jnp.where(strict, gm[0:CHUNK, LANES:], 0.0), vst) for gm, vst in zip(gms, vsts)]
        yhs = [_dot(jnp.where(incl, gm[CHUNK:, LANES:], 0.0), vst) for gm, vst in zip(gms, vsts)]
        for u, yh in zip(units, yhs):
            y_s[u] = yh
        tinvs = _unit_lower_inverse([jnp.where(strict, -gm[0:CHUNK, 0:LANES], 0.0) for gm in gms])
        wus = [_dot(t, jnp.concatenate([_stack_heads(at_s[u]), _stack_heads(akv)], axis=1))
               for u, t, akv in zip(units, tinvs, akvs)]
        for u, wu in zip(units, wus):
            wh_s[u] = wu[:, 0:LANES]
            uh_s[u] = wu[:, LANES:]

    def chunk_step(c, carry):
        starts = [pl.multiple_of(c * CHUNK + b * tb, CHUNK) for b in range(nb)]
        chains = [(b * npair + j, pl.ds(starts[b], CHUNK), cols) for b in range(nb) for j, cols in enumerate(pair_cols)]
        ss = [s_ref[si] for si, _, _ in chains]
        xs = [_dot_nt(jnp.concatenate([wh_s[rows, cols], rt_s[rows, cols]], axis=0), s)
              for (_, rows, cols), s in zip(chains, ss)]
        us = [x[0:CHUNK] + uh_s[rows, cols] for (_, rows, cols), x in zip(chains, xs)]
        upds = [_dot_tn(jnp.concatenate([u, v_s[rows, cols]], axis=0),
                        jnp.concatenate([bt_s[rows, cols], kt_s[rows, cols]], axis=0))
                for (_, rows, cols), u in zip(chains, us)]
        ys = [_dot(mrb_s[rows, cols], _stack_heads(u)) for (_, rows, cols), u in zip(chains, us)]
        for (si, rows, cols), s, x, upd, y in zip(chains, ss, xs, upds, ys):
            s_ref[si] = (s + jnp.where(diag_blocks, upd, 0.0)) * gc_s[pl.ds(starts[si // npair], 1), cols]
            y_s[rows, cols] = y_s[rows, cols] + x[CHUNK:] + y
        return carry

    lax.fori_loop(0, tb // CHUNK, chunk_step, 0)

    y = y_s[...]
    inv_head = 1.0 / RWKV_HEAD
    mean = _segsum(y, ones) * inv_head
    yc = y - mean
    var = _segsum(yc * yc, ones) * inv_head
    yn = yc * lax.rsqrt(var + RWKV_LN_EPS) * lnw_ref[...] + lnb_ref[...]
    bonus = _segsum(r * k * rk_ref[...], ones) * v
    _store_rows(o_ref, (yn + bonus) * g)


def rwkv_mix(p, batch, mu, w0, w2, a0, a2, g2, k_k, k_a, r_k, ln_w, ln_b):
    n, cols = p.shape
    w = w0.shape[0]
    t = n // batch
    tb = min(SEQ_BLOCK, t)
    rank = w2.shape[0]
    zeros = jnp.zeros((LANES - rank, w), f32)
    w2p = jnp.concatenate([w2, zeros], axis=0).astype(bf16)
    a2p = jnp.concatenate([zeros, a2], axis=0).astype(bf16)
    row = lambda v: v.reshape(1, -1)
    consts = [row(mu), row(w0), w2p, row(a0), a2p, g2.astype(bf16), row(k_k), row(k_a), row(r_k), row(ln_w),
              row(ln_b)]
    blk = lambda: pltpu.VMEM((batch * tb, w), f32)
    out = pl.pallas_call(
        _rwkv_body,
        grid=(t // tb,),
        in_specs=_seq_specs(batch, tb, [cols]) + [_const_spec(c.shape) for c in consts],
        out_specs=_seq_specs(batch, tb, [w])[0],
        out_shape=jax.ShapeDtypeStruct((batch, t, w), bf16),
        scratch_shapes=[pltpu.VMEM((batch, SUBLANES, cols), f32),
                        pltpu.VMEM((batch * w // LANES, LANES, LANES), f32)] + [blk() for _ in range(10)],
        compiler_params=_cparams(("arbitrary",)),
        name="rwkv_mix",
    )(p.reshape(batch, t, cols), *consts)
    return out.reshape(n, w)


GATE_LANES = LANES


def _silu(x):
    return x * _sigmoid(x)


def _expand_gate(gates, first, nheads, terms=2):
    sel = (_iota((GATE_LANES, nheads * LANES), 0) - first == _iota((GATE_LANES, nheads * LANES), 1) // LANES)
    return _dot_split(gates, sel.astype(bf16), terms)


def _gate_row(v, first):
    return jnp.zeros((1, GATE_LANES), f32).at[0, first:first + v.shape[0]].set(v)


def _pair_cols(x_ref, rows, pair):
    c = 2 * pair * LANES
    return x_ref[rows, c:c + LANES], x_ref[rows, c + LANES:c + 2 * LANES]


def _pair_select(x0, x1):
    return jnp.where(_iota(x0.shape, 1) < CHUNK, x0, x1)


def _pair_row_form(x0, x1):
    return jnp.transpose(jnp.concatenate([x0, x1], axis=0))[0:CHUNK]


def _pair_block_rhs(x0, x1):
    return jnp.concatenate([jnp.concatenate([x0, jnp.zeros_like(x0)], axis=1),
                            jnp.concatenate([jnp.zeros_like(x1), x1], axis=1)], axis=0)


def _head_rms(o, g, ones):
    return o * lax.rsqrt(_segsum(o * o, ones) * (1.0 / LANES) + NORM_EPS) * g


def _gdn_body(p_ref, z_ref, gate_ref, cw_ref, alog_ref, dtb_ref, ng_ref, o_ref,
              prev_ref, s_ref, q_s, k_s, kb_s, vb_s, qd_s, kbg_s, kd_s, gc_s, gl_s, o_s, u_s, w_s, qk_s):
    nb, tb = p_ref.shape[0], p_ref.shape[1]
    n = nb * tb
    w = o_ref.shape[2]
    nheads = w // LANES

    @pl.when(pl.program_id(0) == 0)
    def _():
        prev_ref[...] = jnp.zeros_like(prev_ref)
        s_ref[...] = jnp.zeros_like(s_ref)

    qkv = _silu(_per_batch(lambda x, prev: _causal_conv(x, prev, cw_ref[...]), _load_rows(p_ref), prev_ref))
    ones = _head_pair_ones(LANES)
    q, k, v = qkv[:, 0:w], qkv[:, w:2 * w], qkv[:, 2 * w:]
    q = q * lax.rsqrt(_segsum(q * q, ones) + 1e-6) * (LANES ** -0.5)
    k = k * lax.rsqrt(_segsum(k * k, ones) + 1e-6)
    gates = _load_rows(gate_ref)
    beta = _expand_gate(_sigmoid(gates), 0, nheads)
    gc_c, ge_c = _chunk_cumsum(-jnp.exp(alog_ref[...]) * _softplus(gates + dtb_ref[...]))
    eg = _expand_gate(jnp.exp(gc_c), nheads, nheads)
    kb = k * beta
    q_s[...] = q
    k_s[...] = k
    kb_s[...] = kb
    vb_s[...] = v * beta
    qd_s[...] = q * eg
    kbg_s[...] = kb * eg
    kd_s[...] = k * _expand_gate(jnp.exp(ge_c - gc_c), nheads, nheads)
    gc_s[...] = _expand_gate(gc_c, nheads, nheads, terms=3)
    gl_s[...] = _expand_gate(jnp.exp(ge_c), nheads, nheads)

    tt = _iota((CHUNK, LANES), 0)
    ss = _iota((CHUNK, LANES), 1) % CHUNK
    strict = tt > ss
    incl = tt >= ss

    all_units = [(slice(c * CHUNK, (c + 1) * CHUNK), pair) for c in range(n // CHUNK) for pair in range(nheads // 2)]
    for u0 in range(0, len(all_units), UNIT_GROUP):
        units = all_units[u0:u0 + UNIT_GROUP]
        xs = [_dot_nt(jnp.concatenate(_pair_cols(kb_s, rows, pair) + _pair_cols(q_s, rows, pair), axis=0),
                      jnp.concatenate(_pair_cols(k_s, rows, pair), axis=0)) for rows, pair in units]
        amats = []
        for (rows, pair), x in zip(units, xs):
            g0, g1 = _pair_cols(gc_s, rows, pair)
            dec = jnp.exp(jnp.minimum(_pair_select(g0, g1) - _pair_row_form(g0, g1), 0.0))
            amats.append(jnp.where(strict, _pair_select(x[0:CHUNK], x[CHUNK:2 * CHUNK]) * dec, 0.0))
            qk_s[rows, pair * LANES:(pair + 1) * LANES] = jnp.where(
                incl, _pair_select(x[2 * CHUNK:3 * CHUNK], x[3 * CHUNK:]) * dec, 0.0)
        tinvs = _unit_lower_inverse(amats)
        uws = []
        for (rows, pair), tinv in zip(units, tinvs):
            vb0, vb1 = _pair_cols(vb_s, rows, pair)
            kg0, kg1 = _pair_cols(kbg_s, rows, pair)
            uws.append(_dot(tinv, _pair_block_rhs(jnp.concatenate([vb0, kg0], axis=1),
                                                  jnp.concatenate([vb1, kg1], axis=1))))
        for (rows, pair), uw in zip(units, uws):
            for h in range(2):
                cols = slice((2 * pair + h) * LANES, (2 * pair + h + 1) * LANES)
                u_s[rows, cols] = uw[:, 2 * h * LANES:(2 * h + 1) * LANES]
                w_s[rows, cols] = uw[:, (2 * h + 1) * LANES:(2 * h + 2) * LANES]

    head_cols = [slice(h * LANES, (h + 1) * LANES) for h in range(nheads)]

    def chunk_step(c, carry):
        starts = [pl.multiple_of(c * CHUNK + b * tb, CHUNK) for b in range(nb)]
        chains = [(b, h, pl.ds(starts[b], CHUNK), cols) for b in range(nb) for h, cols in enumerate(head_cols)]
        ss = [s_ref[b * nheads + h] for b, h, _, _ in chains]
        x2s = [_dot(jnp.concatenate([w_s[rows, cols], qd_s[rows, cols]], axis=0), s)
               for (_, _, rows, cols), s in zip(chains, ss)]
        vns = [u_s[rows, cols] - x2[0:CHUNK] for (_, _, rows, cols), x2 in zip(chains, x2s)]
        upds = [_dot_tn(kd_s[rows, cols], vn) for (_, _, rows, cols), vn in zip(chains, vns)]
        os = [_dot(qk_s[pl.ds(starts[b], CHUNK), pair * LANES:(pair + 1) * LANES],
                   _pair_block_rhs(vns[b * nheads + 2 * pair], vns[b * nheads + 2 * pair + 1]))
              for b in range(nb) for pair in range(nheads // 2)]
        for ci, (b, h, rows, cols) in enumerate(chains):
            s_ref[ci] = ss[ci] * gl_s[pl.ds(starts[b], 1), cols] + upds[ci]
            o_s[rows, cols] = x2s[ci][CHUNK:] + os[ci // 2][:, (h % 2) * LANES:(h % 2 + 1) * LANES]
        return carry

    lax.fori_loop(0, tb // CHUNK, chunk_step, 0)
    _store_rows(o_ref, _head_rms(o_s[...], ng_ref[...], ones) * _silu(_load_rows(z_ref)))


def gdn_mix(p_qkv, p_z, p_gates, batch, conv_w, a_log, dt_bias, norm_g):
    n, cols = p_qkv.shape
    w = p_z.shape[1]
    nheads = w // LANES
    t = n // batch
    tb = min(SEQ_BLOCK, t)
    consts = [conv_w, _gate_row(a_log, nheads), _gate_row(dt_bias, nheads), jnp.tile(norm_g, nheads).reshape(1, w)]
    blk = lambda: pltpu.VMEM((batch * tb, w), f32)
    out = pl.pallas_call(
        _gdn_body,
        grid=(t // tb,),
        in_specs=_seq_specs(batch, tb, [cols, w, GATE_LANES]) + [_const_spec(c.shape) for c in consts],
        out_specs=_seq_specs(batch, tb, [w])[0],
        out_shape=jax.ShapeDtypeStruct((batch, t, w), bf16),
        scratch_shapes=[pltpu.VMEM((batch, SUBLANES, cols), f32), pltpu.VMEM((batch * nheads, LANES, LANES), f32)]
                       + [blk() for _ in range(12)] + [pltpu.VMEM((batch * tb, w // 2), f32)],
        compiler_params=_cparams(("arbitrary",)),
        name="gdn_mix",
    )(p_qkv.reshape(batch, t, cols), p_z.reshape(batch, t, w), p_gates.reshape(batch, t, GATE_LANES), *consts)
    return out.reshape(n, w)


def _cummax_rows(x):
    out = []
    for c in range(x.shape[0] // CHUNK):
        y = x[c * CHUNK:(c + 1) * CHUNK]
        s = 1
        while s < CHUNK:
            y = jnp.maximum(y, _scan_shift(y, s, -jnp.inf))
            s *= 2
        out.append(y)
    return jnp.concatenate(out, axis=0)


def _mlstm_body(pqk_ref, v_ref, og_ref, gate_ref, cw_ref, igb_ref, fgb_ref, ng_ref, o_ref,
                prev_ref, s_ref, m_ref, q_s, k_s, cc_s, mx_s, wi_s, ws_s, ed_s, h_s, v_s, dp_s, intra_s, upd_s):
    nb, tb = pqk_ref.shape[0], pqk_ref.shape[1]
    n = nb * tb
    w = o_ref.shape[2]
    nheads = w // LANES
    hq = pqk_ref.shape[2] // 2

    @pl.when(pl.program_id(0) == 0)
    def _():
        prev_ref[...] = jnp.zeros_like(prev_ref)
        s_ref[...] = jnp.zeros_like(s_ref)
        m_ref[...] = jnp.zeros_like(m_ref)

    qk = _silu(_per_batch(lambda x, prev: _causal_conv(x, prev, cw_ref[...]), _load_rows(pqk_ref), prev_ref))
    q_s[...] = qk[:, 0:hq] * ((hq // nheads) ** -0.5)
    k_s[...] = qk[:, hq:]
    v_s[...] = _load_rows(v_ref)
    gates = _load_rows(gate_ref)
    first = 2 * nheads
    i_pre = gates + igb_ref[...]
    f_pre = pltpu.roll(gates, GATE_LANES - nheads, axis=1) + fgb_ref[...]
    b, b_end = _chunk_cumsum(-_softplus(-f_pre))
    cc = i_pre - b
    cm = _cummax_rows(cc)
    te = b_end + cc
    nc = n // CHUNK
    ncb = tb // CHUNK
    mx, w_inter, wk_scale, dprev = [], [], [], []
    for c in range(nc):
        bi = c // ncb
        m = m_ref[bi * SUBLANES:bi * SUBLANES + 1, :] if c % ncb == 0 else m
        rows = slice(c * CHUNK, (c + 1) * CHUNK)
        b_last = b_end[c * CHUNK:c * CHUNK + 1]
        m_new = jnp.maximum(b_last + m, jnp.max(te[rows], axis=0, keepdims=True))
        mx_c = jnp.maximum(m, cm[rows])
        mx.append(mx_c)
        w_inter.append(jnp.exp(m - mx_c))
        wk_scale.append(jnp.exp(te[rows] - m_new))
        dprev.append(jnp.broadcast_to(jnp.exp(b_last + m - m_new), (SUBLANES, GATE_LANES)))
        m = m_new
        if c % ncb == ncb - 1:
            m_ref[bi * SUBLANES:bi * SUBLANES + 1, :] = m
    mx = jnp.concatenate(mx, axis=0)
    cc_s[...] = _expand_gate(cc, first, nheads, terms=3)
    mx_s[...] = _expand_gate(mx, first, nheads, terms=3)
    wi_s[...] = _expand_gate(jnp.concatenate(w_inter, axis=0), first, nheads)
    ws_s[...] = _expand_gate(jnp.concatenate(wk_scale, axis=0), first, nheads)
    ed_s[...] = jnp.exp(-_expand_gate(b + mx, first, nheads, terms=3))
    dp_s[...] = _expand_gate(jnp.concatenate(dprev, axis=0), first, nheads)

    incl = _iota((CHUNK, LANES), 0) >= _iota((CHUNK, LANES), 1) % CHUNK
    rows128 = _iota((LANES, 4 * LANES), 0) < CHUNK
    lanes512 = _iota((LANES, 4 * LANES), 1) < 2 * LANES
    diag_blocks = rows128 == lanes512
    ones_v = jnp.ones((CHUNK, LANES), f32)
    npair = nheads // 2
    pair_cols = [slice(pair * LANES, (pair + 1) * LANES) for pair in range(npair)]

    def vext(rows, pair):
        v0, v1 = _pair_cols(v_s, rows, pair)
        return jnp.concatenate([v0, ones_v, v1, ones_v], axis=1)

    units = [(c, slice(c * CHUNK, (c + 1) * CHUNK), pair) for c in range(nc) for pair in range(npair)]
    qks = [_dot_nt(q_s[rows, pair_cols[pair]], _stack_heads(k_s[rows, pair_cols[pair]])) for _, rows, pair in units]
    wms = []
    for (_, rows, pair), qk_u in zip(units, qks):
        expo = _pair_row_form(*_pair_cols(cc_s, rows, pair)) - _pair_select(*_pair_cols(mx_s, rows, pair))
        wms.append(jnp.where(incl, jnp.exp(jnp.minimum(expo, 0.0)) * qk_u, 0.0))
    for (_, rows, pair), wm in zip(units, wms):
        ve = vext(rows, pair)
        intra_s[rows, 4 * pair * LANES:4 * (pair + 1) * LANES] = _dot(
            wm, jnp.where(diag_blocks, jnp.concatenate([ve, ve], axis=0), 0.0))
    for c, rows, pair in units:
        wk = k_s[rows, pair_cols[pair]] * _pair_select(*_pair_cols(ws_s, rows, pair))
        upd_s[c, pair] = jnp.where(diag_blocks, _dot_tn(wk, vext(rows, pair)), 0.0)

    def chunk_step(c, carry):
        chains = [(b, pair) for b in range(nb) for pair in range(npair)]
        ss = [s_ref[b * npair + pair] for b, pair in chains]
        inters = [_dot(q_s[pl.ds(pl.multiple_of(c * CHUNK + b * tb, CHUNK), CHUNK), pair_cols[pair]], s)
                  for (b, pair), s in zip(chains, ss)]
        for (b, pair), s, inter in zip(chains, ss, inters):
            g = c + b * ncb
            rows = pl.ds(pl.multiple_of(c * CHUNK + b * tb, CHUNK), CHUNK)
            dp = _pair_cols(dp_s, pl.ds(pl.multiple_of(g * SUBLANES, SUBLANES), 1), pair)
            s_ref[b * npair + pair] = s * jnp.concatenate([dp[0], dp[0], dp[1], dp[1]], axis=1) + upd_s[g, pair]
            for h in range(2):
                cols = slice((2 * pair + h) * LANES, (2 * pair + h + 1) * LANES)
                c0 = 2 * h * LANES
                i0 = 4 * pair * LANES + c0
                wi = wi_s[rows, cols]
                num = wi * inter[:, c0:c0 + LANES] + intra_s[rows, i0:i0 + LANES]
                den = wi * inter[:, c0 + LANES:c0 + 2 * LANES] + intra_s[rows, i0 + LANES:i0 + 2 * LANES]
                h_s[rows, cols] = num / jnp.maximum(jnp.abs(den), ed_s[rows, cols])
        return carry

    lax.fori_loop(0, ncb, chunk_step, 0)
    ones = _head_pair_ones(LANES)
    _store_rows(o_ref, _head_rms(h_s[...], ng_ref[...], ones) * _sigmoid(_load_rows(og_ref)))


def mlstm_mix(p_qk, p_v, p_og, p_gates, batch, conv_w, ig_b, fg_b, norm_g):
    n, cols = p_qk.shape
    w = p_v.shape[1]
    nheads = w // LANES
    t = n // batch
    tb = min(SEQ_BLOCK, t)
    rows = batch * tb
    consts = [conv_w, _gate_row(ig_b, 2 * nheads), _gate_row(fg_b, 2 * nheads),
              jnp.tile(norm_g, nheads).reshape(1, w)]
    blk = lambda: pltpu.VMEM((rows, w), f32)
    out = pl.pallas_call(
        _mlstm_body,
        grid=(t // tb,),
        in_specs=_seq_specs(batch, tb, [cols, w, w, GATE_LANES]) + [_const_spec(c.shape) for c in consts],
        out_specs=_seq_specs(batch, tb, [w])[0],
        out_shape=jax.ShapeDtypeStruct((batch, t, w), bf16),
        scratch_shapes=[pltpu.VMEM((batch, SUBLANES, cols), f32),
                        pltpu.VMEM((batch * nheads // 2, LANES, 4 * LANES), f32),
                        pltpu.VMEM((batch * SUBLANES, GATE_LANES), f32), pltpu.VMEM((rows, cols // 2), f32),
                        pltpu.VMEM((rows, cols // 2), f32)] + [blk() for _ in range(7)]
                       + [pltpu.VMEM((rows // CHUNK * SUBLANES, w), f32), pltpu.VMEM((rows, 2 * w), f32),
                          pltpu.VMEM((rows // CHUNK, nheads // 2, LANES, 4 * LANES), f32)],
        compiler_params=_cparams(("arbitrary",)),
        name="mlstm_mix",
    )(p_qk.reshape(batch, t, cols), p_v.reshape(batch, t, w), p_og.reshape(batch, t, w),
      p_gates.reshape(batch, t, GATE_LANES), *consts)
    return out.reshape(n, w)


def kernel(x, norm_mix_g, norm_mlp_g, mlp_up, mlp_down, final_g, ev_w_in, ev_w_out, lru_conv_w, lru_conv_b, lru_wa,
           lru_ba, lru_wx, lru_bx, lru_lambda, rwkv_mu, rwkv_w0, rwkv_w2, rwkv_a0, rwkv_a2, rwkv_g2, rwkv_kk,
           rwkv_ka, rwkv_rk, rwkv_lnw, rwkv_lnb, od_w_in, od_w_out, gdn_conv_w, gdn_a_log, gdn_dt_bias, gdn_norm_g,
           mlstm_conv_w, mlstm_ig_b, mlstm_fg_b, mlstm_norm_g):
    batch, seq, d = x.shape
    depth = norm_mix_g.shape[0]
    xs = x.reshape(batch * seq, d)
    ev_w = ev_w_in.astype(bf16)
    od_w = od_w_in.astype(bf16)
    for l in range(depth):
        if l % 2 == 0:
            e = l // 2
            lru_cols = 2 * lru_lambda.shape[1]
            p_lru, p_rwkv = norm_proj(xs, norm_mix_g[l], [(ev_w[e], ev_w.shape[2])],
                                      [(lru_cols, ev_w.shape[2] - lru_cols)])
            ya = lru_mix(p_lru, batch, lru_conv_w[e], lru_conv_b[e], lru_wa[e], lru_ba[e], lru_wx[e], lru_bx[e],
                         lru_lambda[e])
            yb = rwkv_mix(p_rwkv, batch, rwkv_mu[e], rwkv_w0[e], rwkv_w2[e], rwkv_a0[e], rwkv_a2[e], rwkv_g2[e],
                          rwkv_kk[e], rwkv_ka[e], rwkv_rk[e], rwkv_lnw[e], rwkv_lnb[e])
            w_out = ev_w_out[e]
        else:
            o = l // 2
            weights, widths = _odd_in_weights(od_w[o], gdn_conv_w.shape[2], gdn_norm_g.shape[1] * gdn_a_log.shape[1],
                                              mlstm_conv_w.shape[2], mlstm_norm_g.shape[1] * mlstm_ig_b.shape[1],
                                              gdn_a_log.shape[1], mlstm_ig_b.shape[1])
            p_qkv, p_z, p_mqk, p_mv, p_mog, p_gates = norm_proj(xs, norm_mix_g[l], weights, widths)
            ya = gdn_mix(p_qkv, p_z, p_gates, batch, gdn_conv_w[o], gdn_a_log[o], gdn_dt_bias[o], gdn_norm_g[o])
            yb = mlstm_mix(p_mqk, p_mv, p_mog, p_gates, batch, mlstm_conv_w[o], mlstm_ig_b[o], mlstm_fg_b[o],
                           mlstm_norm_g[o])
            w_out = od_w_out[o]
        xs = out_mlp(ya, yb, xs, w_out.astype(bf16), norm_mlp_g[l], mlp_up[l].astype(bf16),
                     mlp_down[l].astype(bf16), final_g, l == depth - 1)
    return xs.reshape(batch, seq, d)


def _odd_in_weights(w, gdn_qkv, gdn_w, mlstm_qk, mlstm_w, gdn_heads, mlstm_heads):
    lead = gdn_qkv + gdn_w
    m0 = lead + 2 * gdn_heads
    m1 = m0 + mlstm_qk + 2 * mlstm_w
    ngate = 2 * gdn_heads + 2 * mlstm_heads
    gates = jnp.concatenate([w[:, lead:m0], w[:, m1:m1 + 2 * mlstm_heads],
                             jnp.zeros((w.shape[0], GATE_LANES - ngate), w.dtype)], axis=1)
    weights = [(w, lead), (w[:, m0:m1], m1 - m0), (gates, GATE_LANES)]
    return weights, [(gdn_qkv, gdn_w), (mlstm_qk, mlstm_w, mlstm_w), (GATE_LANES,)]
```

```python
import functools
import math

import jax
import jax.numpy as jnp
from jax import lax
from jax.experimental import pallas as pl
from jax.experimental.pallas import tpu as pltpu

f32 = jnp.float32
bf16 = jnp.bfloat16

LANES = 128
SUBLANES = 8
VMEM_LIMIT = 56 * 1024 * 1024

NORM_EPS = 1e-6
CONV_W = 4
LRU_C = 8.0
RWKV_HEAD = 64
RWKV_LN_EPS = 64e-5
CHUNK = 64

ROW_BLOCK = 512
SEQ_BLOCK = 128


def _cparams(sem):
    return pltpu.CompilerParams(dimension_semantics=sem, vmem_limit_bytes=VMEM_LIMIT)


def _const_spec(shape):
    nd = len(shape)
    return pl.BlockSpec(shape, lambda *_: (0,) * nd)


def _layer_spec(w, layer, cols=None):
    return pl.BlockSpec((None, w.shape[1], w.shape[2] if cols is None else cols), lambda *_: (layer, 0, 0))


def _rms(x, g):
    return x * lax.rsqrt(jnp.mean(x * x, axis=-1, keepdims=True) + NORM_EPS) * g


def _softplus(x):
    return jnp.maximum(x, 0.0) + jnp.log(1.0 + jnp.exp(-jnp.abs(x)))


def _sigmoid(x):
    return 0.5 * jnp.tanh(0.5 * x) + 0.5


def _norm_proj_body(widths, x_ref, g_ref, *refs):
    w_refs, o_refs = refs[:len(widths)], list(refs[len(widths):])
    h = _rms(x_ref[...], g_ref[...]).astype(bf16)
    for w_ref, group in zip(w_refs, widths):
        c0 = 0
        for wd in group:
            o_ref = o_refs.pop(0)
            o_ref[...] = jnp.dot(h, w_ref[:, c0:c0 + wd], preferred_element_type=f32).astype(o_ref.dtype)
            c0 += wd


def norm_proj(x, g, weights, widths):
    n, d = x.shape
    tm = min(ROW_BLOCK, n)
    flat = [wd for group in widths for wd in group]
    return pl.pallas_call(
        functools.partial(_norm_proj_body, widths),
        grid=(n // tm,),
        in_specs=[pl.BlockSpec((tm, d), lambda i: (i, 0)), _const_spec((1, d))] + [spec for _, spec in weights],
        out_specs=[pl.BlockSpec((tm, wd), lambda i: (i, 0)) for wd in flat],
        out_shape=[jax.ShapeDtypeStruct((n, wd), f32) for wd in flat],
        compiler_params=_cparams(("parallel",)),
        name="norm_proj",
    )(x, g.reshape(1, d), *[w for w, _ in weights])


FF_BLOCK = 1024


def _out_mlp_body(final, ya_ref, yb_ref, x_ref, wo_ref, g_ref, wu_ref, wd_ref, gf_ref, o_ref):
    half = ya_ref.shape[1]
    y = jnp.dot(ya_ref[...], wo_ref[0:half, :], preferred_element_type=f32)
    y = y + jnp.dot(yb_ref[...], wo_ref[half:, :], preferred_element_type=f32)
    x1 = x_ref[...] + y
    h = _rms(x1, g_ref[...]).astype(bf16)
    acc = x1
    for c in range(wu_ref.shape[1] // FF_BLOCK):
        u = jnp.dot(h, wu_ref[:, c * FF_BLOCK:(c + 1) * FF_BLOCK], preferred_element_type=f32)
        u = jnp.square(jnp.maximum(u, 0.0)).astype(bf16)
        acc = acc + jnp.dot(u, wd_ref[c * FF_BLOCK:(c + 1) * FF_BLOCK, :], preferred_element_type=f32)
    if final:
        acc = _rms(acc, gf_ref[...])
    o_ref[...] = acc


def out_mlp(ya, yb, x, wo, g, wu, wd, gf, final):
    n, d = x.shape
    half = ya.shape[1]
    tm = min(ROW_BLOCK, n)
    row = lambda i: (i, 0)
    return pl.pallas_call(
        functools.partial(_out_mlp_body, final),
        grid=(n // tm,),
        in_specs=[pl.BlockSpec((tm, half), row), pl.BlockSpec((tm, half), row), pl.BlockSpec((tm, d), row),
                  _layer_spec(*wo), _const_spec((1, d)), _layer_spec(*wu), _layer_spec(*wd), _const_spec((1, d))],
        out_specs=pl.BlockSpec((tm, d), row),
        out_shape=jax.ShapeDtypeStruct((n, d), f32),
        compiler_params=_cparams(("parallel",)),
        name="out_mlp",
    )(ya, yb, x, wo[0], g.reshape(1, d), wu[0], wd[0], gf.reshape(1, d))


def _seq_specs(batch, tb, widths):
    return [pl.BlockSpec((batch, tb, wd), lambda i: (0, i, 0)) for wd in widths]


def _load_rows(ref):
    return jnp.concatenate([ref[b] for b in range(ref.shape[0])], axis=0)


def _store_rows(ref, x):
    tb = ref.shape[1]
    for b in range(ref.shape[0]):
        ref[b] = x[b * tb:(b + 1) * tb].astype(ref.dtype)


def _per_batch(fn, x, prev_ref):
    nb = prev_ref.shape[0]
    tb = x.shape[0] // nb
    out = []
    for b in range(nb):
        xb = x[b * tb:(b + 1) * tb]
        out.append(fn(xb, prev_ref[b]))
        prev_ref[b] = xb[tb - SUBLANES:]
    return jnp.concatenate(out, axis=0)


def _shift_rows(x, prev, k):
    rolled = pltpu.roll(x, k, axis=0)
    head = pltpu.roll(prev, k, axis=0)
    rows = lax.broadcasted_iota(jnp.int32, (SUBLANES, x.shape[1]), 0)
    top = jnp.where(rows < k, head, rolled[0:SUBLANES])
    return jnp.concatenate([top, rolled[SUBLANES:]], axis=0)


def _causal_conv(x, prev, w):
    y = x * w[CONV_W - 1:CONV_W]
    for k in range(1, CONV_W):
        y = y + _shift_rows(x, prev, k) * w[CONV_W - 1 - k:CONV_W - k]
    return y


def _scan_shift(x, s, fill):
    n, c = x.shape
    if s % SUBLANES == 0:
        return jnp.concatenate([jnp.full((s, c), fill, x.dtype), x[:n - s]], axis=0)
    rolled = pltpu.roll(x, s, axis=0)
    rows = lax.broadcasted_iota(jnp.int32, (SUBLANES, c), 0)
    top = jnp.where(rows < s, fill, rolled[0:SUBLANES])
    return jnp.concatenate([top, rolled[SUBLANES:]], axis=0)


def _linear_scan(a, u):
    s = 1
    while s < a.shape[0]:
        u = a * _scan_shift(u, s, 0.0) + u
        a = a * _scan_shift(a, s, 1.0)
        s *= 2
    return a, u


def _lru_body(p_ref, cw_ref, cb_ref, wg_ref, bg_ref, lam_ref, o_ref, xprev_ref, h_ref):
    nb, tb = p_ref.shape[0], p_ref.shape[1]
    w = o_ref.shape[2]
    step = pl.program_id(0)

    @pl.when(step == 0)
    def _():
        xprev_ref[...] = jnp.zeros_like(xprev_ref)
        h_ref[...] = jnp.zeros_like(h_ref)

    gate_in = jnp.concatenate([p_ref[b, :, 0:w] for b in range(nb)], axis=0)
    rec = jnp.concatenate([p_ref[b, :, w:] for b in range(nb)], axis=0)
    xc = _per_batch(lambda x, prev: _causal_conv(x, prev, cw_ref[...]), rec, xprev_ref) + cb_ref[...]
    gates = jnp.dot(xc.astype(bf16), wg_ref[...], preferred_element_type=f32) + bg_ref[...]
    r_gate = _sigmoid(gates[:, 0:w])
    i_gate = _sigmoid(gates[:, w:])
    log_a = (-LRU_C) * r_gate * _softplus(-lam_ref[...])
    a = jnp.exp(log_a)
    mult = jnp.sqrt(1.0 - jnp.exp(2.0 * log_a))
    rows = lax.broadcasted_iota(jnp.int32, (nb * tb, w), 0)
    mult = jnp.where(jnp.logical_and(rows % tb == 0, step == 0), 1.0, mult)
    u = i_gate * xc * mult
    hs = []
    for b in range(nb):
        a_cum, h = _linear_scan(a[b * tb:(b + 1) * tb], u[b * tb:(b + 1) * tb])
        h = h + a_cum * h_ref[b * SUBLANES:b * SUBLANES + 1, :]
        h_ref[b * SUBLANES:b * SUBLANES + 1, :] = h[tb - 1:tb]
        hs.append(h)
    _store_rows(o_ref, jnp.concatenate(hs, axis=0) * jax.nn.gelu(gate_in, approximate=True))


def _block_diag(wb):
    nb, d, e = wb.shape
    eye = jnp.eye(nb, dtype=wb.dtype)
    return (wb[:, :, None, :] * eye[:, None, :, None]).reshape(nb * d, nb * e)


def lru_mix(p, batch, conv_w, conv_b, wa, ba, wx, bx, lam):
    n, c2 = p.shape
    w = c2 // 2
    t = n // batch
    tb = min(SEQ_BLOCK, t)
    wg = jnp.concatenate([_block_diag(wa), _block_diag(wx)], axis=1).astype(bf16)
    bg = jnp.concatenate([ba, bx]).reshape(1, 2 * w)
    out = pl.pallas_call(
        _lru_body,
        grid=(t // tb,),
        in_specs=_seq_specs(batch, tb, [c2]) + [_const_spec((CONV_W, w)), _const_spec((1, w)),
                                                _const_spec((w, 2 * w)), _const_spec((1, 2 * w)), _const_spec((1, w))],
        out_specs=_seq_specs(batch, tb, [w])[0],
        out_shape=jax.ShapeDtypeStruct((batch, t, w), bf16),
        scratch_shapes=[pltpu.VMEM((batch, SUBLANES, w), f32), pltpu.VMEM((batch * SUBLANES, w), f32)],
        compiler_params=_cparams(("arbitrary",)),
        name="lru_mix",
    )(p.reshape(batch, t, c2), conv_w, conv_b.reshape(1, w), wg, bg, lam.reshape(1, w))
    return out.reshape(n, w)


UNIT_GROUP = 32


def _dot(a, b):
    return jnp.dot(a.astype(bf16), b.astype(bf16), preferred_element_type=f32)


def _dot_nt(a, b):
    return lax.dot_general(a.astype(bf16), b.astype(bf16), (((1,), (1,)), ((), ())), preferred_element_type=f32)


def _dot_tn(a, b):
    return lax.dot_general(a.astype(bf16), b.astype(bf16), (((0,), (0,)), ((), ())), preferred_element_type=f32)


def _dot_split(x, e, terms=2):
    out = None
    for _ in range(terms):
        hi = x.astype(bf16)
        x = x - hi.astype(f32)
        part = jnp.dot(hi, e, preferred_element_type=f32)
        out = part if out is None else out + part
    return out


def _iota(shape, axis):
    return lax.broadcasted_iota(jnp.int32, shape, axis)


def _tri_incl():
    return (_iota((CHUNK, CHUNK), 0) >= _iota((CHUNK, CHUNK), 1)).astype(bf16)


def _chunk_cumsum(x):
    tri = _tri_incl()
    cs, ce = [], []
    for c in range(x.shape[0] // CHUNK):
        rest = x[c * CHUNK:(c + 1) * CHUNK]
        cs_c = None
        for _ in range(3):
            hi = rest.astype(bf16)
            rest = rest - hi.astype(f32)
            part = jnp.dot(tri, hi, preferred_element_type=f32)
            cs_c = part if cs_c is None else cs_c + part
        cs.append(cs_c)
        ce.append(jnp.broadcast_to(cs_c[CHUNK - 1:CHUNK], cs_c.shape))
    return jnp.concatenate(cs, axis=0), jnp.concatenate(ce, axis=0)


def _head_pair_ones(head):
    return (_iota((LANES, LANES), 0) // head == _iota((LANES, LANES), 1) // head).astype(bf16)


def _segsum(x, e):
    return jnp.concatenate([_dot_split(x[:, j:j + LANES], e) for j in range(0, x.shape[1], LANES)], axis=1)


def _stack_heads(x):
    lo = _iota(x.shape, 1) < RWKV_HEAD
    return jnp.concatenate([jnp.where(lo, x, 0.0), jnp.where(lo, 0.0, x)], axis=0)


def _unit_lower_inverse(amats):
    shape = amats[0].shape
    ii = _iota(shape, 0)
    jj = _iota(shape, 1) % CHUNK
    eye = (ii == jj).astype(f32)
    first = jnp.logical_and(ii // 2 == jj // 2, ii > jj)
    ts = [eye - jnp.where(first, a, 0.0) for a in amats]
    abf = [a.astype(bf16) for a in amats]
    b = 2
    while b < CHUNK:
        off = jnp.logical_and(ii // (2 * b) == jj // (2 * b), ii // b > jj // b)
        xs = [_dot(a, _stack_heads(t)) for a, t in zip(abf, ts)]
        ts = [t - jnp.where(off, _dot(t, _stack_heads(x)), 0.0) for t, x in zip(ts, xs)]
        b *= 2
    return ts


def _rwkv_body(p_ref, mu_ref, w0_ref, w2_ref, a0_ref, a2_ref, g2_ref, kk_ref, ka_ref, rk_ref, lnw_ref, lnb_ref,
               o_ref, prev_ref, s_ref, at_s, rt_s, bt_s, kt_s, gc_s, v_s, y_s, wh_s, uh_s, mrb_s):
    nb, tb = p_ref.shape[0], p_ref.shape[1]
    n = nb * tb
    w = o_ref.shape[2]
    npair = w // LANES

    @pl.when(pl.program_id(0) == 0)
    def _():
        prev_ref[...] = jnp.zeros_like(prev_ref)
        s_ref[...] = jnp.zeros_like(s_ref)

    p = _load_rows(p_ref)
    pf = p + mu_ref[...] * (_per_batch(lambda x, prev: _shift_rows(x, prev, 1), p, prev_ref) - p)
    r, k, v = pf[:, 0:w], pf[:, w:2 * w], pf[:, 2 * w:3 * w]
    lowrank = pf[:, 3 * w:3 * w + LANES]
    xg = pf[:, 3 * w + LANES:]
    lw = (-math.exp(-0.5)) * _sigmoid(w0_ref[...] + _dot(jnp.tanh(lowrank), w2_ref[...]))
    a = _sigmoid(a0_ref[...] + _dot(lowrank, a2_ref[...]))
    g = _dot(_sigmoid(xg), g2_ref[...])
    ones = _head_pair_ones(RWKV_HEAD)
    kk = k * kk_ref[...]
    kk = kk * lax.rsqrt(_segsum(kk * kk, ones) + 1e-6)
    k = k * (1.0 + (a - 1.0) * ka_ref[...])
    cs, ce = _chunk_cumsum(lw)
    inv_g = jnp.exp(-cs)
    at_s[...] = kk * jnp.exp(cs - lw)
    rt_s[...] = r * jnp.exp(cs)
    bt_s[...] = -(kk * a) * inv_g
    kt_s[...] = k * inv_g
    gc_s[...] = jnp.exp(ce)
    v_s[...] = v

    tt = _iota((CHUNK, LANES), 0)
    ss = _iota((CHUNK, LANES), 1) % CHUNK
    strict = tt > ss
    incl = tt >= ss
    diag_blocks = (_iota((LANES, LANES), 0) < RWKV_HEAD) == (_iota((LANES, LANES), 1) < RWKV_HEAD)

    pair_cols = [slice(j * LANES, (j + 1) * LANES) for j in range(npair)]
    all_units = [(slice(c * CHUNK, (c + 1) * CHUNK), cols) for c in range(n // CHUNK) for cols in pair_cols]
    for u0 in range(0, len(all_units), UNIT_GROUP):
        units = all_units[u0:u0 + UNIT_GROUP]
        gms = [_dot_nt(jnp.concatenate([at_s[u], rt_s[u]], axis=0),
                       jnp.concatenate([_stack_heads(bt_s[u]), _stack_heads(kt_s[u])], axis=0)) for u in units]
        for u, gm in zip(units, gms):
            mrb_s[u] = jnp.where(incl, gm[CHUNK:, 0:LANES], 0.0)
        vsts = [_stack_heads(v_s[u]) for u in units]
        akvs = [_dot(jnp.where(strict, gm[0:CHUNK, LANES:], 0.0), vst) for gm, vst in zip(gms, vsts)]
        yhs = [_dot(jnp.where(incl, gm[CHUNK:, LANES:], 0.0), vst) for gm, vst in zip(gms, vsts)]
        for u, yh in zip(units, yhs):
            y_s[u] = yh
        tinvs = _unit_lower_inverse([jnp.where(strict, -gm[0:CHUNK, 0:LANES], 0.0) for gm in gms])
        wus = [_dot(t, jnp.concatenate([_stack_heads(at_s[u]), _stack_heads(akv)], axis=1))
               for u, t, akv in zip(units, tinvs, akvs)]
        for u, wu in zip(units, wus):
            wh_s[u] = wu[:, 0:LANES]
            uh_s[u] = wu[:, LANES:]

    def chunk_step(c, carry):
        starts = [pl.multiple_of(c * CHUNK + b * tb, CHUNK) for b in range(nb)]
        chains = [(b * npair + j, pl.ds(starts[b], CHUNK), cols) for b in range(nb) for j, cols in enumerate(pair_cols)]
        ss = [s_ref[si] for si, _, _ in chains]
        xs = [_dot_nt(jnp.concatenate([wh_s[rows, cols], rt_s[rows, cols]], axis=0), s)
              for (_, rows, cols), s in zip(chains, ss)]
        us = [x[0:CHUNK] + uh_s[rows, cols] for (_, rows, cols), x in zip(chains, xs)]
        upds = [_dot_tn(jnp.concatenate([u, v_s[rows, cols]], axis=0),
                        jnp.concatenate([bt_s[rows, cols], kt_s[rows, cols]], axis=0))
                for (_, rows, cols), u in zip(chains, us)]
        ys = [_dot(mrb_s[rows, cols], _stack_heads(u)) for (_, rows, cols), u in zip(chains, us)]
        for (si, rows, cols), s, x, upd, y in zip(chains, ss, xs, upds, ys):
            s_ref[si] = (s + jnp.where(diag_blocks, upd, 0.0)) * gc_s[pl.ds(starts[si // npair], 1), cols]
            y_s[rows, cols] = y_s[rows, cols] + x[CHUNK:] + y
        return carry

    lax.fori_loop(0, tb // CHUNK, chunk_step, 0)

    y = y_s[...]
    inv_head = 1.0 / RWKV_HEAD
    mean = _segsum(y, ones) * inv_head
    yc = y - mean
    var = _segsum(yc * yc, ones) * inv_head
    yn = yc * lax.rsqrt(var + RWKV_LN_EPS) * lnw_ref[...] + lnb_ref[...]
    bonus = _segsum(r * k * rk_ref[...], ones) * v
    _store_rows(o_ref, (yn + bonus) * g)


def rwkv_mix(p, batch, mu, w0, w2, a0, a2, g2, k_k, k_a, r_k, ln_w, ln_b):
    n, cols = p.shape
    w = w0.shape[0]
    t = n // batch
    tb = min(SEQ_BLOCK, t)
    rank = w2.shape[0]
    zeros = jnp.zeros((LANES - rank, w), f32)
    w2p = jnp.concatenate([w2, zeros], axis=0).astype(bf16)
    a2p = jnp.concatenate([zeros, a2], axis=0).astype(bf16)
    row = lambda v: v.reshape(1, -1)
    consts = [row(mu), row(w0), w2p, row(a0), a2p, g2.astype(bf16), row(k_k), row(k_a), row(r_k), row(ln_w),
              row(ln_b)]
    blk = lambda: pltpu.VMEM((batch * tb, w), f32)
    out = pl.pallas_call(
        _rwkv_body,
        grid=(t // tb,),
        in_specs=_seq_specs(batch, tb, [cols]) + [_const_spec(c.shape) for c in consts],
        out_specs=_seq_specs(batch, tb, [w])[0],
        out_shape=jax.ShapeDtypeStruct((batch, t, w), bf16),
        scratch_shapes=[pltpu.VMEM((batch, SUBLANES, cols), f32),
                        pltpu.VMEM((batch * w // LANES, LANES, LANES), f32)] + [blk() for _ in range(10)],
        compiler_params=_cparams(("arbitrary",)),
        name="rwkv_mix",
    )(p.reshape(batch, t, cols), *consts)
    return out.reshape(n, w)


GATE_LANES = LANES


def _silu(x):
    return x * _sigmoid(x)


def _expand_gate(gates, first, nheads, terms=2):
    sel = (_iota((GATE_LANES, nheads * LANES), 0) - first == _iota((GATE_LANES, nheads * LANES), 1) // LANES)
    return _dot_split(gates, sel.astype(bf16), terms)


def _gate_row(v, first):
    return jnp.zeros((1, GATE_LANES), f32).at[0, first:first + v.shape[0]].set(v)


def _pair_cols(x_ref, rows, pair):
    c = 2 * pair * LANES
    return x_ref[rows, c:c + LANES], x_ref[rows, c + LANES:c + 2 * LANES]


def _pair_select(x0, x1):
    return jnp.where(_iota(x0.shape, 1) < CHUNK, x0, x1)


def _pair_row_form(x0, x1):
    return jnp.transpose(jnp.concatenate([x0, x1], axis=0))[0:CHUNK]


def _pair_block_rhs(x0, x1):
    return jnp.concatenate([jnp.concatenate([x0, jnp.zeros_like(x0)], axis=1),
                            jnp.concatenate([jnp.zeros_like(x1), x1], axis=1)], axis=0)


def _head_rms(o, g, ones):
    return o * lax.rsqrt(_segsum(o * o, ones) * (1.0 / LANES) + NORM_EPS) * g


def _gdn_body(p_ref, z_ref, gate_ref, cw_ref, alog_ref, dtb_ref, ng_ref, o_ref,
              prev_ref, s_ref, q_s, k_s, kb_s, vb_s, qd_s, kbg_s, kd_s, gc_s, gl_s, o_s, u_s, w_s, qk_s):
    nb, tb = p_ref.shape[0], p_ref.shape[1]
    n = nb * tb
    w = o_ref.shape[2]
    nheads = w // LANES

    @pl.when(pl.program_id(0) == 0)
    def _():
        prev_ref[...] = jnp.zeros_like(prev_ref)
        s_ref[...] = jnp.zeros_like(s_ref)

    qkv = _silu(_per_batch(lambda x, prev: _causal_conv(x, prev, cw_ref[...]), _load_rows(p_ref), prev_ref))
    ones = _head_pair_ones(LANES)
    q, k, v = qkv[:, 0:w], qkv[:, w:2 * w], qkv[:, 2 * w:]
    q = q * lax.rsqrt(_segsum(q * q, ones) + 1e-6) * (LANES ** -0.5)
    k = k * lax.rsqrt(_segsum(k * k, ones) + 1e-6)
    gates = _load_rows(gate_ref)
    beta = _expand_gate(_sigmoid(gates), 0, nheads)
    gc_c, ge_c = _chunk_cumsum(-jnp.exp(alog_ref[...]) * _softplus(gates + dtb_ref[...]))
    eg = _expand_gate(jnp.exp(gc_c), nheads, nheads)
    kb = k * beta
    q_s[...] = q
    k_s[...] = k
    kb_s[...] = kb
    vb_s[...] = v * beta
    qd_s[...] = q * eg
    kbg_s[...] = kb * eg
    kd_s[...] = k * _expand_gate(jnp.exp(ge_c - gc_c), nheads, nheads)
    gc_s[...] = _expand_gate(gc_c, nheads, nheads, terms=3)
    gl_s[...] = _expand_gate(jnp.exp(ge_c), nheads, nheads)

    tt = _iota((CHUNK, LANES), 0)
    ss = _iota((CHUNK, LANES), 1) % CHUNK
    strict = tt > ss
    incl = tt >= ss

    all_units = [(slice(c * CHUNK, (c + 1) * CHUNK), pair) for c in range(n // CHUNK) for pair in range(nheads // 2)]
    for u0 in range(0, len(all_units), UNIT_GROUP):
        units = all_units[u0:u0 + UNIT_GROUP]
        xs = [_dot_nt(jnp.concatenate(_pair_cols(kb_s, rows, pair) + _pair_cols(q_s, rows, pair), axis=0),
                      jnp.concatenate(_pair_cols(k_s, rows, pair), axis=0)) for rows, pair in units]
        amats = []
        for (rows, pair), x in zip(units, xs):
            g0, g1 = _pair_cols(gc_s, rows, pair)
            dec = jnp.exp(jnp.minimum(_pair_select(g0, g1) - _pair_row_form(g0, g1), 0.0))
            amats.append(jnp.where(strict, _pair_select(x[0:CHUNK], x[CHUNK:2 * CHUNK]) * dec, 0.0))
            qk_s[rows, pair * LANES:(pair + 1) * LANES] = jnp.where(
                incl, _pair_select(x[2 * CHUNK:3 * CHUNK], x[3 * CHUNK:]) * dec, 0.0)
        tinvs = _unit_lower_inverse(amats)
        uws = []
        for (rows, pair), tinv in zip(units, tinvs):
            vb0, vb1 = _pair_cols(vb_s, rows, pair)
            kg0, kg1 = _pair_cols(kbg_s, rows, pair)
            uws.append(_dot(tinv, _pair_block_rhs(jnp.concatenate([vb0, kg0], axis=1),
                                                  jnp.concatenate([vb1, kg1], axis=1))))
        for (rows, pair), uw in zip(units, uws):
            for h in range(2):
                cols = slice((2 * pair + h) * LANES, (2 * pair + h + 1) * LANES)
                u_s[rows, cols] = uw[:, 2 * h * LANES:(2 * h + 1) * LANES]
                w_s[rows, cols] = uw[:, (2 * h + 1) * LANES:(2 * h + 2) * LANES]

    head_cols = [slice(h * LANES, (h + 1) * LANES) for h in range(nheads)]

    def chunk_step(c, carry):
        starts = [pl.multiple_of(c * CHUNK + b * tb, CHUNK) for b in range(nb)]
        chains = [(b, h, pl.ds(starts[b], CHUNK), cols) for b in range(nb) for h, cols in enumerate(head_cols)]
        ss = [s_ref[b * nheads + h] for b, h, _, _ in chains]
        x2s = [_dot(jnp.concatenate([w_s[rows, cols], qd_s[rows, cols]], axis=0), s)
               for (_, _, rows, cols), s in zip(chains, ss)]
        vns = [u_s[rows, cols] - x2[0:CHUNK] for (_, _, rows, cols), x2 in zip(chains, x2s)]
        upds = [_dot_tn(kd_s[rows, cols], vn) for (_, _, rows, cols), vn in zip(chains, vns)]
        os = [_dot(qk_s[pl.ds(starts[b], CHUNK), pair * LANES:(pair + 1) * LANES],
                   _pair_block_rhs(vns[b * nheads + 2 * pair], vns[b * nheads + 2 * pair + 1]))
              for b in range(nb) for pair in range(nheads // 2)]
        for ci, (b, h, rows, cols) in enumerate(chains):
            s_ref[ci] = ss[ci] * gl_s[pl.ds(starts[b], 1), cols] + upds[ci]
            o_s[rows, cols] = x2s[ci][CHUNK:] + os[ci // 2][:, (h % 2) * LANES:(h % 2 + 1) * LANES]
        return carry

    lax.fori_loop(0, tb // CHUNK, chunk_step, 0)
    _store_rows(o_ref, _head_rms(o_s[...], ng_ref[...], ones) * _silu(_load_rows(z_ref)))


def gdn_mix(p_qkv, p_z, p_gates, batch, conv_w, a_log, dt_bias, norm_g):
    n, cols = p_qkv.shape
    w = p_z.shape[1]
    nheads = w // LANES
    t = n // batch
    tb = min(SEQ_BLOCK, t)
    consts = [conv_w, _gate_row(a_log, nheads), _gate_row(dt_bias, nheads), jnp.tile(norm_g, nheads).reshape(1, w)]
    blk = lambda: pltpu.VMEM((batch * tb, w), f32)
    out = pl.pallas_call(
        _gdn_body,
        grid=(t // tb,),
        in_specs=_seq_specs(batch, tb, [cols, w, GATE_LANES]) + [_const_spec(c.shape) for c in consts],
        out_specs=_seq_specs(batch, tb, [w])[0],
        out_shape=jax.ShapeDtypeStruct((batch, t, w), bf16),
        scratch_shapes=[pltpu.VMEM((batch, SUBLANES, cols), f32), pltpu.VMEM((batch * nheads, LANES, LANES), f32)]
                       + [blk() for _ in range(12)] + [pltpu.VMEM((batch * tb, w // 2), f32)],
        compiler_params=_cparams(("arbitrary",)),
        name="gdn_mix",
    )(p_qkv.reshape(batch, t, cols), p_z.reshape(batch, t, w), p_gates.reshape(batch, t, GATE_LANES), *consts)
    return out.reshape(n, w)


def _cummax_rows(x):
    out = []
    for c in range(x.shape[0] // CHUNK):
        y = x[c * CHUNK:(c + 1) * CHUNK]
        s = 1
        while s < CHUNK:
            y = jnp.maximum(y, _scan_shift(y, s, -jnp.inf))
            s *= 2
        out.append(y)
    return jnp.concatenate(out, axis=0)


def _mlstm_body(pqk_ref, v_ref, og_ref, gate_ref, cw_ref, igb_ref, fgb_ref, ng_ref, o_ref,
                prev_ref, s_ref, m_ref, q_s, k_s, cc_s, mx_s, wi_s, ws_s, ed_s, h_s, v_s, dp_s, intra_s, upd_s):
    nb, tb = pqk_ref.shape[0], pqk_ref.shape[1]
    n = nb * tb
    w = o_ref.shape[2]
    nheads = w // LANES
    hq = pqk_ref.shape[2] // 2

    @pl.when(pl.program_id(0) == 0)
    def _():
        prev_ref[...] = jnp.zeros_like(prev_ref)
        s_ref[...] = jnp.zeros_like(s_ref)
        m_ref[...] = jnp.zeros_like(m_ref)

    qk = _silu(_per_batch(lambda x, prev: _causal_conv(x, prev, cw_ref[...]), _load_rows(pqk_ref), prev_ref))
    q_s[...] = qk[:, 0:hq] * ((hq // nheads) ** -0.5)
    k_s[...] = qk[:, hq:]
    v_s[...] = _load_rows(v_ref)
    gates = _load_rows(gate_ref)
    first = 2 * nheads
    i_pre = gates + igb_ref[...]
    f_pre = pltpu.roll(gates, GATE_LANES - nheads, axis=1) + fgb_ref[...]
    b, b_end = _chunk_cumsum(-_softplus(-f_pre))
    cc = i_pre - b
    cm = _cummax_rows(cc)
    te = b_end + cc
    nc = n // CHUNK
    ncb = tb // CHUNK
    mx, w_inter, wk_scale, dprev = [], [], [], []
    for c in range(nc):
        bi = c // ncb
        m = m_ref[bi * SUBLANES:bi * SUBLANES + 1, :] if c % ncb == 0 else m
        rows = slice(c * CHUNK, (c + 1) * CHUNK)
        b_last = b_end[c * CHUNK:c * CHUNK + 1]
        m_new = jnp.maximum(b_last + m, jnp.max(te[rows], axis=0, keepdims=True))
        mx_c = jnp.maximum(m, cm[rows])
        mx.append(mx_c)
        w_inter.append(jnp.exp(m - mx_c))
        wk_scale.append(jnp.exp(te[rows] - m_new))
        dprev.append(jnp.broadcast_to(jnp.exp(b_last + m - m_new), (SUBLANES, GATE_LANES)))
        m = m_new
        if c % ncb == ncb - 1:
            m_ref[bi * SUBLANES:bi * SUBLANES + 1, :] = m
    mx = jnp.concatenate(mx, axis=0)
    cc_s[...] = _expand_gate(cc, first, nheads, terms=3)
    mx_s[...] = _expand_gate(mx, first, nheads, terms=3)
    wi_s[...] = _expand_gate(jnp.concatenate(w_inter, axis=0), first, nheads)
    ws_s[...] = _expand_gate(jnp.concatenate(wk_scale, axis=0), first, nheads)
    ed_s[...] = jnp.exp(-_expand_gate(b + mx, first, nheads, terms=3))
    dp_s[...] = _expand_gate(jnp.concatenate(dprev, axis=0), first, nheads)

    incl = _iota((CHUNK, LANES), 0) >= _iota((CHUNK, LANES), 1) % CHUNK
    rows128 = _iota((LANES, 4 * LANES), 0) < CHUNK
    lanes512 = _iota((LANES, 4 * LANES), 1) < 2 * LANES
    diag_blocks = rows128 == lanes512
    ones_v = jnp.ones((CHUNK, LANES), f32)
    npair = nheads // 2
    pair_cols = [slice(pair * LANES, (pair + 1) * LANES) for pair in range(npair)]

    def vext(rows, pair):
        v0, v1 = _pair_cols(v_s, rows, pair)
        return jnp.concatenate([v0, ones_v, v1, ones_v], axis=1)

    units = [(c, slice(c * CHUNK, (c + 1) * CHUNK), pair) for c in range(nc) for pair in range(npair)]
    qks = [_dot_nt(q_s[rows, pair_cols[pair]], _stack_heads(k_s[rows, pair_cols[pair]])) for _, rows, pair in units]
    wms = []
    for (_, rows, pair), qk_u in zip(units, qks):
        expo = _pair_row_form(*_pair_cols(cc_s, rows, pair)) - _pair_select(*_pair_cols(mx_s, rows, pair))
        wms.append(jnp.where(incl, jnp.exp(jnp.minimum(expo, 0.0)) * qk_u, 0.0))
    for (_, rows, pair), wm in zip(units, wms):
        ve = vext(rows, pair)
        intra_s[rows, 4 * pair * LANES:4 * (pair + 1) * LANES] = _dot(
            wm, jnp.where(diag_blocks, jnp.concatenate([ve, ve], axis=0), 0.0))
    for c, rows, pair in units:
        wk = k_s[rows, pair_cols[pair]] * _pair_select(*_pair_cols(ws_s, rows, pair))
        upd_s[c, pair] = jnp.where(diag_blocks, _dot_tn(wk, vext(rows, pair)), 0.0)

    def chunk_step(c, carry):
        chains = [(b, pair) for b in range(nb) for pair in range(npair)]
        ss = [s_ref[b * npair + pair] for b, pair in chains]
        inters = [_dot(q_s[pl.ds(pl.multiple_of(c * CHUNK + b * tb, CHUNK), CHUNK), pair_cols[pair]], s)
                  for (b, pair), s in zip(chains, ss)]
        for (b, pair), s, inter in zip(chains, ss, inters):
            g = c + b * ncb
            rows = pl.ds(pl.multiple_of(c * CHUNK + b * tb, CHUNK), CHUNK)
            dp = _pair_cols(dp_s, pl.ds(pl.multiple_of(g * SUBLANES, SUBLANES), 1), pair)
            s_ref[b * npair + pair] = s * jnp.concatenate([dp[0], dp[0], dp[1], dp[1]], axis=1) + upd_s[g, pair]
            for h in range(2):
                cols = slice((2 * pair + h) * LANES, (2 * pair + h + 1) * LANES)
                c0 = 2 * h * LANES
                i0 = 4 * pair * LANES + c0
                wi = wi_s[rows, cols]
                num = wi * inter[:, c0:c0 + LANES] + intra_s[rows, i0:i0 + LANES]
                den = wi * inter[:, c0 + LANES:c0 + 2 * LANES] + intra_s[rows, i0 + LANES:i0 + 2 * LANES]
                h_s[rows, cols] = num / jnp.maximum(jnp.abs(den), ed_s[rows, cols])
        return carry

    lax.fori_loop(0, ncb, chunk_step, 0)
    ones = _head_pair_ones(LANES)
    _store_rows(o_ref, _head_rms(h_s[...], ng_ref[...], ones) * _sigmoid(_load_rows(og_ref)))


def mlstm_mix(p_qk, p_v, p_og, p_gates, batch, conv_w, ig_b, fg_b, norm_g):
    n, cols = p_qk.shape
    w = p_v.shape[1]
    nheads = w // LANES
    t = n // batch
    tb = min(SEQ_BLOCK, t)
    rows = batch * tb
    consts = [conv_w, _gate_row(ig_b, 2 * nheads), _gate_row(fg_b, 2 * nheads),
              jnp.tile(norm_g, nheads).reshape(1, w)]
    blk = lambda: pltpu.VMEM((rows, w), f32)
    out = pl.pallas_call(
        _mlstm_body,
        grid=(t // tb,),
        in_specs=_seq_specs(batch, tb, [cols, w, w, GATE_LANES]) + [_const_spec(c.shape) for c in consts],
        out_specs=_seq_specs(batch, tb, [w])[0],
        out_shape=jax.ShapeDtypeStruct((batch, t, w), bf16),
        scratch_shapes=[pltpu.VMEM((batch, SUBLANES, cols), f32),
                        pltpu.VMEM((batch * nheads // 2, LANES, 4 * LANES), f32),
                        pltpu.VMEM((batch * SUBLANES, GATE_LANES), f32), pltpu.VMEM((rows, cols // 2), f32),
                        pltpu.VMEM((rows, cols // 2), f32)] + [blk() for _ in range(7)]
                       + [pltpu.VMEM((rows // CHUNK * SUBLANES, w), f32), pltpu.VMEM((rows, 2 * w), f32),
                          pltpu.VMEM((rows // CHUNK, nheads // 2, LANES, 4 * LANES), f32)],
        compiler_params=_cparams(("arbitrary",)),
        name="mlstm_mix",
    )(p_qk.reshape(batch, t, cols), p_v.reshape(batch, t, w), p_og.reshape(batch, t, w),
      p_gates.reshape(batch, t, GATE_LANES), *consts)
    return out.reshape(n, w)


def kernel(x, norm_mix_g, norm_mlp_g, mlp_up, mlp_down, final_g, ev_w_in, ev_w_out, lru_conv_w, lru_conv_b, lru_wa,
           lru_ba, lru_wx, lru_bx, lru_lambda, rwkv_mu, rwkv_w0, rwkv_w2, rwkv_a0, rwkv_a2, rwkv_g2, rwkv_kk,
           rwkv_ka, rwkv_rk, rwkv_lnw, rwkv_lnb, od_w_in, od_w_out, gdn_conv_w, gdn_a_log, gdn_dt_bias, gdn_norm_g,
           mlstm_conv_w, mlstm_ig_b, mlstm_fg_b, mlstm_norm_g):
    batch, seq, d = x.shape
    depth = norm_mix_g.shape[0]
    xs = x.reshape(batch * seq, d)
    ev_w, od_w, ev_wo, od_wo = (w.astype(bf16) for w in (ev_w_in, od_w_in, ev_w_out, od_w_out))
    up_w, down_w = mlp_up.astype(bf16), mlp_down.astype(bf16)
    for l in range(depth):
        if l % 2 == 0:
            e = l // 2
            lru_cols = 2 * lru_lambda.shape[1]
            p_lru, p_rwkv = norm_proj(xs, norm_mix_g[l], [(ev_w, _layer_spec(ev_w, e))],
                                      [(lru_cols, ev_w.shape[2] - lru_cols)])
            ya = lru_mix(p_lru, batch, lru_conv_w[e], lru_conv_b[e], lru_wa[e], lru_ba[e], lru_wx[e], lru_bx[e],
                         lru_lambda[e])
            yb = rwkv_mix(p_rwkv, batch, rwkv_mu[e], rwkv_w0[e], rwkv_w2[e], rwkv_a0[e], rwkv_a2[e], rwkv_g2[e],
                          rwkv_kk[e], rwkv_ka[e], rwkv_rk[e], rwkv_lnw[e], rwkv_lnb[e])
            w_out = (ev_wo, e)
        else:
            o = l // 2
            weights, widths = _odd_in_weights(od_w, o, gdn_conv_w.shape[2], gdn_norm_g.shape[1] * gdn_a_log.shape[1],
                                              mlstm_conv_w.shape[2], mlstm_norm_g.shape[1] * mlstm_ig_b.shape[1],
                                              gdn_a_log.shape[1], mlstm_ig_b.shape[1])
            p_qkv, p_z, p_mqk, p_mv, p_mog, p_gates = norm_proj(xs, norm_mix_g[l], weights, widths)
            ya = gdn_mix(p_qkv, p_z, p_gates, batch, gdn_conv_w[o], gdn_a_log[o], gdn_dt_bias[o], gdn_norm_g[o])
            yb = mlstm_mix(p_mqk, p_mv, p_mog, p_gates, batch, mlstm_conv_w[o], mlstm_ig_b[o], mlstm_fg_b[o],
                           mlstm_norm_g[o])
            w_out = (od_wo, o)
        xs = out_mlp(ya, yb, xs, w_out, norm_mlp_g[l], (up_w, l), (down_w, l), final_g, l == depth - 1)
    return xs.reshape(batch, seq, d)


def _odd_in_weights(w_all, layer, gdn_qkv, gdn_w, mlstm_qk, mlstm_w, gdn_heads, mlstm_heads):
    w = w_all[layer]
    lead = gdn_qkv + gdn_w
    m0 = lead + 2 * gdn_heads
    m1 = m0 + mlstm_qk + 2 * mlstm_w
    ngate = 2 * gdn_heads + 2 * mlstm_heads
    gates = jnp.concatenate([w[:, lead:m0], w[:, m1:m1 + 2 * mlstm_heads],
                             jnp.zeros((w.shape[0], GATE_LANES - ngate), w.dtype)], axis=1)
    mlstm = w[:, m0:m1]
    weights = [(w_all, _layer_spec(w_all, layer, lead)), (mlstm, _const_spec(mlstm.shape)),
               (gates, _const_spec(gates.shape))]
    return weights, [(gdn_qkv, gdn_w), (mlstm_qk, mlstm_w, mlstm_w), (GATE_LANES,)]
```

```python
import functools
import math

import jax
import jax.numpy as jnp
from jax import lax
from jax.experimental import pallas as pl
from jax.experimental.pallas import tpu as pltpu

f32 = jnp.float32
bf16 = jnp.bfloat16

LANES = 128
SUBLANES = 8
VMEM_LIMIT = 56 * 1024 * 1024

NORM_EPS = 1e-6
CONV_W = 4
LRU_C = 8.0
RWKV_HEAD = 64
RWKV_LN_EPS = 64e-5
CHUNK = 64

ROW_BLOCK = 512
SEQ_BLOCK = 128


def _cparams(sem):
    return pltpu.CompilerParams(dimension_semantics=sem, vmem_limit_bytes=VMEM_LIMIT)


def _const_spec(shape):
    nd = len(shape)
    return pl.BlockSpec(shape, lambda *_: (0,) * nd)


def _layer_spec(w, layer, cols=None):
    return pl.BlockSpec((None, w.shape[1], w.shape[2] if cols is None else cols), lambda *_: (layer, 0, 0))


def _rms(x, g):
    return x * lax.rsqrt(jnp.mean(x * x, axis=-1, keepdims=True) + NORM_EPS) * g


def _softplus(x):
    return jnp.maximum(x, 0.0) + jnp.log(1.0 + jnp.exp(-jnp.abs(x)))


def _sigmoid(x):
    return 0.5 * jnp.tanh(0.5 * x) + 0.5


def _norm_proj_body(widths, x_ref, g_ref, *refs):
    w_refs, o_refs = refs[:len(widths)], list(refs[len(widths):])
    h = _rms(x_ref[...], g_ref[...]).astype(bf16)
    for w_ref, group in zip(w_refs, widths):
        c0 = 0
        for wd in group:
            o_ref = o_refs.pop(0)
            w = w_ref[:, c0:c0 + wd].astype(bf16)
            o_ref[...] = jnp.dot(h, w, preferred_element_type=f32).astype(o_ref.dtype)
            c0 += wd


def norm_proj(x, g, weights, widths):
    n, d = x.shape
    tm = min(ROW_BLOCK, n)
    flat = [wd for group in widths for wd in group]
    return pl.pallas_call(
        functools.partial(_norm_proj_body, widths),
        grid=(n // tm,),
        in_specs=[pl.BlockSpec((tm, d), lambda i: (i, 0)), _const_spec((1, d))] + [spec for _, spec in weights],
        out_specs=[pl.BlockSpec((tm, wd), lambda i: (i, 0)) for wd in flat],
        out_shape=[jax.ShapeDtypeStruct((n, wd), f32) for wd in flat],
        compiler_params=_cparams(("parallel",)),
        name="norm_proj",
    )(x, g.reshape(1, d), *[w for w, _ in weights])


FF_BLOCK = 1024


def _out_mlp_body(final, ya_ref, yb_ref, x_ref, wo_ref, g_ref, wu_ref, wd_ref, gf_ref, o_ref):
    half = ya_ref.shape[1]
    y = jnp.dot(ya_ref[...], wo_ref[0:half, :].astype(bf16), preferred_element_type=f32)
    y = y + jnp.dot(yb_ref[...], wo_ref[half:, :].astype(bf16), preferred_element_type=f32)
    x1 = x_ref[...] + y
    h = _rms(x1, g_ref[...]).astype(bf16)
    acc = x1
    for c in range(wu_ref.shape[1] // FF_BLOCK):
        u = jnp.dot(h, wu_ref[:, c * FF_BLOCK:(c + 1) * FF_BLOCK].astype(bf16), preferred_element_type=f32)
        u = jnp.square(jnp.maximum(u, 0.0)).astype(bf16)
        acc = acc + jnp.dot(u, wd_ref[c * FF_BLOCK:(c + 1) * FF_BLOCK, :].astype(bf16), preferred_element_type=f32)
    if final:
        acc = _rms(acc, gf_ref[...])
    o_ref[...] = acc


def out_mlp(ya, yb, x, wo, g, wu, wd, gf, final):
    n, d = x.shape
    half = ya.shape[1]
    tm = min(ROW_BLOCK, n)
    row = lambda i: (i, 0)
    return pl.pallas_call(
        functools.partial(_out_mlp_body, final),
        grid=(n // tm,),
        in_specs=[pl.BlockSpec((tm, half), row), pl.BlockSpec((tm, half), row), pl.BlockSpec((tm, d), row),
                  _layer_spec(*wo), _const_spec((1, d)), _layer_spec(*wu), _layer_spec(*wd), _const_spec((1, d))],
        out_specs=pl.BlockSpec((tm, d), row),
        out_shape=jax.ShapeDtypeStruct((n, d), f32),
        compiler_params=_cparams(("parallel",)),
        name="out_mlp",
    )(ya, yb, x, wo[0], g.reshape(1, d), wu[0], wd[0], gf.reshape(1, d))


def _seq_specs(batch, tb, widths):
    return [pl.BlockSpec((batch, tb, wd), lambda i: (0, i, 0)) for wd in widths]


def _load_rows(ref):
    return jnp.concatenate([ref[b] for b in range(ref.shape[0])], axis=0)


def _store_rows(ref, x):
    tb = ref.shape[1]
    for b in range(ref.shape[0]):
        ref[b] = x[b * tb:(b + 1) * tb].astype(ref.dtype)


def _per_batch(fn, x, prev_ref):
    nb = prev_ref.shape[0]
    tb = x.shape[0] // nb
    out = []
    for b in range(nb):
        xb = x[b * tb:(b + 1) * tb]
        out.append(fn(xb, prev_ref[b]))
        prev_ref[b] = xb[tb - SUBLANES:]
    return jnp.concatenate(out, axis=0)


def _shift_rows(x, prev, k):
    rolled = pltpu.roll(x, k, axis=0)
    head = pltpu.roll(prev, k, axis=0)
    rows = lax.broadcasted_iota(jnp.int32, (SUBLANES, x.shape[1]), 0)
    top = jnp.where(rows < k, head, rolled[0:SUBLANES])
    return jnp.concatenate([top, rolled[SUBLANES:]], axis=0)


def _causal_conv(x, prev, w):
    y = x * w[CONV_W - 1:CONV_W]
    for k in range(1, CONV_W):
        y = y + _shift_rows(x, prev, k) * w[CONV_W - 1 - k:CONV_W - k]
    return y


def _scan_shift(x, s, fill):
    n, c = x.shape
    if s % SUBLANES == 0:
        return jnp.concatenate([jnp.full((s, c), fill, x.dtype), x[:n - s]], axis=0)
    rolled = pltpu.roll(x, s, axis=0)
    rows = lax.broadcasted_iota(jnp.int32, (SUBLANES, c), 0)
    top = jnp.where(rows < s, fill, rolled[0:SUBLANES])
    return jnp.concatenate([top, rolled[SUBLANES:]], axis=0)


def _linear_scan(a, u):
    s = 1
    while s < a.shape[0]:
        u = a * _scan_shift(u, s, 0.0) + u
        a = a * _scan_shift(a, s, 1.0)
        s *= 2
    return a, u


def _lru_body(p_ref, cw_ref, cb_ref, wg_ref, bg_ref, lam_ref, o_ref, xprev_ref, h_ref):
    nb, tb = p_ref.shape[0], p_ref.shape[1]
    w = o_ref.shape[2]
    step = pl.program_id(0)

    @pl.when(step == 0)
    def _():
        xprev_ref[...] = jnp.zeros_like(xprev_ref)
        h_ref[...] = jnp.zeros_like(h_ref)

    gate_in = jnp.concatenate([p_ref[b, :, 0:w] for b in range(nb)], axis=0)
    rec = jnp.concatenate([p_ref[b, :, w:] for b in range(nb)], axis=0)
    xc = _per_batch(lambda x, prev: _causal_conv(x, prev, cw_ref[...]), rec, xprev_ref) + cb_ref[...]
    gates = jnp.dot(xc.astype(bf16), wg_ref[...], preferred_element_type=f32) + bg_ref[...]
    r_gate = _sigmoid(gates[:, 0:w])
    i_gate = _sigmoid(gates[:, w:])
    log_a = (-LRU_C) * r_gate * _softplus(-lam_ref[...])
    a = jnp.exp(log_a)
    mult = jnp.sqrt(1.0 - jnp.exp(2.0 * log_a))
    rows = lax.broadcasted_iota(jnp.int32, (nb * tb, w), 0)
    mult = jnp.where(jnp.logical_and(rows % tb == 0, step == 0), 1.0, mult)
    u = i_gate * xc * mult
    hs = []
    for b in range(nb):
        a_cum, h = _linear_scan(a[b * tb:(b + 1) * tb], u[b * tb:(b + 1) * tb])
        h = h + a_cum * h_ref[b * SUBLANES:b * SUBLANES + 1, :]
        h_ref[b * SUBLANES:b * SUBLANES + 1, :] = h[tb - 1:tb]
        hs.append(h)
    _store_rows(o_ref, jnp.concatenate(hs, axis=0) * jax.nn.gelu(gate_in, approximate=True))


def _block_diag(wb):
    nb, d, e = wb.shape
    eye = jnp.eye(nb, dtype=wb.dtype)
    return (wb[:, :, None, :] * eye[:, None, :, None]).reshape(nb * d, nb * e)


def lru_mix(p, batch, conv_w, conv_b, wa, ba, wx, bx, lam):
    n, c2 = p.shape
    w = c2 // 2
    t = n // batch
    tb = min(SEQ_BLOCK, t)
    wg = jnp.concatenate([_block_diag(wa), _block_diag(wx)], axis=1).astype(bf16)
    bg = jnp.concatenate([ba, bx]).reshape(1, 2 * w)
    out = pl.pallas_call(
        _lru_body,
        grid=(t // tb,),
        in_specs=_seq_specs(batch, tb, [c2]) + [_const_spec((CONV_W, w)), _const_spec((1, w)),
                                                _const_spec((w, 2 * w)), _const_spec((1, 2 * w)), _const_spec((1, w))],
        out_specs=_seq_specs(batch, tb, [w])[0],
        out_shape=jax.ShapeDtypeStruct((batch, t, w), bf16),
        scratch_shapes=[pltpu.VMEM((batch, SUBLANES, w), f32), pltpu.VMEM((batch * SUBLANES, w), f32)],
        compiler_params=_cparams(("arbitrary",)),
        name="lru_mix",
    )(p.reshape(batch, t, c2), conv_w, conv_b.reshape(1, w), wg, bg, lam.reshape(1, w))
    return out.reshape(n, w)


UNIT_GROUP = 32


def _dot(a, b):
    return jnp.dot(a.astype(bf16), b.astype(bf16), preferred_element_type=f32)


def _dot_nt(a, b):
    return lax.dot_general(a.astype(bf16), b.astype(bf16), (((1,), (1,)), ((), ())), preferred_element_type=f32)


def _dot_tn(a, b):
    return lax.dot_general(a.astype(bf16), b.astype(bf16), (((0,), (0,)), ((), ())), preferred_element_type=f32)


def _dot_split(x, e, terms=2):
    out = None
    for _ in range(terms):
        hi = x.astype(bf16)
        x = x - hi.astype(f32)
        part = jnp.dot(hi, e, preferred_element_type=f32)
        out = part if out is None else out + part
    return out


def _iota(shape, axis):
    return lax.broadcasted_iota(jnp.int32, shape, axis)


def _tri_incl():
    return (_iota((CHUNK, CHUNK), 0) >= _iota((CHUNK, CHUNK), 1)).astype(bf16)


def _chunk_cumsum(x):
    tri = _tri_incl()
    cs, ce = [], []
    for c in range(x.shape[0] // CHUNK):
        rest = x[c * CHUNK:(c + 1) * CHUNK]
        cs_c = None
        for _ in range(3):
            hi = rest.astype(bf16)
            rest = rest - hi.astype(f32)
            part = jnp.dot(tri, hi, preferred_element_type=f32)
            cs_c = part if cs_c is None else cs_c + part
        cs.append(cs_c)
        ce.append(jnp.broadcast_to(cs_c[CHUNK - 1:CHUNK], cs_c.shape))
    return jnp.concatenate(cs, axis=0), jnp.concatenate(ce, axis=0)


def _head_pair_ones(head):
    return (_iota((LANES, LANES), 0) // head == _iota((LANES, LANES), 1) // head).astype(bf16)


def _segsum(x, e):
    return jnp.concatenate([_dot_split(x[:, j:j + LANES], e) for j in range(0, x.shape[1], LANES)], axis=1)


def _stack_heads(x):
    lo = _iota(x.shape, 1) < RWKV_HEAD
    return jnp.concatenate([jnp.where(lo, x, 0.0), jnp.where(lo, 0.0, x)], axis=0)


def _unit_lower_inverse(amats):
    shape = amats[0].shape
    ii = _iota(shape, 0)
    jj = _iota(shape, 1) % CHUNK
    eye = (ii == jj).astype(f32)
    first = jnp.logical_and(ii // 2 == jj // 2, ii > jj)
    ts = [eye - jnp.where(first, a, 0.0) for a in amats]
    abf = [a.astype(bf16) for a in amats]
    b = 2
    while b < CHUNK:
        off = jnp.logical_and(ii // (2 * b) == jj // (2 * b), ii // b > jj // b)
        xs = [_dot(a, _stack_heads(t)) for a, t in zip(abf, ts)]
        ts = [t - jnp.where(off, _dot(t, _stack_heads(x)), 0.0) for t, x in zip(ts, xs)]
        b *= 2
    return ts


def _rwkv_body(p_ref, mu_ref, w0_ref, w2_ref, a0_ref, a2_ref, g2_ref, kk_ref, ka_ref, rk_ref, lnw_ref, lnb_ref,
               o_ref, prev_ref, s_ref, at_s, rt_s, bt_s, kt_s, gc_s, v_s, y_s, wh_s, uh_s, mrb_s):
    nb, tb = p_ref.shape[0], p_ref.shape[1]
    n = nb * tb
    w = o_ref.shape[2]
    npair = w // LANES

    @pl.when(pl.program_id(0) == 0)
    def _():
        prev_ref[...] = jnp.zeros_like(prev_ref)
        s_ref[...] = jnp.zeros_like(s_ref)

    p = _load_rows(p_ref)
    pf = p + mu_ref[...] * (_per_batch(lambda x, prev: _shift_rows(x, prev, 1), p, prev_ref) - p)
    r, k, v = pf[:, 0:w], pf[:, w:2 * w], pf[:, 2 * w:3 * w]
    lowrank = pf[:, 3 * w:3 * w + LANES]
    xg = pf[:, 3 * w + LANES:]
    lw = (-math.exp(-0.5)) * _sigmoid(w0_ref[...] + _dot(jnp.tanh(lowrank), w2_ref[...]))
    a = _sigmoid(a0_ref[...] + _dot(lowrank, a2_ref[...]))
    g = _dot(_sigmoid(xg), g2_ref[...])
    ones = _head_pair_ones(RWKV_HEAD)
    kk = k * kk_ref[...]
    kk = kk * lax.rsqrt(_segsum(kk * kk, ones) + 1e-6)
    k = k * (1.0 + (a - 1.0) * ka_ref[...])
    cs, ce = _chunk_cumsum(lw)
    inv_g = jnp.exp(-cs)
    at_s[...] = kk * jnp.exp(cs - lw)
    rt_s[...] = r * jnp.exp(cs)
    bt_s[...] = -(kk * a) * inv_g
    kt_s[...] = k * inv_g
    gc_s[...] = jnp.exp(ce)
    v_s[...] = v

    tt = _iota((CHUNK, LANES), 0)
    ss = _iota((CHUNK, LANES), 1) % CHUNK
    strict = tt > ss
    incl = tt >= ss
    diag_blocks = (_iota((LANES, LANES), 0) < RWKV_HEAD) == (_iota((LANES, LANES), 1) < RWKV_HEAD)

    pair_cols = [slice(j * LANES, (j + 1) * LANES) for j in range(npair)]
    all_units = [(slice(c * CHUNK, (c + 1) * CHUNK), cols) for c in range(n // CHUNK) for cols in pair_cols]
    for u0 in range(0, len(all_units), UNIT_GROUP):
        units = all_units[u0:u0 + UNIT_GROUP]
        gms = [_dot_nt(jnp.concatenate([at_s[u], rt_s[u]], axis=0),
                       jnp.concatenate([_stack_heads(bt_s[u]), _stack_heads(kt_s[u])], axis=0)) for u in units]
        for u, gm in zip(units, gms):
            mrb_s[u] = jnp.where(incl, gm[CHUNK:, 0:LANES], 0.0)
        vsts = [_stack_heads(v_s[u]) for u in units]
        akvs = [_dot(jnp.where(strict, gm[0:CHUNK, LANES:], 0.0), vst) for gm, vst in zip(gms, vsts)]
        yhs = [_dot(jnp.where(incl, gm[CHUNK:, LANES:], 0.0), vst) for gm, vst in zip(gms, vsts)]
        for u, yh in zip(units, yhs):
            y_s[u] = yh
        tinvs = _unit_lower_inverse([jnp.where(strict, -gm[0:CHUNK, 0:LANES], 0.0) for gm in gms])
        wus = [_dot(t, jnp.concatenate([_stack_heads(at_s[u]), _stack_heads(akv)], axis=1))
               for u, t, akv in zip(units, tinvs, akvs)]
        for u, wu in zip(units, wus):
            wh_s[u] = wu[:, 0:LANES]
            uh_s[u] = wu[:, LANES:]

    def chunk_step(c, carry):
        starts = [pl.multiple_of(c * CHUNK + b * tb, CHUNK) for b in range(nb)]
        chains = [(b * npair + j, pl.ds(starts[b], CHUNK), cols) for b in range(nb) for j, cols in enumerate(pair_cols)]
        ss = [s_ref[si] for si, _, _ in chains]
        xs = [_dot_nt(jnp.concatenate([wh_s[rows, cols], rt_s[rows, cols]], axis=0), s)
              for (_, rows, cols), s in zip(chains, ss)]
        us = [x[0:CHUNK] + uh_s[rows, cols] for (_, rows, cols), x in zip(chains, xs)]
        upds = [_dot_tn(jnp.concatenate([u, v_s[rows, cols]], axis=0),
                        jnp.concatenate([bt_s[rows, cols], kt_s[rows, cols]], axis=0))
                for (_, rows, cols), u in zip(chains, us)]
        ys = [_dot(mrb_s[rows, cols], _stack_heads(u)) for (_, rows, cols), u in zip(chains, us)]
        for (si, rows, cols), s, x, upd, y in zip(chains, ss, xs, upds, ys):
            s_ref[si] = (s + jnp.where(diag_blocks, upd, 0.0)) * gc_s[pl.ds(starts[si // npair], 1), cols]
            y_s[rows, cols] = y_s[rows, cols] + x[CHUNK:] + y
        return carry

    lax.fori_loop(0, tb // CHUNK, chunk_step, 0)

    y = y_s[...]
    inv_head = 1.0 / RWKV_HEAD
    mean = _segsum(y, ones) * inv_head
    yc = y - mean
    var = _segsum(yc * yc, ones) * inv_head
    yn = yc * lax.rsqrt(var + RWKV_LN_EPS) * lnw_ref[...] + lnb_ref[...]
    bonus = _segsum(r * k * rk_ref[...], ones) * v
    _store_rows(o_ref, (yn + bonus) * g)


def rwkv_mix(p, batch, mu, w0, w2, a0, a2, g2, k_k, k_a, r_k, ln_w, ln_b):
    n, cols = p.shape
    w = w0.shape[0]
    t = n // batch
    tb = min(SEQ_BLOCK, t)
    rank = w2.shape[0]
    zeros = jnp.zeros((LANES - rank, w), f32)
    w2p = jnp.concatenate([w2, zeros], axis=0).astype(bf16)
    a2p = jnp.concatenate([zeros, a2], axis=0).astype(bf16)
    row = lambda v: v.reshape(1, -1)
    consts = [row(mu), row(w0), w2p, row(a0), a2p, g2.astype(bf16), row(k_k), row(k_a), row(r_k), row(ln_w),
              row(ln_b)]
    blk = lambda: pltpu.VMEM((batch * tb, w), f32)
    out = pl.pallas_call(
        _rwkv_body,
        grid=(t // tb,),
        in_specs=_seq_specs(batch, tb, [cols]) + [_const_spec(c.shape) for c in consts],
        out_specs=_seq_specs(batch, tb, [w])[0],
        out_shape=jax.ShapeDtypeStruct((batch, t, w), bf16),
        scratch_shapes=[pltpu.VMEM((batch, SUBLANES, cols), f32),
                        pltpu.VMEM((batch * w // LANES, LANES, LANES), f32)] + [blk() for _ in range(10)],
        compiler_params=_cparams(("arbitrary",)),
        name="rwkv_mix",
    )(p.reshape(batch, t, cols), *consts)
    return out.reshape(n, w)


GATE_LANES = LANES


def _silu(x):
    return x * _sigmoid(x)


def _expand_gate(gates, first, nheads, terms=2):
    sel = (_iota((GATE_LANES, nheads * LANES), 0) - first == _iota((GATE_LANES, nheads * LANES), 1) // LANES)
    return _dot_split(gates, sel.astype(bf16), terms)


def _gate_row(v, first):
    return jnp.zeros((1, GATE_LANES), f32).at[0, first:first + v.shape[0]].set(v)


def _pair_cols(x_ref, rows, pair):
    c = 2 * pair * LANES
    return x_ref[rows, c:c + LANES], x_ref[rows, c + LANES:c + 2 * LANES]


def _pair_select(x0, x1):
    return jnp.where(_iota(x0.shape, 1) < CHUNK, x0, x1)


def _pair_row_form(x0, x1):
    return jnp.transpose(jnp.concatenate([x0, x1], axis=0))[0:CHUNK]


def _pair_block_rhs(x0, x1):
    return jnp.concatenate([jnp.concatenate([x0, jnp.zeros_like(x0)], axis=1),
                            jnp.concatenate([jnp.zeros_like(x1), x1], axis=1)], axis=0)


def _head_rms(o, g, ones):
    return o * lax.rsqrt(_segsum(o * o, ones) * (1.0 / LANES) + NORM_EPS) * g


def _gdn_body(p_ref, z_ref, gate_ref, cw_ref, alog_ref, dtb_ref, ng_ref, o_ref,
              prev_ref, s_ref, q_s, k_s, kb_s, vb_s, qd_s, kbg_s, kd_s, gc_s, gl_s, o_s, u_s, w_s, qk_s):
    nb, tb = p_ref.shape[0], p_ref.shape[1]
    n = nb * tb
    w = o_ref.shape[2]
    nheads = w // LANES

    @pl.when(pl.program_id(0) == 0)
    def _():
        prev_ref[...] = jnp.zeros_like(prev_ref)
        s_ref[...] = jnp.zeros_like(s_ref)

    qkv = _silu(_per_batch(lambda x, prev: _causal_conv(x, prev, cw_ref[...]), _load_rows(p_ref), prev_ref))
    ones = _head_pair_ones(LANES)
    q, k, v = qkv[:, 0:w], qkv[:, w:2 * w], qkv[:, 2 * w:]
    q = q * lax.rsqrt(_segsum(q * q, ones) + 1e-6) * (LANES ** -0.5)
    k = k * lax.rsqrt(_segsum(k * k, ones) + 1e-6)
    gates = _load_rows(gate_ref)
    beta = _expand_gate(_sigmoid(gates), 0, nheads)
    gc_c, ge_c = _chunk_cumsum(-jnp.exp(alog_ref[...]) * _softplus(gates + dtb_ref[...]))
    eg = _expand_gate(jnp.exp(gc_c), nheads, nheads)
    kb = k * beta
    q_s[...] = q
    k_s[...] = k
    kb_s[...] = kb
    vb_s[...] = v * beta
    qd_s[...] = q * eg
    kbg_s[...] = kb * eg
    kd_s[...] = k * _expand_gate(jnp.exp(ge_c - gc_c), nheads, nheads)
    gc_s[...] = _expand_gate(gc_c, nheads, nheads, terms=3)
    gl_s[...] = _expand_gate(jnp.exp(ge_c), nheads, nheads)

    tt = _iota((CHUNK, LANES), 0)
    ss = _iota((CHUNK, LANES), 1) % CHUNK
    strict = tt > ss
    incl = tt >= ss

    all_units = [(slice(c * CHUNK, (c + 1) * CHUNK), pair) for c in range(n // CHUNK) for pair in range(nheads // 2)]
    for u0 in range(0, len(all_units), UNIT_GROUP):
        units = all_units[u0:u0 + UNIT_GROUP]
        xs = [_dot_nt(jnp.concatenate(_pair_cols(kb_s, rows, pair) + _pair_cols(q_s, rows, pair), axis=0),
                      jnp.concatenate(_pair_cols(k_s, rows, pair), axis=0)) for rows, pair in units]
        amats = []
        for (rows, pair), x in zip(units, xs):
            g0, g1 = _pair_cols(gc_s, rows, pair)
            dec = jnp.exp(jnp.minimum(_pair_select(g0, g1) - _pair_row_form(g0, g1), 0.0))
            amats.append(jnp.where(strict, _pair_select(x[0:CHUNK], x[CHUNK:2 * CHUNK]) * dec, 0.0))
            qk_s[rows, pair * LANES:(pair + 1) * LANES] = jnp.where(
                incl, _pair_select(x[2 * CHUNK:3 * CHUNK], x[3 * CHUNK:]) * dec, 0.0)
        tinvs = _unit_lower_inverse(amats)
        uws = []
        for (rows, pair), tinv in zip(units, tinvs):
            vb0, vb1 = _pair_cols(vb_s, rows, pair)
            kg0, kg1 = _pair_cols(kbg_s, rows, pair)
            uws.append(_dot(tinv, _pair_block_rhs(jnp.concatenate([vb0, kg0], axis=1),
                                                  jnp.concatenate([vb1, kg1], axis=1))))
        for (rows, pair), uw in zip(units, uws):
            for h in range(2):
                cols = slice((2 * pair + h) * LANES, (2 * pair + h + 1) * LANES)
                u_s[rows, cols] = uw[:, 2 * h * LANES:(2 * h + 1) * LANES]
                w_s[rows, cols] = uw[:, (2 * h + 1) * LANES:(2 * h + 2) * LANES]

    head_cols = [slice(h * LANES, (h + 1) * LANES) for h in range(nheads)]

    def chunk_step(c, carry):
        starts = [pl.multiple_of(c * CHUNK + b * tb, CHUNK) for b in range(nb)]
        chains = [(b, h, pl.ds(starts[b], CHUNK), cols) for b in range(nb) for h, cols in enumerate(head_cols)]
        ss = [s_ref[b * nheads + h] for b, h, _, _ in chains]
        x2s = [_dot(jnp.concatenate([w_s[rows, cols], qd_s[rows, cols]], axis=0), s)
               for (_, _, rows, cols), s in zip(chains, ss)]
        vns = [u_s[rows, cols] - x2[0:CHUNK] for (_, _, rows, cols), x2 in zip(chains, x2s)]
        upds = [_dot_tn(kd_s[rows, cols], vn) for (_, _, rows, cols), vn in zip(chains, vns)]
        os = [_dot(qk_s[pl.ds(starts[b], CHUNK), pair * LANES:(pair + 1) * LANES],
                   _pair_block_rhs(vns[b * nheads + 2 * pair], vns[b * nheads + 2 * pair + 1]))
              for b in range(nb) for pair in range(nheads // 2)]
        for ci, (b, h, rows, cols) in enumerate(chains):
            s_ref[ci] = ss[ci] * gl_s[pl.ds(starts[b], 1), cols] + upds[ci]
            o_s[rows, cols] = x2s[ci][CHUNK:] + os[ci // 2][:, (h % 2) * LANES:(h % 2 + 1) * LANES]
        return carry

    lax.fori_loop(0, tb // CHUNK, chunk_step, 0)
    _store_rows(o_ref, _head_rms(o_s[...], ng_ref[...], ones) * _silu(_load_rows(z_ref)))


def gdn_mix(p_qkv, p_z, p_gates, batch, conv_w, a_log, dt_bias, norm_g):
    n, cols = p_qkv.shape
    w = p_z.shape[1]
    nheads = w // LANES
    t = n // batch
    tb = min(SEQ_BLOCK, t)
    consts = [conv_w, _gate_row(a_log, nheads), _gate_row(dt_bias, nheads), jnp.tile(norm_g, nheads).reshape(1, w)]
    blk = lambda: pltpu.VMEM((batch * tb, w), f32)
    out = pl.pallas_call(
        _gdn_body,
        grid=(t // tb,),
        in_specs=_seq_specs(batch, tb, [cols, w, GATE_LANES]) + [_const_spec(c.shape) for c in consts],
        out_specs=_seq_specs(batch, tb, [w])[0],
        out_shape=jax.ShapeDtypeStruct((batch, t, w), bf16),
        scratch_shapes=[pltpu.VMEM((batch, SUBLANES, cols), f32), pltpu.VMEM((batch * nheads, LANES, LANES), f32)]
                       + [blk() for _ in range(12)] + [pltpu.VMEM((batch * tb, w // 2), f32)],
        compiler_params=_cparams(("arbitrary",)),
        name="gdn_mix",
    )(p_qkv.reshape(batch, t, cols), p_z.reshape(batch, t, w), p_gates.reshape(batch, t, GATE_LANES), *consts)
    return out.reshape(n, w)


def _cummax_rows(x):
    out = []
    for c in range(x.shape[0] // CHUNK):
        y = x[c * CHUNK:(c + 1) * CHUNK]
        s = 1
        while s < CHUNK:
            y = jnp.maximum(y, _scan_shift(y, s, -jnp.inf))
            s *= 2
        out.append(y)
    return jnp.concatenate(out, axis=0)


def _mlstm_body(pqk_ref, v_ref, og_ref, gate_ref, cw_ref, igb_ref, fgb_ref, ng_ref, o_ref,
                prev_ref, s_ref, m_ref, q_s, k_s, cc_s, mx_s, wi_s, ws_s, ed_s, h_s, v_s, dp_s, intra_s, upd_s):
    nb, tb = pqk_ref.shape[0], pqk_ref.shape[1]
    n = nb * tb
    w = o_ref.shape[2]
    nheads = w // LANES
    hq = pqk_ref.shape[2] // 2

    @pl.when(pl.program_id(0) == 0)
    def _():
        prev_ref[...] = jnp.zeros_like(prev_ref)
        s_ref[...] = jnp.zeros_like(s_ref)
        m_ref[...] = jnp.zeros_like(m_ref)

    qk = _silu(_per_batch(lambda x, prev: _causal_conv(x, prev, cw_ref[...]), _load_rows(pqk_ref), prev_ref))
    q_s[...] = qk[:, 0:hq] * ((hq // nheads) ** -0.5)
    k_s[...] = qk[:, hq:]
    v_s[...] = _load_rows(v_ref)
    gates = _load_rows(gate_ref)
    first = 2 * nheads
    i_pre = gates + igb_ref[...]
    f_pre = pltpu.roll(gates, GATE_LANES - nheads, axis=1) + fgb_ref[...]
    b, b_end = _chunk_cumsum(-_softplus(-f_pre))
    cc = i_pre - b
    cm = _cummax_rows(cc)
    te = b_end + cc
    nc = n // CHUNK
    ncb = tb // CHUNK
    mx, w_inter, wk_scale, dprev = [], [], [], []
    for c in range(nc):
        bi = c // ncb
        m = m_ref[bi * SUBLANES:bi * SUBLANES + 1, :] if c % ncb == 0 else m
        rows = slice(c * CHUNK, (c + 1) * CHUNK)
        b_last = b_end[c * CHUNK:c * CHUNK + 1]
        m_new = jnp.maximum(b_last + m, jnp.max(te[rows], axis=0, keepdims=True))
        mx_c = jnp.maximum(m, cm[rows])
        mx.append(mx_c)
        w_inter.append(jnp.exp(m - mx_c))
        wk_scale.append(jnp.exp(te[rows] - m_new))
        dprev.append(jnp.broadcast_to(jnp.exp(b_last + m - m_new), (SUBLANES, GATE_LANES)))
        m = m_new
        if c % ncb == ncb - 1:
            m_ref[bi * SUBLANES:bi * SUBLANES + 1, :] = m
    mx = jnp.concatenate(mx, axis=0)
    cc_s[...] = _expand_gate(cc, first, nheads, terms=3)
    mx_s[...] = _expand_gate(mx, first, nheads, terms=3)
    wi_s[...] = _expand_gate(jnp.concatenate(w_inter, axis=0), first, nheads)
    ws_s[...] = _expand_gate(jnp.concatenate(wk_scale, axis=0), first, nheads)
    ed_s[...] = jnp.exp(-_expand_gate(b + mx, first, nheads, terms=3))
    dp_s[...] = _expand_gate(jnp.concatenate(dprev, axis=0), first, nheads)

    incl = _iota((CHUNK, LANES), 0) >= _iota((CHUNK, LANES), 1) % CHUNK
    rows128 = _iota((LANES, 4 * LANES), 0) < CHUNK
    lanes512 = _iota((LANES, 4 * LANES), 1) < 2 * LANES
    diag_blocks = rows128 == lanes512
    ones_v = jnp.ones((CHUNK, LANES), f32)
    npair = nheads // 2
    pair_cols = [slice(pair * LANES, (pair + 1) * LANES) for pair in range(npair)]

    def vext(rows, pair):
        v0, v1 = _pair_cols(v_s, rows, pair)
        return jnp.concatenate([v0, ones_v, v1, ones_v], axis=1)

    units = [(c, slice(c * CHUNK, (c + 1) * CHUNK), pair) for c in range(nc) for pair in range(npair)]
    qks = [_dot_nt(q_s[rows, pair_cols[pair]], _stack_heads(k_s[rows, pair_cols[pair]])) for _, rows, pair in units]
    wms = []
    for (_, rows, pair), qk_u in zip(units, qks):
        expo = _pair_row_form(*_pair_cols(cc_s, rows, pair)) - _pair_select(*_pair_cols(mx_s, rows, pair))
        wms.append(jnp.where(incl, jnp.exp(jnp.minimum(expo, 0.0)) * qk_u, 0.0))
    for (_, rows, pair), wm in zip(units, wms):
        ve = vext(rows, pair)
        intra_s[rows, 4 * pair * LANES:4 * (pair + 1) * LANES] = _dot(
            wm, jnp.where(diag_blocks, jnp.concatenate([ve, ve], axis=0), 0.0))
    for c, rows, pair in units:
        wk = k_s[rows, pair_cols[pair]] * _pair_select(*_pair_cols(ws_s, rows, pair))
        upd_s[c, pair] = jnp.where(diag_blocks, _dot_tn(wk, vext(rows, pair)), 0.0)

    def chunk_step(c, carry):
        chains = [(b, pair) for b in range(nb) for pair in range(npair)]
        ss = [s_ref[b * npair + pair] for b, pair in chains]
        inters = [_dot(q_s[pl.ds(pl.multiple_of(c * CHUNK + b * tb, CHUNK), CHUNK), pair_cols[pair]], s)
                  for (b, pair), s in zip(chains, ss)]
        for (b, pair), s, inter in zip(chains, ss, inters):
            g = c + b * ncb
            rows = pl.ds(pl.multiple_of(c * CHUNK + b * tb, CHUNK), CHUNK)
            dp = _pair_cols(dp_s, pl.ds(pl.multiple_of(g * SUBLANES, SUBLANES), 1), pair)
            s_ref[b * npair + pair] = s * jnp.concatenate([dp[0], dp[0], dp[1], dp[1]], axis=1) + upd_s[g, pair]
            for h in range(2):
                cols = slice((2 * pair + h) * LANES, (2 * pair + h + 1) * LANES)
                c0 = 2 * h * LANES
                i0 = 4 * pair * LANES + c0
                wi = wi_s[rows, cols]
                num = wi * inter[:, c0:c0 + LANES] + intra_s[rows, i0:i0 + LANES]
                den = wi * inter[:, c0 + LANES:c0 + 2 * LANES] + intra_s[rows, i0 + LANES:i0 + 2 * LANES]
                h_s[rows, cols] = num / jnp.maximum(jnp.abs(den), ed_s[rows, cols])
        return carry

    lax.fori_loop(0, ncb, chunk_step, 0)
    ones = _head_pair_ones(LANES)
    _store_rows(o_ref, _head_rms(h_s[...], ng_ref[...], ones) * _sigmoid(_load_rows(og_ref)))


def mlstm_mix(p_qk, p_v, p_og, p_gates, batch, conv_w, ig_b, fg_b, norm_g):
    n, cols = p_qk.shape
    w = p_v.shape[1]
    nheads = w // LANES
    t = n // batch
    tb = min(SEQ_BLOCK, t)
    rows = batch * tb
    consts = [conv_w, _gate_row(ig_b, 2 * nheads), _gate_row(fg_b, 2 * nheads),
              jnp.tile(norm_g, nheads).reshape(1, w)]
    blk = lambda: pltpu.VMEM((rows, w), f32)
    out = pl.pallas_call(
        _mlstm_body,
        grid=(t // tb,),
        in_specs=_seq_specs(batch, tb, [cols, w, w, GATE_LANES]) + [_const_spec(c.shape) for c in consts],
        out_specs=_seq_specs(batch, tb, [w])[0],
        out_shape=jax.ShapeDtypeStruct((batch, t, w), bf16),
        scratch_shapes=[pltpu.VMEM((batch, SUBLANES, cols), f32),
                        pltpu.VMEM((batch * nheads // 2, LANES, 4 * LANES), f32),
                        pltpu.VMEM((batch * SUBLANES, GATE_LANES), f32), pltpu.VMEM((rows, cols // 2), f32),
                        pltpu.VMEM((rows, cols // 2), f32)] + [blk() for _ in range(7)]
                       + [pltpu.VMEM((rows // CHUNK * SUBLANES, w), f32), pltpu.VMEM((rows, 2 * w), f32),
                          pltpu.VMEM((rows // CHUNK, nheads // 2, LANES, 4 * LANES), f32)],
        compiler_params=_cparams(("arbitrary",)),
        name="mlstm_mix",
    )(p_qk.reshape(batch, t, cols), p_v.reshape(batch, t, w), p_og.reshape(batch, t, w),
      p_gates.reshape(batch, t, GATE_LANES), *consts)
    return out.reshape(n, w)


def kernel(x, norm_mix_g, norm_mlp_g, mlp_up, mlp_down, final_g, ev_w_in, ev_w_out, lru_conv_w, lru_conv_b, lru_wa,
           lru_ba, lru_wx, lru_bx, lru_lambda, rwkv_mu, rwkv_w0, rwkv_w2, rwkv_a0, rwkv_a2, rwkv_g2, rwkv_kk,
           rwkv_ka, rwkv_rk, rwkv_lnw, rwkv_lnb, od_w_in, od_w_out, gdn_conv_w, gdn_a_log, gdn_dt_bias, gdn_norm_g,
           mlstm_conv_w, mlstm_ig_b, mlstm_fg_b, mlstm_norm_g):
    batch, seq, d = x.shape
    depth = norm_mix_g.shape[0]
    xs = x.reshape(batch * seq, d)
    ev_w, od_w, ev_wo, od_wo, up_w, down_w = ev_w_in, od_w_in, ev_w_out, od_w_out, mlp_up, mlp_down
    for l in range(depth):
        if l % 2 == 0:
            e = l // 2
            lru_cols = 2 * lru_lambda.shape[1]
            p_lru, p_rwkv = norm_proj(xs, norm_mix_g[l], [(ev_w, _layer_spec(ev_w, e))],
                                      [(lru_cols, ev_w.shape[2] - lru_cols)])
            ya = lru_mix(p_lru, batch, lru_conv_w[e], lru_conv_b[e], lru_wa[e], lru_ba[e], lru_wx[e], lru_bx[e],
                         lru_lambda[e])
            yb = rwkv_mix(p_rwkv, batch, rwkv_mu[e], rwkv_w0[e], rwkv_w2[e], rwkv_a0[e], rwkv_a2[e], rwkv_g2[e],
                          rwkv_kk[e], rwkv_ka[e], rwkv_rk[e], rwkv_lnw[e], rwkv_lnb[e])
            w_out = (ev_wo, e)
        else:
            o = l // 2
            weights, widths = _odd_in_weights(od_w, o, gdn_conv_w.shape[2], gdn_norm_g.shape[1] * gdn_a_log.shape[1],
                                              mlstm_conv_w.shape[2], mlstm_norm_g.shape[1] * mlstm_ig_b.shape[1],
                                              gdn_a_log.shape[1], mlstm_ig_b.shape[1])
            p_qkv, p_z, p_mqk, p_mv, p_mog, p_gates = norm_proj(xs, norm_mix_g[l], weights, widths)
            ya = gdn_mix(p_qkv, p_z, p_gates, batch, gdn_conv_w[o], gdn_a_log[o], gdn_dt_bias[o], gdn_norm_g[o])
            yb = mlstm_mix(p_mqk, p_mv, p_mog, p_gates, batch, mlstm_conv_w[o], mlstm_ig_b[o], mlstm_fg_b[o],
                           mlstm_norm_g[o])
            w_out = (od_wo, o)
        xs = out_mlp(ya, yb, xs, w_out, norm_mlp_g[l], (up_w, l), (down_w, l), final_g, l == depth - 1)
    return xs.reshape(batch, seq, d)


def _odd_in_weights(w_all, layer, gdn_qkv, gdn_w, mlstm_qk, mlstm_w, gdn_heads, mlstm_heads):
    w = w_all[layer]
    lead = gdn_qkv + gdn_w
    m0 = lead + 2 * gdn_heads
    m1 = m0 + mlstm_qk + 2 * mlstm_w
    ngate = 2 * gdn_heads + 2 * mlstm_heads
    gates = jnp.concatenate([w[:, lead:m0], w[:, m1:m1 + 2 * mlstm_heads],
                             jnp.zeros((w.shape[0], GATE_LANES - ngate), w.dtype)], axis=1)
    mlstm = w[:, m0:m1]
    weights = [(w_all, _layer_spec(w_all, layer, lead)), (mlstm, _const_spec(mlstm.shape)),
               (gates, _const_spec(gates.shape))]
    return weights, [(gdn_qkv, gdn_w), (mlstm_qk, mlstm_w, mlstm_w), (GATE_LANES,)]
```

```python
import functools
import math

import jax
import jax.numpy as jnp
from jax import lax
from jax.experimental import pallas as pl
from jax.experimental.pallas import tpu as pltpu

f32 = jnp.float32
bf16 = jnp.bfloat16

LANES = 128
SUBLANES = 8
VMEM_LIMIT = 56 * 1024 * 1024

NORM_EPS = 1e-6
CONV_W = 4
LRU_C = 8.0
RWKV_HEAD = 64
RWKV_LN_EPS = 64e-5
CHUNK = 64

ROW_BLOCK = 512
SEQ_BLOCK = 128


def _cparams(sem):
    return pltpu.CompilerParams(dimension_semantics=sem, vmem_limit_bytes=VMEM_LIMIT)


def _const_spec(shape):
    nd = len(shape)
    return pl.BlockSpec(shape, lambda *_: (0,) * nd)


def _layer_spec(w, layer, cols=None):
    return pl.BlockSpec((None, w.shape[1], w.shape[2] if cols is None else cols), lambda *_: (layer, 0, 0))


def _rms(x, g):
    return x * lax.rsqrt(jnp.mean(x * x, axis=-1, keepdims=True) + NORM_EPS) * g


def _softplus(x):
    return jnp.maximum(x, 0.0) + jnp.log(1.0 + jnp.exp(-jnp.abs(x)))


def _sigmoid(x):
    return 0.5 * jnp.tanh(0.5 * x) + 0.5


def _norm_proj_body(widths, x_ref, g_ref, *refs):
    w_refs, o_refs = refs[:len(widths)], list(refs[len(widths):])
    h = _rms(x_ref[...], g_ref[...]).astype(bf16)
    for w_ref, group in zip(w_refs, widths):
        c0 = 0
        for wd in group:
            o_ref = o_refs.pop(0)
            w = w_ref[:, c0:c0 + wd].astype(bf16)
            o_ref[...] = jnp.dot(h, w, preferred_element_type=f32).astype(o_ref.dtype)
            c0 += wd


def norm_proj(x, g, weights, widths):
    n, d = x.shape
    tm = min(ROW_BLOCK, n)
    flat = [wd for group in widths for wd in group]
    return pl.pallas_call(
        functools.partial(_norm_proj_body, widths),
        grid=(n // tm,),
        in_specs=[pl.BlockSpec((tm, d), lambda i: (i, 0)), _const_spec((1, d))] + [spec for _, spec in weights],
        out_specs=[pl.BlockSpec((tm, wd), lambda i: (i, 0)) for wd in flat],
        out_shape=[jax.ShapeDtypeStruct((n, wd), f32) for wd in flat],
        compiler_params=_cparams(("parallel",)),
        name="norm_proj",
    )(x, g.reshape(1, d), *[w for w, _ in weights])


FF_BLOCK = 1024


def _out_mlp_body(final, ya_ref, yb_ref, x_ref, wo_ref, g_ref, wu_ref, wd_ref, gf_ref, o_ref):
    half = ya_ref.shape[1]
    y = jnp.dot(ya_ref[...], wo_ref[0:half, :].astype(bf16), preferred_element_type=f32)
    y = y + jnp.dot(yb_ref[...], wo_ref[half:, :].astype(bf16), preferred_element_type=f32)
    x1 = x_ref[...] + y
    h = _rms(x1, g_ref[...]).astype(bf16)
    acc = x1
    for c in range(wu_ref.shape[1] // FF_BLOCK):
        u = jnp.dot(h, wu_ref[:, c * FF_BLOCK:(c + 1) * FF_BLOCK].astype(bf16), preferred_element_type=f32)
        u = jnp.square(jnp.maximum(u, 0.0)).astype(bf16)
        acc = acc + jnp.dot(u, wd_ref[c * FF_BLOCK:(c + 1) * FF_BLOCK, :].astype(bf16), preferred_element_type=f32)
    if final:
        acc = _rms(acc, gf_ref[...])
    o_ref[...] = acc


def out_mlp(ya, yb, x, wo, g, wu, wd, gf, final):
    n, d = x.shape
    half = ya.shape[1]
    tm = min(ROW_BLOCK, n)
    row = lambda i: (i, 0)
    return pl.pallas_call(
        functools.partial(_out_mlp_body, final),
        grid=(n // tm,),
        in_specs=[pl.BlockSpec((tm, half), row), pl.BlockSpec((tm, half), row), pl.BlockSpec((tm, d), row),
                  _layer_spec(*wo), _const_spec((1, d)), _layer_spec(*wu), _layer_spec(*wd), _const_spec((1, d))],
        out_specs=pl.BlockSpec((tm, d), row),
        out_shape=jax.ShapeDtypeStruct((n, d), f32),
        compiler_params=_cparams(("parallel",)),
        name="out_mlp",
    )(ya, yb, x, wo[0], g.reshape(1, d), wu[0], wd[0], gf.reshape(1, d))


def _seq_specs(batch, tb, widths):
    return [pl.BlockSpec((batch, tb, wd), lambda i: (0, i, 0)) for wd in widths]


def _load_rows(ref):
    return jnp.concatenate([ref[b] for b in range(ref.shape[0])], axis=0)


def _store_rows(ref, x):
    tb = ref.shape[1]
    for b in range(ref.shape[0]):
        ref[b] = x[b * tb:(b + 1) * tb].astype(ref.dtype)


def _per_batch(fn, x, prev_ref):
    nb = prev_ref.shape[0]
    tb = x.shape[0] // nb
    out = []
    for b in range(nb):
        xb = x[b * tb:(b + 1) * tb]
        out.append(fn(xb, prev_ref[b]))
        prev_ref[b] = xb[tb - SUBLANES:]
    return jnp.concatenate(out, axis=0)


def _shift_rows(x, prev, k):
    rolled = pltpu.roll(x, k, axis=0)
    head = pltpu.roll(prev, k, axis=0)
    rows = lax.broadcasted_iota(jnp.int32, (SUBLANES, x.shape[1]), 0)
    top = jnp.where(rows < k, head, rolled[0:SUBLANES])
    return jnp.concatenate([top, rolled[SUBLANES:]], axis=0)


def _causal_conv(x, prev, w):
    y = x * w[CONV_W - 1:CONV_W]
    for k in range(1, CONV_W):
        y = y + _shift_rows(x, prev, k) * w[CONV_W - 1 - k:CONV_W - k]
    return y


def _scan_shift(x, s, fill):
    n, c = x.shape
    if s % SUBLANES == 0:
        return jnp.concatenate([jnp.full((s, c), fill, x.dtype), x[:n - s]], axis=0)
    rolled = pltpu.roll(x, s, axis=0)
    rows = lax.broadcasted_iota(jnp.int32, (SUBLANES, c), 0)
    top = jnp.where(rows < s, fill, rolled[0:SUBLANES])
    return jnp.concatenate([top, rolled[SUBLANES:]], axis=0)


def _linear_scan(a, u):
    s = 1
    while s < a.shape[0]:
        u = a * _scan_shift(u, s, 0.0) + u
        a = a * _scan_shift(a, s, 1.0)
        s *= 2
    return a, u


def _lru_body(p_ref, cw_ref, cb_ref, wg_ref, bg_ref, lam_ref, o_ref, xprev_ref, h_ref):
    nb, tb = p_ref.shape[0], p_ref.shape[1]
    w = o_ref.shape[2]
    step = pl.program_id(0)

    @pl.when(step == 0)
    def _():
        xprev_ref[...] = jnp.zeros_like(xprev_ref)
        h_ref[...] = jnp.zeros_like(h_ref)

    gate_in = jnp.concatenate([p_ref[b, :, 0:w] for b in range(nb)], axis=0)
    rec = jnp.concatenate([p_ref[b, :, w:] for b in range(nb)], axis=0)
    xc = _per_batch(lambda x, prev: _causal_conv(x, prev, cw_ref[...]), rec, xprev_ref) + cb_ref[...]
    gates = jnp.dot(xc.astype(bf16), wg_ref[...], preferred_element_type=f32) + bg_ref[...]
    r_gate = _sigmoid(gates[:, 0:w])
    i_gate = _sigmoid(gates[:, w:])
    log_a = (-LRU_C) * r_gate * _softplus(-lam_ref[...])
    a = jnp.exp(log_a)
    mult = jnp.sqrt(1.0 - jnp.exp(2.0 * log_a))
    rows = lax.broadcasted_iota(jnp.int32, (nb * tb, w), 0)
    mult = jnp.where(jnp.logical_and(rows % tb == 0, step == 0), 1.0, mult)
    u = i_gate * xc * mult
    hs = []
    for b in range(nb):
        a_cum, h = _linear_scan(a[b * tb:(b + 1) * tb], u[b * tb:(b + 1) * tb])
        h = h + a_cum * h_ref[b * SUBLANES:b * SUBLANES + 1, :]
        h_ref[b * SUBLANES:b * SUBLANES + 1, :] = h[tb - 1:tb]
        hs.append(h)
    _store_rows(o_ref, jnp.concatenate(hs, axis=0) * jax.nn.gelu(gate_in, approximate=True))


def _block_diag(wb):
    nb, d, e = wb.shape
    eye = jnp.eye(nb, dtype=wb.dtype)
    return (wb[:, :, None, :] * eye[:, None, :, None]).reshape(nb * d, nb * e)


def lru_mix(p, batch, conv_w, conv_b, wa, ba, wx, bx, lam):
    n, c2 = p.shape
    w = c2 // 2
    t = n // batch
    tb = min(SEQ_BLOCK, t)
    wg = jnp.concatenate([_block_diag(wa), _block_diag(wx)], axis=1).astype(bf16)
    bg = jnp.concatenate([ba, bx]).reshape(1, 2 * w)
    out = pl.pallas_call(
        _lru_body,
        grid=(t // tb,),
        in_specs=_seq_specs(batch, tb, [c2]) + [_const_spec((CONV_W, w)), _const_spec((1, w)),
                                                _const_spec((w, 2 * w)), _const_spec((1, 2 * w)), _const_spec((1, w))],
        out_specs=_seq_specs(batch, tb, [w])[0],
        out_shape=jax.ShapeDtypeStruct((batch, t, w), bf16),
        scratch_shapes=[pltpu.VMEM((batch, SUBLANES, w), f32), pltpu.VMEM((batch * SUBLANES, w), f32)],
        compiler_params=_cparams(("arbitrary",)),
        name="lru_mix",
    )(p.reshape(batch, t, c2), conv_w, conv_b.reshape(1, w), wg, bg, lam.reshape(1, w))
    return out.reshape(n, w)


UNIT_GROUP = 32


def _dot(a, b):
    return jnp.dot(a.astype(bf16), b.astype(bf16), preferred_element_type=f32)


def _dot_nt(a, b):
    return lax.dot_general(a.astype(bf16), b.astype(bf16), (((1,), (1,)), ((), ())), preferred_element_type=f32)


def _dot_tn(a, b):
    return lax.dot_general(a.astype(bf16), b.astype(bf16), (((0,), (0,)), ((), ())), preferred_element_type=f32)


def _dot_split(x, e, terms=2):
    out = None
    for _ in range(terms):
        hi = x.astype(bf16)
        x = x - hi.astype(f32)
        part = jnp.dot(hi, e, preferred_element_type=f32)
        out = part if out is None else out + part
    return out


def _iota(shape, axis):
    return lax.broadcasted_iota(jnp.int32, shape, axis)


def _tri_incl():
    return (_iota((CHUNK, CHUNK), 0) >= _iota((CHUNK, CHUNK), 1)).astype(bf16)


def _chunk_cumsum(x):
    tri = _tri_incl()
    cs, ce = [], []
    for c in range(x.shape[0] // CHUNK):
        rest = x[c * CHUNK:(c + 1) * CHUNK]
        cs_c = None
        for _ in range(3):
            hi = rest.astype(bf16)
            rest = rest - hi.astype(f32)
            part = jnp.dot(tri, hi, preferred_element_type=f32)
            cs_c = part if cs_c is None else cs_c + part
        cs.append(cs_c)
        ce.append(jnp.broadcast_to(cs_c[CHUNK - 1:CHUNK], cs_c.shape))
    return jnp.concatenate(cs, axis=0), jnp.concatenate(ce, axis=0)


def _head_pair_ones(head):
    return (_iota((LANES, LANES), 0) // head == _iota((LANES, LANES), 1) // head).astype(bf16)


def _segsum(x, e):
    return jnp.concatenate([_dot_split(x[:, j:j + LANES], e) for j in range(0, x.shape[1], LANES)], axis=1)


def _stack_heads(x):
    lo = _iota(x.shape, 1) < RWKV_HEAD
    return jnp.concatenate([jnp.where(lo, x, 0.0), jnp.where(lo, 0.0, x)], axis=0)


def _unit_lower_inverse(amats):
    shape = amats[0].shape
    ii = _iota(shape, 0)
    jj = _iota(shape, 1) % CHUNK
    eye = (ii == jj).astype(f32)
    first = jnp.logical_and(ii // 2 == jj // 2, ii > jj)
    ts = [eye - jnp.where(first, a, 0.0) for a in amats]
    abf = [a.astype(bf16) for a in amats]
    b = 2
    while b < CHUNK:
        off = jnp.logical_and(ii // (2 * b) == jj // (2 * b), ii // b > jj // b)
        xs = [_dot(a, _stack_heads(t)) for a, t in zip(abf, ts)]
        ts = [t - jnp.where(off, _dot(t, _stack_heads(x)), 0.0) for t, x in zip(ts, xs)]
        b *= 2
    return ts


def _rwkv_body(p_ref, mu_ref, w0_ref, w2_ref, a0_ref, a2_ref, g2_ref, kk_ref, ka_ref, rk_ref, lnw_ref, lnb_ref,
               o_ref, prev_ref, s_ref, at_s, rt_s, bt_s, kt_s, gc_s, v_s, y_s, wh_s, uh_s, mrb_s):
    nb, tb = p_ref.shape[0], p_ref.shape[1]
    n = nb * tb
    w = o_ref.shape[2]
    npair = w // LANES

    @pl.when(pl.program_id(0) == 0)
    def _():
        prev_ref[...] = jnp.zeros_like(prev_ref)
        s_ref[...] = jnp.zeros_like(s_ref)

    p = _load_rows(p_ref)
    pf = p + mu_ref[...] * (_per_batch(lambda x, prev: _shift_rows(x, prev, 1), p, prev_ref) - p)
    r, k, v = pf[:, 0:w], pf[:, w:2 * w], pf[:, 2 * w:3 * w]
    lowrank = pf[:, 3 * w:3 * w + LANES]
    xg = pf[:, 3 * w + LANES:]
    lw = (-math.exp(-0.5)) * _sigmoid(w0_ref[...] + _dot(jnp.tanh(lowrank), w2_ref[...]))
    a = _sigmoid(a0_ref[...] + _dot(lowrank, a2_ref[...]))
    g = _dot(_sigmoid(xg), g2_ref[...])
    ones = _head_pair_ones(RWKV_HEAD)
    kk = k * kk_ref[...]
    kk = kk * lax.rsqrt(_segsum(kk * kk, ones) + 1e-6)
    k = k * (1.0 + (a - 1.0) * ka_ref[...])
    cs, ce = _chunk_cumsum(lw)
    inv_g = jnp.exp(-cs)
    at_s[...] = kk * jnp.exp(cs - lw)
    rt_s[...] = r * jnp.exp(cs)
    bt_s[...] = -(kk * a) * inv_g
    kt_s[...] = k * inv_g
    gc_s[...] = jnp.exp(ce)
    v_s[...] = v

    tt = _iota((CHUNK, LANES), 0)
    ss = _iota((CHUNK, LANES), 1) % CHUNK
    strict = tt > ss
    incl = tt >= ss
    t2 = _iota((2 * CHUNK, LANES), 0)
    s2 = _iota((2 * CHUNK, LANES), 1) % CHUNK
    strict_incl = jnp.logical_or(t2 % CHUNK > s2, jnp.logical_and(t2 >= CHUNK, t2 % CHUNK == s2))
    diag_blocks = (_iota((LANES, LANES), 0) < RWKV_HEAD) == (_iota((LANES, LANES), 1) < RWKV_HEAD)

    pair_cols = [slice(j * LANES, (j + 1) * LANES) for j in range(npair)]
    all_units = [(slice(c * CHUNK, (c + 1) * CHUNK), cols) for c in range(n // CHUNK) for cols in pair_cols]
    for u0 in range(0, len(all_units), UNIT_GROUP):
        units = all_units[u0:u0 + UNIT_GROUP]
        gms = [_dot_nt(jnp.concatenate([at_s[u], rt_s[u]], axis=0),
                       jnp.concatenate([_stack_heads(bt_s[u]), _stack_heads(kt_s[u])], axis=0)) for u in units]
        for u, gm in zip(units, gms):
            mrb_s[u] = jnp.where(incl, gm[CHUNK:, 0:LANES], 0.0)
        vsts = [_stack_heads(v_s[u]) for u in units]
        avs = [_dot(jnp.where(strict_incl, gm[:, LANES:], 0.0), vst) for gm, vst in zip(gms, vsts)]
        akvs = [av[0:CHUNK] for av in avs]
        for u, av in zip(units, avs):
            y_s[u] = av[CHUNK:]
        tinvs = _unit_lower_inverse([jnp.where(strict, -gm[0:CHUNK, 0:LANES], 0.0) for gm in gms])
        wus = [_dot(t, jnp.concatenate([_stack_heads(at_s[u]), _stack_heads(akv)], axis=1))
               for u, t, akv in zip(units, tinvs, akvs)]
        for u, wu in zip(units, wus):
            wh_s[u] = wu[:, 0:LANES]
            uh_s[u] = wu[:, LANES:]

    def chunk_step(c, carry):
        starts = [pl.multiple_of(c * CHUNK + b * tb, CHUNK) for b in range(nb)]
        chains = [(b * npair + j, pl.ds(starts[b], CHUNK), cols) for b in range(nb) for j, cols in enumerate(pair_cols)]
        ss = [s_ref[si] for si, _, _ in chains]
        xs = [_dot_nt(jnp.concatenate([wh_s[rows, cols], rt_s[rows, cols]], axis=0), s)
              for (_, rows, cols), s in zip(chains, ss)]
        us = [x[0:CHUNK] + uh_s[rows, cols] for (_, rows, cols), x in zip(chains, xs)]
        upds = [_dot_tn(jnp.concatenate([u, v_s[rows, cols]], axis=0),
                        jnp.concatenate([bt_s[rows, cols], kt_s[rows, cols]], axis=0))
                for (_, rows, cols), u in zip(chains, us)]
        ys = [_dot(mrb_s[rows, cols], _stack_heads(u)) for (_, rows, cols), u in zip(chains, us)]
        for (si, rows, cols), s, x, upd, y in zip(chains, ss, xs, upds, ys):
            s_ref[si] = (s + jnp.where(diag_blocks, upd, 0.0)) * gc_s[pl.ds(starts[si // npair], 1), cols]
            y_s[rows, cols] = y_s[rows, cols] + x[CHUNK:] + y
        return carry

    lax.fori_loop(0, tb // CHUNK, chunk_step, 0)

    y = y_s[...]
    inv_head = 1.0 / RWKV_HEAD
    mean = _segsum(y, ones) * inv_head
    yc = y - mean
    var = _segsum(yc * yc, ones) * inv_head
    yn = yc * lax.rsqrt(var + RWKV_LN_EPS) * lnw_ref[...] + lnb_ref[...]
    bonus = _segsum(r * k * rk_ref[...], ones) * v
    _store_rows(o_ref, (yn + bonus) * g)


def rwkv_mix(p, batch, mu, w0, w2, a0, a2, g2, k_k, k_a, r_k, ln_w, ln_b):
    n, cols = p.shape
    w = w0.shape[0]
    t = n // batch
    tb = min(SEQ_BLOCK, t)
    rank = w2.shape[0]
    zeros = jnp.zeros((LANES - rank, w), f32)
    w2p = jnp.concatenate([w2, zeros], axis=0).astype(bf16)
    a2p = jnp.concatenate([zeros, a2], axis=0).astype(bf16)
    row = lambda v: v.reshape(1, -1)
    consts = [row(mu), row(w0), w2p, row(a0), a2p, g2.astype(bf16), row(k_k), row(k_a), row(r_k), row(ln_w),
              row(ln_b)]
    blk = lambda: pltpu.VMEM((batch * tb, w), f32)
    out = pl.pallas_call(
        _rwkv_body,
        grid=(t // tb,),
        in_specs=_seq_specs(batch, tb, [cols]) + [_const_spec(c.shape) for c in consts],
        out_specs=_seq_specs(batch, tb, [w])[0],
        out_shape=jax.ShapeDtypeStruct((batch, t, w), bf16),
        scratch_shapes=[pltpu.VMEM((batch, SUBLANES, cols), f32),
                        pltpu.VMEM((batch * w // LANES, LANES, LANES), f32)] + [blk() for _ in range(10)],
        compiler_params=_cparams(("arbitrary",)),
        name="rwkv_mix",
    )(p.reshape(batch, t, cols), *consts)
    return out.reshape(n, w)


GATE_LANES = LANES


def _silu(x):
    return x * _sigmoid(x)


def _expand_gate(gates, first, nheads, terms=2):
    sel = (_iota((GATE_LANES, nheads * LANES), 0) - first == _iota((GATE_LANES, nheads * LANES), 1) // LANES)
    return _dot_split(gates, sel.astype(bf16), terms)


def _gate_row(v, first):
    return jnp.zeros((1, GATE_LANES), f32).at[0, first:first + v.shape[0]].set(v)


def _pair_cols(x_ref, rows, pair):
    c = 2 * pair * LANES
    return x_ref[rows, c:c + LANES], x_ref[rows, c + LANES:c + 2 * LANES]


def _pair_select(x0, x1):
    return jnp.where(_iota(x0.shape, 1) < CHUNK, x0, x1)


def _pair_row_form(x0, x1):
    return jnp.transpose(jnp.concatenate([x0, x1], axis=0))[0:CHUNK]


def _pair_block_rhs(x0, x1):
    return jnp.concatenate([jnp.concatenate([x0, jnp.zeros_like(x0)], axis=1),
                            jnp.concatenate([jnp.zeros_like(x1), x1], axis=1)], axis=0)


def _head_rms(o, g, ones):
    return o * lax.rsqrt(_segsum(o * o, ones) * (1.0 / LANES) + NORM_EPS) * g


def _gdn_body(p_ref, z_ref, gate_ref, cw_ref, alog_ref, dtb_ref, ng_ref, o_ref,
              prev_ref, s_ref, q_s, k_s, kb_s, vb_s, qd_s, kbg_s, kd_s, gc_s, gl_s, o_s, u_s, w_s, qk_s):
    nb, tb = p_ref.shape[0], p_ref.shape[1]
    n = nb * tb
    w = o_ref.shape[2]
    nheads = w // LANES

    @pl.when(pl.program_id(0) == 0)
    def _():
        prev_ref[...] = jnp.zeros_like(prev_ref)
        s_ref[...] = jnp.zeros_like(s_ref)

    qkv = _silu(_per_batch(lambda x, prev: _causal_conv(x, prev, cw_ref[...]), _load_rows(p_ref), prev_ref))
    ones = _head_pair_ones(LANES)
    q, k, v = qkv[:, 0:w], qkv[:, w:2 * w], qkv[:, 2 * w:]
    q = q * lax.rsqrt(_segsum(q * q, ones) + 1e-6) * (LANES ** -0.5)
    k = k * lax.rsqrt(_segsum(k * k, ones) + 1e-6)
    gates = _load_rows(gate_ref)
    beta = _expand_gate(_sigmoid(gates), 0, nheads)
    gc_c, ge_c = _chunk_cumsum(-jnp.exp(alog_ref[...]) * _softplus(gates + dtb_ref[...]))
    eg = _expand_gate(jnp.exp(gc_c), nheads, nheads)
    kb = k * beta
    q_s[...] = q
    k_s[...] = k
    kb_s[...] = kb
    vb_s[...] = v * beta
    qd_s[...] = q * eg
    kbg_s[...] = kb * eg
    kd_s[...] = k * _expand_gate(jnp.exp(ge_c - gc_c), nheads, nheads)
    gc_s[...] = _expand_gate(gc_c, nheads, nheads, terms=3)
    gl_s[...] = _expand_gate(jnp.exp(ge_c), nheads, nheads)

    tt = _iota((CHUNK, LANES), 0)
    ss = _iota((CHUNK, LANES), 1) % CHUNK
    strict = tt > ss
    incl = tt >= ss

    all_units = [(slice(c * CHUNK, (c + 1) * CHUNK), pair) for c in range(n // CHUNK) for pair in range(nheads // 2)]
    for u0 in range(0, len(all_units), UNIT_GROUP):
        units = all_units[u0:u0 + UNIT_GROUP]
        xs = [_dot_nt(jnp.concatenate(_pair_cols(kb_s, rows, pair) + _pair_cols(q_s, rows, pair), axis=0),
                      jnp.concatenate(_pair_cols(k_s, rows, pair), axis=0)) for rows, pair in units]
        amats = []
        for (rows, pair), x in zip(units, xs):
            g0, g1 = _pair_cols(gc_s, rows, pair)
            dec = jnp.exp(jnp.minimum(_pair_select(g0, g1) - _pair_row_form(g0, g1), 0.0))
            amats.append(jnp.where(strict, _pair_select(x[0:CHUNK], x[CHUNK:2 * CHUNK]) * dec, 0.0))
            qk_s[rows, pair * LANES:(pair + 1) * LANES] = jnp.where(
                incl, _pair_select(x[2 * CHUNK:3 * CHUNK], x[3 * CHUNK:]) * dec, 0.0)
        tinvs = _unit_lower_inverse(amats)
        uws = []
        for (rows, pair), tinv in zip(units, tinvs):
            vb0, vb1 = _pair_cols(vb_s, rows, pair)
            kg0, kg1 = _pair_cols(kbg_s, rows, pair)
            uws.append(_dot(tinv, _pair_block_rhs(jnp.concatenate([vb0, kg0], axis=1),
                                                  jnp.concatenate([vb1, kg1], axis=1))))
        for (rows, pair), uw in zip(units, uws):
            for h in range(2):
                cols = slice((2 * pair + h) * LANES, (2 * pair + h + 1) * LANES)
                u_s[rows, cols] = uw[:, 2 * h * LANES:(2 * h + 1) * LANES]
                w_s[rows, cols] = uw[:, (2 * h + 1) * LANES:(2 * h + 2) * LANES]

    head_cols = [slice(h * LANES, (h + 1) * LANES) for h in range(nheads)]

    def chunk_step(c, carry):
        starts = [pl.multiple_of(c * CHUNK + b * tb, CHUNK) for b in range(nb)]
        chains = [(b, h, pl.ds(starts[b], CHUNK), cols) for b in range(nb) for h, cols in enumerate(head_cols)]
        ss = [s_ref[b * nheads + h] for b, h, _, _ in chains]
        x2s = [_dot(jnp.concatenate([w_s[rows, cols], qd_s[rows, cols]], axis=0), s)
               for (_, _, rows, cols), s in zip(chains, ss)]
        vns = [u_s[rows, cols] - x2[0:CHUNK] for (_, _, rows, cols), x2 in zip(chains, x2s)]
        upds = [_dot_tn(kd_s[rows, cols], vn) for (_, _, rows, cols), vn in zip(chains, vns)]
        os = [_dot(qk_s[pl.ds(starts[b], CHUNK), pair * LANES:(pair + 1) * LANES],
                   _pair_block_rhs(vns[b * nheads + 2 * pair], vns[b * nheads + 2 * pair + 1]))
              for b in range(nb) for pair in range(nheads // 2)]
        for ci, (b, h, rows, cols) in enumerate(chains):
            s_ref[ci] = ss[ci] * gl_s[pl.ds(starts[b], 1), cols] + upds[ci]
            o_s[rows, cols] = x2s[ci][CHUNK:] + os[ci // 2][:, (h % 2) * LANES:(h % 2 + 1) * LANES]
        return carry

    lax.fori_loop(0, tb // CHUNK, chunk_step, 0)
    _store_rows(o_ref, _head_rms(o_s[...], ng_ref[...], ones) * _silu(_load_rows(z_ref)))


def gdn_mix(p_qkv, p_z, p_gates, batch, conv_w, a_log, dt_bias, norm_g):
    n, cols = p_qkv.shape
    w = p_z.shape[1]
    nheads = w // LANES
    t = n // batch
    tb = min(SEQ_BLOCK, t)
    consts = [conv_w, _gate_row(a_log, nheads), _gate_row(dt_bias, nheads), jnp.tile(norm_g, nheads).reshape(1, w)]
    blk = lambda: pltpu.VMEM((batch * tb, w), f32)
    out = pl.pallas_call(
        _gdn_body,
        grid=(t // tb,),
        in_specs=_seq_specs(batch, tb, [cols, w, GATE_LANES]) + [_const_spec(c.shape) for c in consts],
        out_specs=_seq_specs(batch, tb, [w])[0],
        out_shape=jax.ShapeDtypeStruct((batch, t, w), bf16),
        scratch_shapes=[pltpu.VMEM((batch, SUBLANES, cols), f32), pltpu.VMEM((batch * nheads, LANES, LANES), f32)]
                       + [blk() for _ in range(12)] + [pltpu.VMEM((batch * tb, w // 2), f32)],
        compiler_params=_cparams(("arbitrary",)),
        name="gdn_mix",
    )(p_qkv.reshape(batch, t, cols), p_z.reshape(batch, t, w), p_gates.reshape(batch, t, GATE_LANES), *consts)
    return out.reshape(n, w)


def _cummax_rows(x):
    out = []
    for c in range(x.shape[0] // CHUNK):
        y = x[c * CHUNK:(c + 1) * CHUNK]
        s = 1
        while s < CHUNK:
            y = jnp.maximum(y, _scan_shift(y, s, -jnp.inf))
            s *= 2
        out.append(y)
    return jnp.concatenate(out, axis=0)


def _mlstm_body(pqk_ref, v_ref, og_ref, gate_ref, cw_ref, igb_ref, fgb_ref, ng_ref, o_ref,
                prev_ref, s_ref, m_ref, q_s, k_s, cc_s, mx_s, wi_s, ws_s, ed_s, h_s, v_s, dp_s, intra_s, upd_s):
    nb, tb = pqk_ref.shape[0], pqk_ref.shape[1]
    n = nb * tb
    w = o_ref.shape[2]
    nheads = w // LANES
    hq = pqk_ref.shape[2] // 2

    @pl.when(pl.program_id(0) == 0)
    def _():
        prev_ref[...] = jnp.zeros_like(prev_ref)
        s_ref[...] = jnp.zeros_like(s_ref)
        m_ref[...] = jnp.zeros_like(m_ref)

    qk = _silu(_per_batch(lambda x, prev: _causal_conv(x, prev, cw_ref[...]), _load_rows(pqk_ref), prev_ref))
    q_s[...] = qk[:, 0:hq] * ((hq // nheads) ** -0.5)
    k_s[...] = qk[:, hq:]
    v_s[...] = _load_rows(v_ref)
    gates = _load_rows(gate_ref)
    first = 2 * nheads
    i_pre = gates + igb_ref[...]
    f_pre = pltpu.roll(gates, GATE_LANES - nheads, axis=1) + fgb_ref[...]
    b, b_end = _chunk_cumsum(-_softplus(-f_pre))
    cc = i_pre - b
    cm = _cummax_rows(cc)
    te = b_end + cc
    nc = n // CHUNK
    ncb = tb // CHUNK
    mx, w_inter, wk_scale, dprev = [], [], [], []
    for c in range(nc):
        bi = c // ncb
        m = m_ref[bi * SUBLANES:bi * SUBLANES + 1, :] if c % ncb == 0 else m
        rows = slice(c * CHUNK, (c + 1) * CHUNK)
        b_last = b_end[c * CHUNK:c * CHUNK + 1]
        m_new = jnp.maximum(b_last + m, jnp.max(te[rows], axis=0, keepdims=True))
        mx_c = jnp.maximum(m, cm[rows])
        mx.append(mx_c)
        w_inter.append(jnp.exp(m - mx_c))
        wk_scale.append(jnp.exp(te[rows] - m_new))
        dprev.append(jnp.broadcast_to(jnp.exp(b_last + m - m_new), (SUBLANES, GATE_LANES)))
        m = m_new
        if c % ncb == ncb - 1:
            m_ref[bi * SUBLANES:bi * SUBLANES + 1, :] = m
    mx = jnp.concatenate(mx, axis=0)
    cc_s[...] = _expand_gate(cc, first, nheads, terms=3)
    mx_s[...] = _expand_gate(mx, first, nheads, terms=3)
    wi_s[...] = _expand_gate(jnp.concatenate(w_inter, axis=0), first, nheads)
    ws_s[...] = _expand_gate(jnp.concatenate(wk_scale, axis=0), first, nheads)
    ed_s[...] = jnp.exp(-_expand_gate(b + mx, first, nheads, terms=3))
    dp_s[...] = _expand_gate(jnp.concatenate(dprev, axis=0), first, nheads)

    incl = _iota((CHUNK, LANES), 0) >= _iota((CHUNK, LANES), 1) % CHUNK
    rows128 = _iota((LANES, 4 * LANES), 0) < CHUNK
    lanes512 = _iota((LANES, 4 * LANES), 1) < 2 * LANES
    diag_blocks = rows128 == lanes512
    ones_v = jnp.ones((CHUNK, LANES), f32)
    npair = nheads // 2
    pair_cols = [slice(pair * LANES, (pair + 1) * LANES) for pair in range(npair)]

    def vext(rows, pair):
        v0, v1 = _pair_cols(v_s, rows, pair)
        return jnp.concatenate([v0, ones_v, v1, ones_v], axis=1)

    units = [(c, slice(c * CHUNK, (c + 1) * CHUNK), pair) for c in range(nc) for pair in range(npair)]
    qks = [_dot_nt(q_s[rows, pair_cols[pair]], _stack_heads(k_s[rows, pair_cols[pair]])) for _, rows, pair in units]
    wms = []
    for (_, rows, pair), qk_u in zip(units, qks):
        expo = _pair_row_form(*_pair_cols(cc_s, rows, pair)) - _pair_select(*_pair_cols(mx_s, rows, pair))
        wms.append(jnp.where(incl, jnp.exp(jnp.minimum(expo, 0.0)) * qk_u, 0.0))
    for (_, rows, pair), wm in zip(units, wms):
        ve = vext(rows, pair)
        intra_s[rows, 4 * pair * LANES:4 * (pair + 1) * LANES] = _dot(
            wm, jnp.where(diag_blocks, jnp.concatenate([ve, ve], axis=0), 0.0))
    for c, rows, pair in units:
        wk = k_s[rows, pair_cols[pair]] * _pair_select(*_pair_cols(ws_s, rows, pair))
        upd_s[c, pair] = jnp.where(diag_blocks, _dot_tn(wk, vext(rows, pair)), 0.0)

    def chunk_step(c, carry):
        chains = [(b, pair) for b in range(nb) for pair in range(npair)]
        ss = [s_ref[b * npair + pair] for b, pair in chains]
        inters = [_dot(q_s[pl.ds(pl.multiple_of(c * CHUNK + b * tb, CHUNK), CHUNK), pair_cols[pair]], s)
                  for (b, pair), s in zip(chains, ss)]
        for (b, pair), s, inter in zip(chains, ss, inters):
            g = c + b * ncb
            rows = pl.ds(pl.multiple_of(c * CHUNK + b * tb, CHUNK), CHUNK)
            dp = _pair_cols(dp_s, pl.ds(pl.multiple_of(g * SUBLANES, SUBLANES), 1), pair)
            s_ref[b * npair + pair] = s * jnp.concatenate([dp[0], dp[0], dp[1], dp[1]], axis=1) + upd_s[g, pair]
            for h in range(2):
                cols = slice((2 * pair + h) * LANES, (2 * pair + h + 1) * LANES)
                c0 = 2 * h * LANES
                i0 = 4 * pair * LANES + c0
                wi = wi_s[rows, cols]
                num = wi * inter[:, c0:c0 + LANES] + intra_s[rows, i0:i0 + LANES]
                den = wi * inter[:, c0 + LANES:c0 + 2 * LANES] + intra_s[rows, i0 + LANES:i0 + 2 * LANES]
                h_s[rows, cols] = num / jnp.maximum(jnp.abs(den), ed_s[rows, cols])
        return carry

    lax.fori_loop(0, ncb, chunk_step, 0)
    ones = _head_pair_ones(LANES)
    _store_rows(o_ref, _head_rms(h_s[...], ng_ref[...], ones) * _sigmoid(_load_rows(og_ref)))


def mlstm_mix(p_qk, p_v, p_og, p_gates, batch, conv_w, ig_b, fg_b, norm_g):
    n, cols = p_qk.shape
    w = p_v.shape[1]
    nheads = w // LANES
    t = n // batch
    tb = min(SEQ_BLOCK, t)
    rows = batch * tb
    consts = [conv_w, _gate_row(ig_b, 2 * nheads), _gate_row(fg_b, 2 * nheads),
              jnp.tile(norm_g, nheads).reshape(1, w)]
    blk = lambda: pltpu.VMEM((rows, w), f32)
    out = pl.pallas_call(
        _mlstm_body,
        grid=(t // tb,),
        in_specs=_seq_specs(batch, tb, [cols, w, w, GATE_LANES]) + [_const_spec(c.shape) for c in consts],
        out_specs=_seq_specs(batch, tb, [w])[0],
        out_shape=jax.ShapeDtypeStruct((batch, t, w), bf16),
        scratch_shapes=[pltpu.VMEM((batch, SUBLANES, cols), f32),
                        pltpu.VMEM((batch * nheads // 2, LANES, 4 * LANES), f32),
                        pltpu.VMEM((batch * SUBLANES, GATE_LANES), f32), pltpu.VMEM((rows, cols // 2), f32),
                        pltpu.VMEM((rows, cols // 2), f32)] + [blk() for _ in range(7)]
                       + [pltpu.VMEM((rows // CHUNK * SUBLANES, w), f32), pltpu.VMEM((rows, 2 * w), f32),
                          pltpu.VMEM((rows // CHUNK, nheads // 2, LANES, 4 * LANES), f32)],
        compiler_params=_cparams(("arbitrary",)),
        name="mlstm_mix",
    )(p_qk.reshape(batch, t, cols), p_v.reshape(batch, t, w), p_og.reshape(batch, t, w),
      p_gates.reshape(batch, t, GATE_LANES), *consts)
    return out.reshape(n, w)


def kernel(x, norm_mix_g, norm_mlp_g, mlp_up, mlp_down, final_g, ev_w_in, ev_w_out, lru_conv_w, lru_conv_b, lru_wa,
           lru_ba, lru_wx, lru_bx, lru_lambda, rwkv_mu, rwkv_w0, rwkv_w2, rwkv_a0, rwkv_a2, rwkv_g2, rwkv_kk,
           rwkv_ka, rwkv_rk, rwkv_lnw, rwkv_lnb, od_w_in, od_w_out, gdn_conv_w, gdn_a_log, gdn_dt_bias, gdn_norm_g,
           mlstm_conv_w, mlstm_ig_b, mlstm_fg_b, mlstm_norm_g):
    batch, seq, d = x.shape
    depth = norm_mix_g.shape[0]
    xs = x.reshape(batch * seq, d)
    ev_w, od_w, ev_wo, od_wo, up_w, down_w = ev_w_in, od_w_in, ev_w_out, od_w_out, mlp_up, mlp_down
    for l in range(depth):
        if l % 2 == 0:
            e = l // 2
            lru_cols = 2 * lru_lambda.shape[1]
            p_lru, p_rwkv = norm_proj(xs, norm_mix_g[l], [(ev_w, _layer_spec(ev_w, e))],
                                      [(lru_cols, ev_w.shape[2] - lru_cols)])
            ya = lru_mix(p_lru, batch, lru_conv_w[e], lru_conv_b[e], lru_wa[e], lru_ba[e], lru_wx[e], lru_bx[e],
                         lru_lambda[e])
            yb = rwkv_mix(p_rwkv, batch, rwkv_mu[e], rwkv_w0[e], rwkv_w2[e], rwkv_a0[e], rwkv_a2[e], rwkv_g2[e],
                          rwkv_kk[e], rwkv_ka[e], rwkv_rk[e], rwkv_lnw[e], rwkv_lnb[e])
            w_out = (ev_wo, e)
        else:
            o = l // 2
            weights, widths = _odd_in_weights(od_w, o, gdn_conv_w.shape[2], gdn_norm_g.shape[1] * gdn_a_log.shape[1],
                                              mlstm_conv_w.shape[2], mlstm_norm_g.shape[1] * mlstm_ig_b.shape[1],
                                              gdn_a_log.shape[1], mlstm_ig_b.shape[1])
            p_qkv, p_z, p_mqk, p_mv, p_mog, p_gates = norm_proj(xs, norm_mix_g[l], weights, widths)
            ya = gdn_mix(p_qkv, p_z, p_gates, batch, gdn_conv_w[o], gdn_a_log[o], gdn_dt_bias[o], gdn_norm_g[o])
            yb = mlstm_mix(p_mqk, p_mv, p_mog, p_gates, batch, mlstm_conv_w[o], mlstm_ig_b[o], mlstm_fg_b[o],
                           mlstm_norm_g[o])
            w_out = (od_wo, o)
        xs = out_mlp(ya, yb, xs, w_out, norm_mlp_g[l], (up_w, l), (down_w, l), final_g, l == depth - 1)
    return xs.reshape(batch, seq, d)


def _odd_in_weights(w_all, layer, gdn_qkv, gdn_w, mlstm_qk, mlstm_w, gdn_heads, mlstm_heads):
    w = w_all[layer]
    lead = gdn_qkv + gdn_w
    m0 = lead + 2 * gdn_heads
    m1 = m0 + mlstm_qk + 2 * mlstm_w
    ngate = 2 * gdn_heads + 2 * mlstm_heads
    gates = jnp.concatenate([w[:, lead:m0], w[:, m1:m1 + 2 * mlstm_heads],
                             jnp.zeros((w.shape[0], GATE_LANES - ngate), w.dtype)], axis=1)
    weights = [(part, _const_spec(part.shape)) for part in (w[:, 0:lead], w[:, m0:m1], gates)]
    return weights, [(gdn_qkv, gdn_w), (mlstm_qk, mlstm_w, mlstm_w), (GATE_LANES,)]
```

```python
import functools
import math

import jax
import jax.numpy as jnp
from jax import lax
from jax.experimental import pallas as pl
from jax.experimental.pallas import tpu as pltpu

f32 = jnp.float32
bf16 = jnp.bfloat16

LANES = 128
SUBLANES = 8
VMEM_LIMIT = 56 * 1024 * 1024

NORM_EPS = 1e-6
CONV_W = 4
LRU_C = 8.0
RWKV_HEAD = 64
RWKV_LN_EPS = 64e-5
CHUNK = 64

ROW_BLOCK = 512
SEQ_BLOCK = 128


def _cparams(sem):
    return pltpu.CompilerParams(dimension_semantics=sem, vmem_limit_bytes=VMEM_LIMIT)


def _const_spec(shape):
    nd = len(shape)
    return pl.BlockSpec(shape, lambda *_: (0,) * nd)


def _layer_spec(w, layer, cols=None):
    return pl.BlockSpec((None, w.shape[1], w.shape[2] if cols is None else cols), lambda *_: (layer, 0, 0))


def _rms(x, g):
    return x * lax.rsqrt(jnp.mean(x * x, axis=-1, keepdims=True) + NORM_EPS) * g


def _softplus(x):
    return jnp.maximum(x, 0.0) + jnp.log(1.0 + jnp.exp(-jnp.abs(x)))


def _sigmoid(x):
    return 0.5 * jnp.tanh(0.5 * x) + 0.5


def _norm_proj_body(widths, x_ref, g_ref, *refs):
    w_refs, o_refs = refs[:len(widths)], list(refs[len(widths):])
    h = _rms(x_ref[...], g_ref[...]).astype(bf16)
    for w_ref, group in zip(w_refs, widths):
        c0 = 0
        for wd in group:
            o_ref = o_refs.pop(0)
            w = w_ref[:, c0:c0 + wd].astype(bf16)
            o_ref[...] = jnp.dot(h, w, preferred_element_type=f32).astype(o_ref.dtype)
            c0 += wd


def norm_proj(x, g, weights, widths):
    n, d = x.shape
    tm = min(ROW_BLOCK, n)
    flat = [wd for group in widths for wd in group]
    return pl.pallas_call(
        functools.partial(_norm_proj_body, widths),
        grid=(n // tm,),
        in_specs=[pl.BlockSpec((tm, d), lambda i: (i, 0)), _const_spec((1, d))] + [spec for _, spec in weights],
        out_specs=[pl.BlockSpec((tm, wd), lambda i: (i, 0)) for wd in flat],
        out_shape=[jax.ShapeDtypeStruct((n, wd), f32) for wd in flat],
        compiler_params=_cparams(("parallel",)),
        name="norm_proj",
    )(x, g.reshape(1, d), *[w for w, _ in weights])


FF_BLOCK = 1024


def _out_mlp_body(final, ya_ref, yb_ref, x_ref, wo_ref, g_ref, wu_ref, wd_ref, gf_ref, o_ref):
    half = ya_ref.shape[1]
    y = jnp.dot(ya_ref[...], wo_ref[0:half, :].astype(bf16), preferred_element_type=f32)
    y = y + jnp.dot(yb_ref[...], wo_ref[half:, :].astype(bf16), preferred_element_type=f32)
    x1 = x_ref[...] + y
    h = _rms(x1, g_ref[...]).astype(bf16)
    acc = x1
    for c in range(wu_ref.shape[1] // FF_BLOCK):
        u = jnp.dot(h, wu_ref[:, c * FF_BLOCK:(c + 1) * FF_BLOCK].astype(bf16), preferred_element_type=f32)
        u = jnp.square(jnp.maximum(u, 0.0)).astype(bf16)
        acc = acc + jnp.dot(u, wd_ref[c * FF_BLOCK:(c + 1) * FF_BLOCK, :].astype(bf16), preferred_element_type=f32)
    if final:
        acc = _rms(acc, gf_ref[...])
    o_ref[...] = acc


def out_mlp(ya, yb, x, wo, g, wu, wd, gf, final):
    n, d = x.shape
    half = ya.shape[1]
    tm = min(ROW_BLOCK, n)
    row = lambda i: (i, 0)
    return pl.pallas_call(
        functools.partial(_out_mlp_body, final),
        grid=(n // tm,),
        in_specs=[pl.BlockSpec((tm, half), row), pl.BlockSpec((tm, half), row), pl.BlockSpec((tm, d), row),
                  _layer_spec(*wo), _const_spec((1, d)), _layer_spec(*wu), _layer_spec(*wd), _const_spec((1, d))],
        out_specs=pl.BlockSpec((tm, d), row),
        out_shape=jax.ShapeDtypeStruct((n, d), f32),
        compiler_params=_cparams(("parallel",)),
        name="out_mlp",
    )(ya, yb, x, wo[0], g.reshape(1, d), wu[0], wd[0], gf.reshape(1, d))


def _seq_specs(batch, tb, widths):
    return [pl.BlockSpec((batch, tb, wd), lambda i: (0, i, 0)) for wd in widths]


def _load_rows(ref):
    return jnp.concatenate([ref[b] for b in range(ref.shape[0])], axis=0)


def _store_rows(ref, x):
    tb = ref.shape[1]
    for b in range(ref.shape[0]):
        ref[b] = x[b * tb:(b + 1) * tb].astype(ref.dtype)


def _per_batch(fn, x, prev_ref):
    nb = prev_ref.shape[0]
    tb = x.shape[0] // nb
    out = []
    for b in range(nb):
        xb = x[b * tb:(b + 1) * tb]
        out.append(fn(xb, prev_ref[b]))
        prev_ref[b] = xb[tb - SUBLANES:]
    return jnp.concatenate(out, axis=0)


def _shift_rows(x, prev, k):
    rolled = pltpu.roll(x, k, axis=0)
    head = pltpu.roll(prev, k, axis=0)
    rows = lax.broadcasted_iota(jnp.int32, (SUBLANES, x.shape[1]), 0)
    top = jnp.where(rows < k, head, rolled[0:SUBLANES])
    return jnp.concatenate([top, rolled[SUBLANES:]], axis=0)


def _causal_conv(x, prev, w):
    y = x * w[CONV_W - 1:CONV_W]
    for k in range(1, CONV_W):
        y = y + _shift_rows(x, prev, k) * w[CONV_W - 1 - k:CONV_W - k]
    return y


def _scan_shift(x, s, fill):
    n, c = x.shape
    if s % SUBLANES == 0:
        return jnp.concatenate([jnp.full((s, c), fill, x.dtype), x[:n - s]], axis=0)
    rolled = pltpu.roll(x, s, axis=0)
    rows = lax.broadcasted_iota(jnp.int32, (SUBLANES, c), 0)
    top = jnp.where(rows < s, fill, rolled[0:SUBLANES])
    return jnp.concatenate([top, rolled[SUBLANES:]], axis=0)


def _linear_scan(a, u):
    s = 1
    while s < a.shape[0]:
        u = a * _scan_shift(u, s, 0.0) + u
        a = a * _scan_shift(a, s, 1.0)
        s *= 2
    return a, u


def _lru_body(p_ref, cw_ref, cb_ref, wg_ref, bg_ref, lam_ref, o_ref, xprev_ref, h_ref):
    nb, tb = p_ref.shape[0], p_ref.shape[1]
    w = o_ref.shape[2]
    step = pl.program_id(0)

    @pl.when(step == 0)
    def _():
        xprev_ref[...] = jnp.zeros_like(xprev_ref)
        h_ref[...] = jnp.zeros_like(h_ref)

    gate_in = jnp.concatenate([p_ref[b, :, 0:w] for b in range(nb)], axis=0)
    rec = jnp.concatenate([p_ref[b, :, w:] for b in range(nb)], axis=0)
    xc = _per_batch(lambda x, prev: _causal_conv(x, prev, cw_ref[...]), rec, xprev_ref) + cb_ref[...]
    gates = jnp.dot(xc.astype(bf16), wg_ref[...], preferred_element_type=f32) + bg_ref[...]
    r_gate = _sigmoid(gates[:, 0:w])
    i_gate = _sigmoid(gates[:, w:])
    log_a = (-LRU_C) * r_gate * _softplus(-lam_ref[...])
    a = jnp.exp(log_a)
    mult = jnp.sqrt(1.0 - jnp.exp(2.0 * log_a))
    rows = lax.broadcasted_iota(jnp.int32, (nb * tb, w), 0)
    mult = jnp.where(jnp.logical_and(rows % tb == 0, step == 0), 1.0, mult)
    u = i_gate * xc * mult
    hs = []
    for b in range(nb):
        a_cum, h = _linear_scan(a[b * tb:(b + 1) * tb], u[b * tb:(b + 1) * tb])
        h = h + a_cum * h_ref[b * SUBLANES:b * SUBLANES + 1, :]
        h_ref[b * SUBLANES:b * SUBLANES + 1, :] = h[tb - 1:tb]
        hs.append(h)
    _store_rows(o_ref, jnp.concatenate(hs, axis=0) * jax.nn.gelu(gate_in, approximate=True))


def _block_diag(wb):
    nb, d, e = wb.shape
    eye = jnp.eye(nb, dtype=wb.dtype)
    return (wb[:, :, None, :] * eye[:, None, :, None]).reshape(nb * d, nb * e)


def lru_mix(p, batch, conv_w, conv_b, wa, ba, wx, bx, lam):
    n, c2 = p.shape
    w = c2 // 2
    t = n // batch
    tb = min(SEQ_BLOCK, t)
    wg = jnp.concatenate([_block_diag(wa), _block_diag(wx)], axis=1).astype(bf16)
    bg = jnp.concatenate([ba, bx]).reshape(1, 2 * w)
    out = pl.pallas_call(
        _lru_body,
        grid=(t // tb,),
        in_specs=_seq_specs(batch, tb, [c2]) + [_const_spec((CONV_W, w)), _const_spec((1, w)),
                                                _const_spec((w, 2 * w)), _const_spec((1, 2 * w)), _const_spec((1, w))],
        out_specs=_seq_specs(batch, tb, [w])[0],
        out_shape=jax.ShapeDtypeStruct((batch, t, w), bf16),
        scratch_shapes=[pltpu.VMEM((batch, SUBLANES, w), f32), pltpu.VMEM((batch * SUBLANES, w), f32)],
        compiler_params=_cparams(("arbitrary",)),
        name="lru_mix",
    )(p.reshape(batch, t, c2), conv_w, conv_b.reshape(1, w), wg, bg, lam.reshape(1, w))
    return out.reshape(n, w)


UNIT_GROUP = 32


def _dot(a, b):
    return jnp.dot(a.astype(bf16), b.astype(bf16), preferred_element_type=f32)


def _dot_nt(a, b):
    return lax.dot_general(a.astype(bf16), b.astype(bf16), (((1,), (1,)), ((), ())), preferred_element_type=f32)


def _dot_tn(a, b):
    return lax.dot_general(a.astype(bf16), b.astype(bf16), (((0,), (0,)), ((), ())), preferred_element_type=f32)


def _dot_split(x, e, terms=2):
    out = None
    for _ in range(terms):
        hi = x.astype(bf16)
        x = x - hi.astype(f32)
        part = jnp.dot(hi, e, preferred_element_type=f32)
        out = part if out is None else out + part
    return out


def _iota(shape, axis):
    return lax.broadcasted_iota(jnp.int32, shape, axis)


def _tri_incl():
    return (_iota((CHUNK, CHUNK), 0) >= _iota((CHUNK, CHUNK), 1)).astype(bf16)


def _chunk_cumsum(x):
    tri = _tri_incl()
    cs, ce = [], []
    for c in range(x.shape[0] // CHUNK):
        rest = x[c * CHUNK:(c + 1) * CHUNK]
        cs_c = None
        for _ in range(3):
            hi = rest.astype(bf16)
            rest = rest - hi.astype(f32)
            part = jnp.dot(tri, hi, preferred_element_type=f32)
            cs_c = part if cs_c is None else cs_c + part
        cs.append(cs_c)
        ce.append(jnp.broadcast_to(cs_c[CHUNK - 1:CHUNK], cs_c.shape))
    return jnp.concatenate(cs, axis=0), jnp.concatenate(ce, axis=0)


def _head_pair_ones(head):
    return (_iota((LANES, LANES), 0) // head == _iota((LANES, LANES), 1) // head).astype(bf16)


def _segsum(x, e):
    return jnp.concatenate([_dot_split(x[:, j:j + LANES], e) for j in range(0, x.shape[1], LANES)], axis=1)


def _stack_heads(x):
    lo = _iota(x.shape, 1) < RWKV_HEAD
    return jnp.concatenate([jnp.where(lo, x, 0.0), jnp.where(lo, 0.0, x)], axis=0)


def _unit_lower_inverse(amats):
    shape = amats[0].shape
    ii = _iota(shape, 0)
    jj = _iota(shape, 1) % CHUNK
    eye = (ii == jj).astype(f32)
    first = jnp.logical_and(ii // 2 == jj // 2, ii > jj)
    ts = [eye - jnp.where(first, a, 0.0) for a in amats]
    abf = [a.astype(bf16) for a in amats]
    b = 2
    while b < CHUNK:
        off = jnp.logical_and(ii // (2 * b) == jj // (2 * b), ii // b > jj // b)
        xs = [_dot(a, _stack_heads(t)) for a, t in zip(abf, ts)]
        ts = [t - jnp.where(off, _dot(t, _stack_heads(x)), 0.0) for t, x in zip(ts, xs)]
        b *= 2
    return ts


def _rwkv_body(p_ref, mu_ref, w0_ref, w2_ref, a0_ref, a2_ref, g2_ref, kk_ref, ka_ref, rk_ref, lnw_ref, lnb_ref,
               o_ref, prev_ref, s_ref, at_s, rt_s, bt_s, kt_s, gc_s, v_s, y_s, wh_s, uh_s, mrb_s):
    nb, tb = p_ref.shape[0], p_ref.shape[1]
    n = nb * tb
    w = o_ref.shape[2]
    npair = w // LANES

    @pl.when(pl.program_id(0) == 0)
    def _():
        prev_ref[...] = jnp.zeros_like(prev_ref)
        s_ref[...] = jnp.zeros_like(s_ref)

    p = _load_rows(p_ref)
    pf = p + mu_ref[...] * (_per_batch(lambda x, prev: _shift_rows(x, prev, 1), p, prev_ref) - p)
    r, k, v = pf[:, 0:w], pf[:, w:2 * w], pf[:, 2 * w:3 * w]
    lowrank = pf[:, 3 * w:3 * w + LANES]
    xg = pf[:, 3 * w + LANES:]
    lw = (-math.exp(-0.5)) * _sigmoid(w0_ref[...] + _dot(jnp.tanh(lowrank), w2_ref[...]))
    a = _sigmoid(a0_ref[...] + _dot(lowrank, a2_ref[...]))
    g = _dot(_sigmoid(xg), g2_ref[...])
    ones = _head_pair_ones(RWKV_HEAD)
    kk = k * kk_ref[...]
    kk = kk * lax.rsqrt(_segsum(kk * kk, ones) + 1e-6)
    k = k * (1.0 + (a - 1.0) * ka_ref[...])
    cs, ce = _chunk_cumsum(lw)
    inv_g = jnp.exp(-cs)
    at_s[...] = kk * jnp.exp(cs - lw)
    rt_s[...] = r * jnp.exp(cs)
    bt_s[...] = -(kk * a) * inv_g
    kt_s[...] = k * inv_g
    gc_s[...] = jnp.exp(ce)
    v_s[...] = v

    tt = _iota((CHUNK, LANES), 0)
    ss = _iota((CHUNK, LANES), 1) % CHUNK
    strict = tt > ss
    incl = tt >= ss
    t2 = _iota((2 * CHUNK, LANES), 0)
    s2 = _iota((2 * CHUNK, LANES), 1) % CHUNK
    strict_incl = jnp.logical_or(t2 % CHUNK > s2, jnp.logical_and(t2 >= CHUNK, t2 % CHUNK == s2))
    diag_blocks = (_iota((LANES, LANES), 0) < RWKV_HEAD) == (_iota((LANES, LANES), 1) < RWKV_HEAD)

    pair_cols = [slice(j * LANES, (j + 1) * LANES) for j in range(npair)]
    all_units = [(slice(c * CHUNK, (c + 1) * CHUNK), cols) for c in range(n // CHUNK) for cols in pair_cols]
    for u0 in range(0, len(all_units), UNIT_GROUP):
        units = all_units[u0:u0 + UNIT_GROUP]
        gms = [_dot_nt(jnp.concatenate([at_s[u], rt_s[u]], axis=0),
                       jnp.concatenate([_stack_heads(bt_s[u]), _stack_heads(kt_s[u])], axis=0)) for u in units]
        for u, gm in zip(units, gms):
            mrb_s[u] = jnp.where(incl, gm[CHUNK:, 0:LANES], 0.0)
        vsts = [_stack_heads(v_s[u]) for u in units]
        avs = [_dot(jnp.where(strict_incl, gm[:, LANES:], 0.0), vst) for gm, vst in zip(gms, vsts)]
        akvs = [av[0:CHUNK] for av in avs]
        for u, av in zip(units, avs):
            y_s[u] = av[CHUNK:]
        tinvs = _unit_lower_inverse([jnp.where(strict, -gm[0:CHUNK, 0:LANES], 0.0) for gm in gms])
        wus = [_dot(t, jnp.concatenate([_stack_heads(at_s[u]), _stack_heads(akv)], axis=1))
               for u, t, akv in zip(units, tinvs, akvs)]
        for u, wu in zip(units, wus):
            wh_s[u] = wu[:, 0:LANES]
            uh_s[u] = wu[:, LANES:]

    def chunk_step(c, carry):
        starts = [pl.multiple_of(c * CHUNK + b * tb, CHUNK) for b in range(nb)]
        chains = [(b * npair + j, pl.ds(starts[b], CHUNK), cols) for b in range(nb) for j, cols in enumerate(pair_cols)]
        ss = [s_ref[si] for si, _, _ in chains]
        xs = [_dot_nt(jnp.concatenate([wh_s[rows, cols], rt_s[rows, cols]], axis=0), s)
              for (_, rows, cols), s in zip(chains, ss)]
        us = [x[0:CHUNK] + uh_s[rows, cols] for (_, rows, cols), x in zip(chains, xs)]
        upds = [_dot_tn(jnp.concatenate([u, v_s[rows, cols]], axis=0),
                        jnp.concatenate([bt_s[rows, cols], kt_s[rows, cols]], axis=0))
                for (_, rows, cols), u in zip(chains, us)]
        ys = [_dot(mrb_s[rows, cols], _stack_heads(u)) for (_, rows, cols), u in zip(chains, us)]
        for (si, rows, cols), s, x, upd, y in zip(chains, ss, xs, upds, ys):
            s_ref[si] = (s + jnp.where(diag_blocks, upd, 0.0)) * gc_s[pl.ds(starts[si // npair], 1), cols]
            y_s[rows, cols] = y_s[rows, cols] + x[CHUNK:] + y
        return carry

    lax.fori_loop(0, tb // CHUNK, chunk_step, 0)

    y = y_s[...]
    inv_head = 1.0 / RWKV_HEAD
    mean = _segsum(y, ones) * inv_head
    yc = y - mean
    var = _segsum(yc * yc, ones) * inv_head
    yn = yc * lax.rsqrt(var + RWKV_LN_EPS) * lnw_ref[...] + lnb_ref[...]
    bonus = _segsum(r * k * rk_ref[...], ones) * v
    _store_rows(o_ref, (yn + bonus) * g)


def rwkv_mix(p, batch, mu, w0, w2, a0, a2, g2, k_k, k_a, r_k, ln_w, ln_b):
    n, cols = p.shape
    w = w0.shape[0]
    t = n // batch
    tb = min(SEQ_BLOCK, t)
    rank = w2.shape[0]
    zeros = jnp.zeros((LANES - rank, w), f32)
    w2p = jnp.concatenate([w2, zeros], axis=0).astype(bf16)
    a2p = jnp.concatenate([zeros, a2], axis=0).astype(bf16)
    row = lambda v: v.reshape(1, -1)
    consts = [row(mu), row(w0), w2p, row(a0), a2p, g2.astype(bf16), row(k_k), row(k_a), row(r_k), row(ln_w),
              row(ln_b)]
    blk = lambda: pltpu.VMEM((batch * tb, w), f32)
    out = pl.pallas_call(
        _rwkv_body,
        grid=(t // tb,),
        in_specs=_seq_specs(batch, tb, [cols]) + [_const_spec(c.shape) for c in consts],
        out_specs=_seq_specs(batch, tb, [w])[0],
        out_shape=jax.ShapeDtypeStruct((batch, t, w), bf16),
        scratch_shapes=[pltpu.VMEM((batch, SUBLANES, cols), f32),
                        pltpu.VMEM((batch * w // LANES, LANES, LANES), f32)] + [blk() for _ in range(10)],
        compiler_params=_cparams(("arbitrary",)),
        name="rwkv_mix",
    )(p.reshape(batch, t, cols), *consts)
    return out.reshape(n, w)


GATE_LANES = LANES


def _silu(x):
    h = 0.5 * x
    return h * jnp.tanh(h) + h


def _expand_gate(gates, first, nheads, terms=2):
    sel = (_iota((GATE_LANES, nheads * LANES), 0) - first == _iota((GATE_LANES, nheads * LANES), 1) // LANES)
    return _dot_split(gates, sel.astype(bf16), terms)


def _gate_row(v, first):
    return jnp.zeros((1, GATE_LANES), f32).at[0, first:first + v.shape[0]].set(v)


def _pair_cols(x_ref, rows, pair):
    c = 2 * pair * LANES
    return x_ref[rows, c:c + LANES], x_ref[rows, c + LANES:c + 2 * LANES]


def _pair_select(x0, x1):
    return jnp.where(_iota(x0.shape, 1) < CHUNK, x0, x1)


def _pair_row_form(x0, x1):
    return jnp.transpose(jnp.concatenate([x0, x1], axis=0))[0:CHUNK]


def _pair_block_rhs(x0, x1):
    return jnp.concatenate([jnp.concatenate([x0, jnp.zeros_like(x0)], axis=1),
                            jnp.concatenate([jnp.zeros_like(x1), x1], axis=1)], axis=0)


def _head_sumsq(x):
    return jnp.concatenate(
        [jnp.broadcast_to(jnp.sum(jnp.square(x[:, j:j + LANES]), axis=-1, keepdims=True), (x.shape[0], LANES))
         for j in range(0, x.shape[1], LANES)], axis=1)


def _head_rms(o, g):
    return o * lax.rsqrt(_head_sumsq(o) * (1.0 / LANES) + NORM_EPS) * g


def _gdn_body(p_ref, z_ref, gate_ref, cw_ref, alog_ref, dtb_ref, ng_ref, o_ref,
              prev_ref, s_ref, q_s, k_s, kb_s, vb_s, qd_s, kbg_s, kd_s, gc_s, gl_s, o_s, u_s, w_s, qk_s):
    nb, tb = p_ref.shape[0], p_ref.shape[1]
    n = nb * tb
    w = o_ref.shape[2]
    nheads = w // LANES

    @pl.when(pl.program_id(0) == 0)
    def _():
        prev_ref[...] = jnp.zeros_like(prev_ref)
        s_ref[...] = jnp.zeros_like(s_ref)

    qkv = _silu(_per_batch(lambda x, prev: _causal_conv(x, prev, cw_ref[...]), _load_rows(p_ref), prev_ref))
    q, k, v = qkv[:, 0:w], qkv[:, w:2 * w], qkv[:, 2 * w:]
    q = q * lax.rsqrt(_head_sumsq(q) + 1e-6) * (LANES ** -0.5)
    k = k * lax.rsqrt(_head_sumsq(k) + 1e-6)
    gates = _load_rows(gate_ref)
    beta = _expand_gate(_sigmoid(gates), 0, nheads)
    gc_c, ge_c = _chunk_cumsum(-jnp.exp(alog_ref[...]) * _softplus(gates + dtb_ref[...]))
    eg = _expand_gate(jnp.exp(gc_c), nheads, nheads)
    kb = k * beta
    q_s[...] = q
    k_s[...] = k
    kb_s[...] = kb
    vb_s[...] = v * beta
    qd_s[...] = q * eg
    kbg_s[...] = kb * eg
    kd_s[...] = k * _expand_gate(jnp.exp(ge_c - gc_c), nheads, nheads)
    gc_s[...] = _expand_gate(gc_c, nheads, nheads, terms=3)
    gl_s[...] = _expand_gate(jnp.exp(ge_c), nheads, nheads)

    tt = _iota((CHUNK, LANES), 0)
    ss = _iota((CHUNK, LANES), 1) % CHUNK
    strict = tt > ss
    incl = tt >= ss

    all_units = [(slice(c * CHUNK, (c + 1) * CHUNK), pair) for c in range(n // CHUNK) for pair in range(nheads // 2)]
    for u0 in range(0, len(all_units), UNIT_GROUP):
        units = all_units[u0:u0 + UNIT_GROUP]
        xs = [_dot_nt(jnp.concatenate(_pair_cols(kb_s, rows, pair) + _pair_cols(q_s, rows, pair), axis=0),
                      jnp.concatenate(_pair_cols(k_s, rows, pair), axis=0)) for rows, pair in units]
        amats = []
        for (rows, pair), x in zip(units, xs):
            g0, g1 = _pair_cols(gc_s, rows, pair)
            dec = jnp.exp(jnp.minimum(_pair_select(g0, g1) - _pair_row_form(g0, g1), 0.0))
            amats.append(jnp.where(strict, _pair_select(x[0:CHUNK], x[CHUNK:2 * CHUNK]) * dec, 0.0))
            qk_s[rows, pair * LANES:(pair + 1) * LANES] = jnp.where(
                incl, _pair_select(x[2 * CHUNK:3 * CHUNK], x[3 * CHUNK:]) * dec, 0.0)
        tinvs = _unit_lower_inverse(amats)
        uws = []
        for (rows, pair), tinv in zip(units, tinvs):
            vb0, vb1 = _pair_cols(vb_s, rows, pair)
            kg0, kg1 = _pair_cols(kbg_s, rows, pair)
            uws.append(_dot(tinv, _pair_block_rhs(jnp.concatenate([vb0, kg0], axis=1),
                                                  jnp.concatenate([vb1, kg1], axis=1))))
        for (rows, pair), uw in zip(units, uws):
            for h in range(2):
                cols = slice((2 * pair + h) * LANES, (2 * pair + h + 1) * LANES)
                u_s[rows, cols] = uw[:, 2 * h * LANES:(2 * h + 1) * LANES]
                w_s[rows, cols] = uw[:, (2 * h + 1) * LANES:(2 * h + 2) * LANES]

    head_cols = [slice(h * LANES, (h + 1) * LANES) for h in range(nheads)]

    def chunk_step(c, carry):
        starts = [pl.multiple_of(c * CHUNK + b * tb, CHUNK) for b in range(nb)]
        chains = [(b, h, pl.ds(starts[b], CHUNK), cols) for b in range(nb) for h, cols in enumerate(head_cols)]
        ss = [s_ref[b * nheads + h] for b, h, _, _ in chains]
        x2s = [_dot(jnp.concatenate([w_s[rows, cols], qd_s[rows, cols]], axis=0), s)
               for (_, _, rows, cols), s in zip(chains, ss)]
        vns = [u_s[rows, cols] - x2[0:CHUNK] for (_, _, rows, cols), x2 in zip(chains, x2s)]
        upds = [_dot_tn(kd_s[rows, cols], vn) for (_, _, rows, cols), vn in zip(chains, vns)]
        os = [_dot(qk_s[pl.ds(starts[b], CHUNK), pair * LANES:(pair + 1) * LANES],
                   _pair_block_rhs(vns[b * nheads + 2 * pair], vns[b * nheads + 2 * pair + 1]))
              for b in range(nb) for pair in range(nheads // 2)]
        for ci, (b, h, rows, cols) in enumerate(chains):
            s_ref[ci] = ss[ci] * gl_s[pl.ds(starts[b], 1), cols] + upds[ci]
            o_s[rows, cols] = x2s[ci][CHUNK:] + os[ci // 2][:, (h % 2) * LANES:(h % 2 + 1) * LANES]
        return carry

    lax.fori_loop(0, tb // CHUNK, chunk_step, 0)
    _store_rows(o_ref, _head_rms(o_s[...], ng_ref[...]) * _silu(_load_rows(z_ref)))


def gdn_mix(p_qkv, p_z, p_gates, batch, conv_w, a_log, dt_bias, norm_g):
    n, cols = p_qkv.shape
    w = p_z.shape[1]
    nheads = w // LANES
    t = n // batch
    tb = min(SEQ_BLOCK, t)
    consts = [conv_w, _gate_row(a_log, nheads), _gate_row(dt_bias, nheads), jnp.tile(norm_g, nheads).reshape(1, w)]
    blk = lambda: pltpu.VMEM((batch * tb, w), f32)
    out = pl.pallas_call(
        _gdn_body,
        grid=(t // tb,),
        in_specs=_seq_specs(batch, tb, [cols, w, GATE_LANES]) + [_const_spec(c.shape) for c in consts],
        out_specs=_seq_specs(batch, tb, [w])[0],
        out_shape=jax.ShapeDtypeStruct((batch, t, w), bf16),
        scratch_shapes=[pltpu.VMEM((batch, SUBLANES, cols), f32), pltpu.VMEM((batch * nheads, LANES, LANES), f32)]
                       + [blk() for _ in range(12)] + [pltpu.VMEM((batch * tb, w // 2), f32)],
        compiler_params=_cparams(("arbitrary",)),
        name="gdn_mix",
    )(p_qkv.reshape(batch, t, cols), p_z.reshape(batch, t, w), p_gates.reshape(batch, t, GATE_LANES), *consts)
    return out.reshape(n, w)


def _cummax_rows(x):
    out = []
    for c in range(x.shape[0] // CHUNK):
        y = x[c * CHUNK:(c + 1) * CHUNK]
        s = 1
        while s < CHUNK:
            y = jnp.maximum(y, _scan_shift(y, s, -jnp.inf))
            s *= 2
        out.append(y)
    return jnp.concatenate(out, axis=0)


def _mlstm_body(pqk_ref, v_ref, og_ref, gate_ref, cw_ref, igb_ref, fgb_ref, ng_ref, o_ref,
                prev_ref, s_ref, m_ref, q_s, k_s, cc_s, mx_s, wi_s, ws_s, ed_s, h_s, v_s, dp_s, intra_s, upd_s):
    nb, tb = pqk_ref.shape[0], pqk_ref.shape[1]
    n = nb * tb
    w = o_ref.shape[2]
    nheads = w // LANES
    hq = pqk_ref.shape[2] // 2

    @pl.when(pl.program_id(0) == 0)
    def _():
        prev_ref[...] = jnp.zeros_like(prev_ref)
        s_ref[...] = jnp.zeros_like(s_ref)
        m_ref[...] = jnp.zeros_like(m_ref)

    qk = _silu(_per_batch(lambda x, prev: _causal_conv(x, prev, cw_ref[...]), _load_rows(pqk_ref), prev_ref))
    q_s[...] = qk[:, 0:hq] * ((hq // nheads) ** -0.5)
    k_s[...] = qk[:, hq:]
    v_s[...] = _load_rows(v_ref)
    gates = _load_rows(gate_ref)
    first = 2 * nheads
    i_pre = gates + igb_ref[...]
    f_pre = pltpu.roll(gates, GATE_LANES - nheads, axis=1) + fgb_ref[...]
    b, b_end = _chunk_cumsum(-_softplus(-f_pre))
    cc = i_pre - b
    cm = _cummax_rows(cc)
    te = b_end + cc
    nc = n // CHUNK
    ncb = tb // CHUNK
    mx, w_inter, wk_scale, dprev = [], [], [], []
    for c in range(nc):
        bi = c // ncb
        m = m_ref[bi * SUBLANES:bi * SUBLANES + 1, :] if c % ncb == 0 else m
        rows = slice(c * CHUNK, (c + 1) * CHUNK)
        b_last = b_end[c * CHUNK:c * CHUNK + 1]
        m_new = jnp.maximum(b_last + m, jnp.max(te[rows], axis=0, keepdims=True))
        mx_c = jnp.maximum(m, cm[rows])
        mx.append(mx_c)
        w_inter.append(jnp.exp(m - mx_c))
        wk_scale.append(jnp.exp(te[rows] - m_new))
        dprev.append(jnp.broadcast_to(jnp.exp(b_last + m - m_new), (SUBLANES, GATE_LANES)))
        m = m_new
        if c % ncb == ncb - 1:
            m_ref[bi * SUBLANES:bi * SUBLANES + 1, :] = m
    mx = jnp.concatenate(mx, axis=0)
    cc_s[...] = _expand_gate(cc, first, nheads, terms=3)
    mx_s[...] = _expand_gate(mx, first, nheads, terms=3)
    wi_s[...] = _expand_gate(jnp.concatenate(w_inter, axis=0), first, nheads)
    ws_s[...] = _expand_gate(jnp.concatenate(wk_scale, axis=0), first, nheads)
    ed_s[...] = jnp.exp(-_expand_gate(b + mx, first, nheads, terms=3))
    dp_s[...] = _expand_gate(jnp.concatenate(dprev, axis=0), first, nheads)

    incl = _iota((CHUNK, LANES), 0) >= _iota((CHUNK, LANES), 1) % CHUNK
    rows128 = _iota((LANES, 4 * LANES), 0) < CHUNK
    lanes512 = _iota((LANES, 4 * LANES), 1) < 2 * LANES
    diag_blocks = rows128 == lanes512
    ones_v = jnp.ones((CHUNK, LANES), f32)
    npair = nheads // 2
    pair_cols = [slice(pair * LANES, (pair + 1) * LANES) for pair in range(npair)]

    def vext(rows, pair):
        v0, v1 = _pair_cols(v_s, rows, pair)
        return jnp.concatenate([v0, ones_v, v1, ones_v], axis=1)

    units = [(c, slice(c * CHUNK, (c + 1) * CHUNK), pair) for c in range(nc) for pair in range(npair)]
    qks = [_dot_nt(q_s[rows, pair_cols[pair]], _stack_heads(k_s[rows, pair_cols[pair]])) for _, rows, pair in units]
    wms = []
    for (_, rows, pair), qk_u in zip(units, qks):
        expo = _pair_row_form(*_pair_cols(cc_s, rows, pair)) - _pair_select(*_pair_cols(mx_s, rows, pair))
        wms.append(jnp.where(incl, jnp.exp(jnp.minimum(expo, 0.0)) * qk_u, 0.0))
    for (_, rows, pair), wm in zip(units, wms):
        ve = vext(rows, pair)
        intra_s[rows, 4 * pair * LANES:4 * (pair + 1) * LANES] = _dot(
            wm, jnp.where(diag_blocks, jnp.concatenate([ve, ve], axis=0), 0.0))
    for c, rows, pair in units:
        wk = k_s[rows, pair_cols[pair]] * _pair_select(*_pair_cols(ws_s, rows, pair))
        upd_s[c, pair] = jnp.where(diag_blocks, _dot_tn(wk, vext(rows, pair)), 0.0)

    def chunk_step(c, carry):
        chains = [(b, pair) for b in range(nb) for pair in range(npair)]
        ss = [s_ref[b * npair + pair] for b, pair in chains]
        inters = [_dot(q_s[pl.ds(pl.multiple_of(c * CHUNK + b * tb, CHUNK), CHUNK), pair_cols[pair]], s)
                  for (b, pair), s in zip(chains, ss)]
        for (b, pair), s, inter in zip(chains, ss, inters):
            g = c + b * ncb
            rows = pl.ds(pl.multiple_of(c * CHUNK + b * tb, CHUNK), CHUNK)
            dp = _pair_cols(dp_s, pl.ds(pl.multiple_of(g * SUBLANES, SUBLANES), 1), pair)
            s_ref[b * npair + pair] = s * jnp.concatenate([dp[0], dp[0], dp[1], dp[1]], axis=1) + upd_s[g, pair]
            for h in range(2):
                cols = slice((2 * pair + h) * LANES, (2 * pair + h + 1) * LANES)
                c0 = 2 * h * LANES
                i0 = 4 * pair * LANES + c0
                wi = wi_s[rows, cols]
                num = wi * inter[:, c0:c0 + LANES] + intra_s[rows, i0:i0 + LANES]
                den = wi * inter[:, c0 + LANES:c0 + 2 * LANES] + intra_s[rows, i0 + LANES:i0 + 2 * LANES]
                h_s[rows, cols] = num / jnp.maximum(jnp.abs(den), ed_s[rows, cols])
        return carry

    lax.fori_loop(0, ncb, chunk_step, 0)
    _store_rows(o_ref, _head_rms(h_s[...], ng_ref[...]) * _sigmoid(_load_rows(og_ref)))


def mlstm_mix(p_qk, p_v, p_og, p_gates, batch, conv_w, ig_b, fg_b, norm_g):
    n, cols = p_qk.shape
    w = p_v.shape[1]
    nheads = w // LANES
    t = n // batch
    tb = min(SEQ_BLOCK, t)
    rows = batch * tb
    consts = [conv_w, _gate_row(ig_b, 2 * nheads), _gate_row(fg_b, 2 * nheads),
              jnp.tile(norm_g, nheads).reshape(1, w)]
    blk = lambda: pltpu.VMEM((rows, w), f32)
    out = pl.pallas_call(
        _mlstm_body,
        grid=(t // tb,),
        in_specs=_seq_specs(batch, tb, [cols, w, w, GATE_LANES]) + [_const_spec(c.shape) for c in consts],
        out_specs=_seq_specs(batch, tb, [w])[0],
        out_shape=jax.ShapeDtypeStruct((batch, t, w), bf16),
        scratch_shapes=[pltpu.VMEM((batch, SUBLANES, cols), f32),
                        pltpu.VMEM((batch * nheads // 2, LANES, 4 * LANES), f32),
                        pltpu.VMEM((batch * SUBLANES, GATE_LANES), f32), pltpu.VMEM((rows, cols // 2), f32),
                        pltpu.VMEM((rows, cols // 2), f32)] + [blk() for _ in range(7)]
                       + [pltpu.VMEM((rows // CHUNK * SUBLANES, w), f32), pltpu.VMEM((rows, 2 * w), f32),
                          pltpu.VMEM((rows // CHUNK, nheads // 2, LANES, 4 * LANES), f32)],
        compiler_params=_cparams(("arbitrary",)),
        name="mlstm_mix",
    )(p_qk.reshape(batch, t, cols), p_v.reshape(batch, t, w), p_og.reshape(batch, t, w),
      p_gates.reshape(batch, t, GATE_LANES), *consts)
    return out.reshape(n, w)


def kernel(x, norm_mix_g, norm_mlp_g, mlp_up, mlp_down, final_g, ev_w_in, ev_w_out, lru_conv_w, lru_conv_b, lru_wa,
           lru_ba, lru_wx, lru_bx, lru_lambda, rwkv_mu, rwkv_w0, rwkv_w2, rwkv_a0, rwkv_a2, rwkv_g2, rwkv_kk,
           rwkv_ka, rwkv_rk, rwkv_lnw, rwkv_lnb, od_w_in, od_w_out, gdn_conv_w, gdn_a_log, gdn_dt_bias, gdn_norm_g,
           mlstm_conv_w, mlstm_ig_b, mlstm_fg_b, mlstm_norm_g):
    batch, seq, d = x.shape
    depth = norm_mix_g.shape[0]
    xs = x.reshape(batch * seq, d)
    ev_w, od_w, ev_wo, od_wo, up_w, down_w = ev_w_in, od_w_in, ev_w_out, od_w_out, mlp_up, mlp_down
    for l in range(depth):
        if l % 2 == 0:
            e = l // 2
            lru_cols = 2 * lru_lambda.shape[1]
            p_lru, p_rwkv = norm_proj(xs, norm_mix_g[l], [(ev_w, _layer_spec(ev_w, e))],
                                      [(lru_cols, ev_w.shape[2] - lru_cols)])
            ya = lru_mix(p_lru, batch, lru_conv_w[e], lru_conv_b[e], lru_wa[e], lru_ba[e], lru_wx[e], lru_bx[e],
                         lru_lambda[e])
            yb = rwkv_mix(p_rwkv, batch, rwkv_mu[e], rwkv_w0[e], rwkv_w2[e], rwkv_a0[e], rwkv_a2[e], rwkv_g2[e],
                          rwkv_kk[e], rwkv_ka[e], rwkv_rk[e], rwkv_lnw[e], rwkv_lnb[e])
            w_out = (ev_wo, e)
        else:
            o = l // 2
            weights, widths = _odd_in_weights(od_w, o, gdn_conv_w.shape[2], gdn_norm_g.shape[1] * gdn_a_log.shape[1],
                                              mlstm_conv_w.shape[2], mlstm_norm_g.shape[1] * mlstm_ig_b.shape[1],
                                              gdn_a_log.shape[1], mlstm_ig_b.shape[1])
            p_qkv, p_z, p_mqk, p_mv, p_mog, p_gates = norm_proj(xs, norm_mix_g[l], weights, widths)
            ya = gdn_mix(p_qkv, p_z, p_gates, batch, gdn_conv_w[o], gdn_a_log[o], gdn_dt_bias[o], gdn_norm_g[o])
            yb = mlstm_mix(p_mqk, p_mv, p_mog, p_gates, batch, mlstm_conv_w[o], mlstm_ig_b[o], mlstm_fg_b[o],
                           mlstm_norm_g[o])
            w_out = (od_wo, o)
        xs = out_mlp(ya, yb, xs, w_out, norm_mlp_g[l], (up_w, l), (down_w, l), final_g, l == depth - 1)
    return xs.reshape(batch, seq, d)


def _odd_in_weights(w_all, layer, gdn_qkv, gdn_w, mlstm_qk, mlstm_w, gdn_heads, mlstm_heads):
    w = w_all[layer]
    lead = gdn_qkv + gdn_w
    m0 = lead + 2 * gdn_heads
    m1 = m0 + mlstm_qk + 2 * mlstm_w
    ngate = 2 * gdn_heads + 2 * mlstm_heads
    gates = jnp.concatenate([w[:, lead:m0], w[:, m1:m1 + 2 * mlstm_heads],
                             jnp.zeros((w.shape[0], GATE_LANES - ngate), w.dtype)], axis=1)
    weights = [(part, _const_spec(part.shape)) for part in (w[:, 0:lead], w[:, m0:m1], gates)]
    return weights, [(gdn_qkv, gdn_w), (mlstm_qk, mlstm_w, mlstm_w), (GATE_LANES,)]
```

```python
import functools
import math

import jax
import jax.numpy as jnp
from jax import lax
from jax.experimental import pallas as pl
from jax.experimental.pallas import tpu as pltpu

f32 = jnp.float32
bf16 = jnp.bfloat16

LANES = 128
SUBLANES = 8
VMEM_LIMIT = 56 * 1024 * 1024

NORM_EPS = 1e-6
CONV_W = 4
LRU_C = 8.0
RWKV_HEAD = 64
RWKV_LN_EPS = 64e-5
CHUNK = 64

ROW_BLOCK = 512
SEQ_BLOCK = 256
LRU_BLOCK = 128


def _cparams(sem):
    return pltpu.CompilerParams(dimension_semantics=sem, vmem_limit_bytes=VMEM_LIMIT)


def _const_spec(shape):
    nd = len(shape)
    return pl.BlockSpec(shape, lambda *_: (0,) * nd)


def _layer_spec(w, layer, cols=None):
    return pl.BlockSpec((None, w.shape[1], w.shape[2] if cols is None else cols), lambda *_: (layer, 0, 0))


def _rms(x, g):
    return x * lax.rsqrt(jnp.mean(x * x, axis=-1, keepdims=True) + NORM_EPS) * g


def _softplus(x):
    return jnp.maximum(x, 0.0) + jnp.log(1.0 + jnp.exp(-jnp.abs(x)))


def _sigmoid(x):
    return 0.5 * jnp.tanh(0.5 * x) + 0.5


def _norm_proj_body(widths, x_ref, g_ref, *refs):
    w_refs, o_refs = refs[:len(widths)], list(refs[len(widths):])
    h = _rms(x_ref[...], g_ref[...]).astype(bf16)
    for w_ref, group in zip(w_refs, widths):
        c0 = 0
        for wd in group:
            o_ref = o_refs.pop(0)
            w = w_ref[:, c0:c0 + wd].astype(bf16)
            o_ref[...] = jnp.dot(h, w, preferred_element_type=f32).astype(o_ref.dtype)
            c0 += wd


def norm_proj(x, g, weights, widths):
    n, d = x.shape
    tm = min(ROW_BLOCK, n)
    flat = [wd for group in widths for wd in group]
    return pl.pallas_call(
        functools.partial(_norm_proj_body, widths),
        grid=(n // tm,),
        in_specs=[pl.BlockSpec((tm, d), lambda i: (i, 0)), _const_spec((1, d))] + [spec for _, spec in weights],
        out_specs=[pl.BlockSpec((tm, wd), lambda i: (i, 0)) for wd in flat],
        out_shape=[jax.ShapeDtypeStruct((n, wd), f32) for wd in flat],
        compiler_params=_cparams(("parallel",)),
        name="norm_proj",
    )(x, g.reshape(1, d), *[w for w, _ in weights])


FF_BLOCK = 1024


def _out_mlp_body(final, ya_ref, yb_ref, x_ref, wo_ref, g_ref, wu_ref, wd_ref, gf_ref, o_ref):
    half = ya_ref.shape[1]
    y = jnp.dot(ya_ref[...], wo_ref[0:half, :].astype(bf16), preferred_element_type=f32)
    y = y + jnp.dot(yb_ref[...], wo_ref[half:, :].astype(bf16), preferred_element_type=f32)
    x1 = x_ref[...] + y
    h = _rms(x1, g_ref[...]).astype(bf16)
    acc = x1
    for c in range(wu_ref.shape[1] // FF_BLOCK):
        u = jnp.dot(h, wu_ref[:, c * FF_BLOCK:(c + 1) * FF_BLOCK].astype(bf16), preferred_element_type=f32)
        u = jnp.square(jnp.maximum(u, 0.0)).astype(bf16)
        acc = acc + jnp.dot(u, wd_ref[c * FF_BLOCK:(c + 1) * FF_BLOCK, :].astype(bf16), preferred_element_type=f32)
    if final:
        acc = _rms(acc, gf_ref[...])
    o_ref[...] = acc


def out_mlp(ya, yb, x, wo, g, wu, wd, gf, final):
    n, d = x.shape
    half = ya.shape[1]
    tm = min(ROW_BLOCK, n)
    row = lambda i: (i, 0)
    return pl.pallas_call(
        functools.partial(_out_mlp_body, final),
        grid=(n // tm,),
        in_specs=[pl.BlockSpec((tm, half), row), pl.BlockSpec((tm, half), row), pl.BlockSpec((tm, d), row),
                  _layer_spec(*wo), _const_spec((1, d)), _layer_spec(*wu), _layer_spec(*wd), _const_spec((1, d))],
        out_specs=pl.BlockSpec((tm, d), row),
        out_shape=jax.ShapeDtypeStruct((n, d), f32),
        compiler_params=_cparams(("parallel",)),
        name="out_mlp",
    )(ya, yb, x, wo[0], g.reshape(1, d), wu[0], wd[0], gf.reshape(1, d))


def _seq_specs(batch, tb, widths):
    return [pl.BlockSpec((batch, tb, wd), lambda i: (0, i, 0)) for wd in widths]


def _load_rows(ref):
    return jnp.concatenate([ref[b] for b in range(ref.shape[0])], axis=0)


def _store_rows(ref, x):
    tb = ref.shape[1]
    for b in range(ref.shape[0]):
        ref[b] = x[b * tb:(b + 1) * tb].astype(ref.dtype)


def _per_batch(fn, x, prev_ref):
    nb = prev_ref.shape[0]
    tb = x.shape[0] // nb
    out = []
    for b in range(nb):
        xb = x[b * tb:(b + 1) * tb]
        out.append(fn(xb, prev_ref[b]))
        prev_ref[b] = xb[tb - SUBLANES:]
    return jnp.concatenate(out, axis=0)


def _shift_rows(x, prev, k):
    rolled = pltpu.roll(x, k, axis=0)
    head = pltpu.roll(prev, k, axis=0)
    rows = lax.broadcasted_iota(jnp.int32, (SUBLANES, x.shape[1]), 0)
    top = jnp.where(rows < k, head, rolled[0:SUBLANES])
    return jnp.concatenate([top, rolled[SUBLANES:]], axis=0)


def _causal_conv(x, prev, w):
    y = x * w[CONV_W - 1:CONV_W]
    for k in range(1, CONV_W):
        y = y + _shift_rows(x, prev, k) * w[CONV_W - 1 - k:CONV_W - k]
    return y


def _scan_shift(x, s, fill):
    n, c = x.shape
    if s % SUBLANES == 0:
        return jnp.concatenate([jnp.full((s, c), fill, x.dtype), x[:n - s]], axis=0)
    rolled = pltpu.roll(x, s, axis=0)
    rows = lax.broadcasted_iota(jnp.int32, (SUBLANES, c), 0)
    top = jnp.where(rows < s, fill, rolled[0:SUBLANES])
    return jnp.concatenate([top, rolled[SUBLANES:]], axis=0)


def _linear_scan(a, u):
    s = 1
    while s < a.shape[0]:
        u = a * _scan_shift(u, s, 0.0) + u
        a = a * _scan_shift(a, s, 1.0)
        s *= 2
    return a, u


def _lru_body(p_ref, cw_ref, cb_ref, wg_ref, bg_ref, lam_ref, o_ref, xprev_ref, h_ref):
    nb, tb = p_ref.shape[0], p_ref.shape[1]
    w = o_ref.shape[2]
    step = pl.program_id(0)

    @pl.when(step == 0)
    def _():
        xprev_ref[...] = jnp.zeros_like(xprev_ref)
        h_ref[...] = jnp.zeros_like(h_ref)

    gate_in = jnp.concatenate([p_ref[b, :, 0:w] for b in range(nb)], axis=0)
    rec = jnp.concatenate([p_ref[b, :, w:] for b in range(nb)], axis=0)
    xc = _per_batch(lambda x, prev: _causal_conv(x, prev, cw_ref[...]), rec, xprev_ref) + cb_ref[...]
    gates = jnp.dot(xc.astype(bf16), wg_ref[...], preferred_element_type=f32) + bg_ref[...]
    r_gate = _sigmoid(gates[:, 0:w])
    i_gate = _sigmoid(gates[:, w:])
    log_a = (-LRU_C) * r_gate * _softplus(-lam_ref[...])
    a = jnp.exp(log_a)
    mult = jnp.sqrt(1.0 - jnp.exp(2.0 * log_a))
    rows = lax.broadcasted_iota(jnp.int32, (nb * tb, w), 0)
    mult = jnp.where(jnp.logical_and(rows % tb == 0, step == 0), 1.0, mult)
    u = i_gate * xc * mult
    hs = []
    for b in range(nb):
        a_cum, h = _linear_scan(a[b * tb:(b + 1) * tb], u[b * tb:(b + 1) * tb])
        h = h + a_cum * h_ref[b * SUBLANES:b * SUBLANES + 1, :]
        h_ref[b * SUBLANES:b * SUBLANES + 1, :] = h[tb - 1:tb]
        hs.append(h)
    _store_rows(o_ref, jnp.concatenate(hs, axis=0) * jax.nn.gelu(gate_in, approximate=True))


def _block_diag(wb):
    nb, d, e = wb.shape
    eye = jnp.eye(nb, dtype=wb.dtype)
    return (wb[:, :, None, :] * eye[:, None, :, None]).reshape(nb * d, nb * e)


def lru_mix(p, batch, conv_w, conv_b, wa, ba, wx, bx, lam):
    n, c2 = p.shape
    w = c2 // 2
    t = n // batch
    tb = min(LRU_BLOCK, t)
    wg = jnp.concatenate([_block_diag(wa), _block_diag(wx)], axis=1).astype(bf16)
    bg = jnp.concatenate([ba, bx]).reshape(1, 2 * w)
    out = pl.pallas_call(
        _lru_body,
        grid=(t // tb,),
        in_specs=_seq_specs(batch, tb, [c2]) + [_const_spec((CONV_W, w)), _const_spec((1, w)),
                                                _const_spec((w, 2 * w)), _const_spec((1, 2 * w)), _const_spec((1, w))],
        out_specs=_seq_specs(batch, tb, [w])[0],
        out_shape=jax.ShapeDtypeStruct((batch, t, w), bf16),
        scratch_shapes=[pltpu.VMEM((batch, SUBLANES, w), f32), pltpu.VMEM((batch * SUBLANES, w), f32)],
        compiler_params=_cparams(("arbitrary",)),
        name="lru_mix",
    )(p.reshape(batch, t, c2), conv_w, conv_b.reshape(1, w), wg, bg, lam.reshape(1, w))
    return out.reshape(n, w)


UNIT_GROUP = 32


def _dot(a, b):
    return jnp.dot(a.astype(bf16), b.astype(bf16), preferred_element_type=f32)


def _dot_nt(a, b):
    return lax.dot_general(a.astype(bf16), b.astype(bf16), (((1,), (1,)), ((), ())), preferred_element_type=f32)


def _dot_tn(a, b):
    return lax.dot_general(a.astype(bf16), b.astype(bf16), (((0,), (0,)), ((), ())), preferred_element_type=f32)


def _dot_split(x, e, terms=2):
    out = None
    for _ in range(terms):
        hi = x.astype(bf16)
        x = x - hi.astype(f32)
        part = jnp.dot(hi, e, preferred_element_type=f32)
        out = part if out is None else out + part
    return out


def _iota(shape, axis):
    return lax.broadcasted_iota(jnp.int32, shape, axis)


def _tri_incl():
    return (_iota((CHUNK, CHUNK), 0) >= _iota((CHUNK, CHUNK), 1)).astype(bf16)


def _chunk_cumsum(x):
    tri = _tri_incl()
    cs, ce = [], []
    for c in range(x.shape[0] // CHUNK):
        rest = x[c * CHUNK:(c + 1) * CHUNK]
        cs_c = None
        for _ in range(3):
            hi = rest.astype(bf16)
            rest = rest - hi.astype(f32)
            part = jnp.dot(tri, hi, preferred_element_type=f32)
            cs_c = part if cs_c is None else cs_c + part
        cs.append(cs_c)
        ce.append(jnp.broadcast_to(cs_c[CHUNK - 1:CHUNK], cs_c.shape))
    return jnp.concatenate(cs, axis=0), jnp.concatenate(ce, axis=0)


def _head_pair_ones(head):
    return (_iota((LANES, LANES), 0) // head == _iota((LANES, LANES), 1) // head).astype(bf16)


def _segsum(x, e):
    return jnp.concatenate([_dot_split(x[:, j:j + LANES], e) for j in range(0, x.shape[1], LANES)], axis=1)


def _stack_heads(x):
    lo = _iota(x.shape, 1) < RWKV_HEAD
    return jnp.concatenate([jnp.where(lo, x, 0.0), jnp.where(lo, 0.0, x)], axis=0)


def _unit_lower_inverse(amats):
    shape = amats[0].shape
    ii = _iota(shape, 0)
    jj = _iota(shape, 1) % CHUNK
    eye = (ii == jj).astype(f32)
    first = jnp.logical_and(ii // 2 == jj // 2, ii > jj)
    ts = [eye - jnp.where(first, a, 0.0) for a in amats]
    abf = [a.astype(bf16) for a in amats]
    b = 2
    while b < CHUNK:
        off = jnp.logical_and(ii // (2 * b) == jj // (2 * b), ii // b > jj // b)
        xs = [_dot(a, _stack_heads(t)) for a, t in zip(abf, ts)]
        ts = [t - jnp.where(off, _dot(t, _stack_heads(x)), 0.0) for t, x in zip(ts, xs)]
        b *= 2
    return ts


def _rwkv_body(p_ref, mu_ref, w0_ref, w2_ref, a0_ref, a2_ref, g2_ref, kk_ref, ka_ref, rk_ref, lnw_ref, lnb_ref,
               o_ref, prev_ref, s_ref, at_s, rt_s, bt_s, kt_s, gc_s, v_s, y_s, wh_s, uh_s, mrb_s):
    nb, tb = p_ref.shape[0], p_ref.shape[1]
    n = nb * tb
    w = o_ref.shape[2]
    npair = w // LANES

    @pl.when(pl.program_id(0) == 0)
    def _():
        prev_ref[...] = jnp.zeros_like(prev_ref)
        s_ref[...] = jnp.zeros_like(s_ref)

    p = _load_rows(p_ref)
    pf = p + mu_ref[...] * (_per_batch(lambda x, prev: _shift_rows(x, prev, 1), p, prev_ref) - p)
    r, k, v = pf[:, 0:w], pf[:, w:2 * w], pf[:, 2 * w:3 * w]
    lowrank = pf[:, 3 * w:3 * w + LANES]
    xg = pf[:, 3 * w + LANES:]
    lw = (-math.exp(-0.5)) * _sigmoid(w0_ref[...] + _dot(jnp.tanh(lowrank), w2_ref[...]))
    a = _sigmoid(a0_ref[...] + _dot(lowrank, a2_ref[...]))
    g = _dot(_sigmoid(xg), g2_ref[...])
    ones = _head_pair_ones(RWKV_HEAD)
    kk = k * kk_ref[...]
    kk = kk * lax.rsqrt(_segsum(kk * kk, ones) + 1e-6)
    k = k * (1.0 + (a - 1.0) * ka_ref[...])
    cs, ce = _chunk_cumsum(lw)
    inv_g = jnp.exp(-cs)
    at_s[...] = kk * jnp.exp(cs - lw)
    rt_s[...] = r * jnp.exp(cs)
    bt_s[...] = -(kk * a) * inv_g
    kt_s[...] = k * inv_g
    gc_s[...] = jnp.exp(ce)
    v_s[...] = v

    tt = _iota((CHUNK, LANES), 0)
    ss = _iota((CHUNK, LANES), 1) % CHUNK
    strict = tt > ss
    incl = tt >= ss
    t2 = _iota((2 * CHUNK, LANES), 0)
    s2 = _iota((2 * CHUNK, LANES), 1) % CHUNK
    strict_incl = jnp.logical_or(t2 % CHUNK > s2, jnp.logical_and(t2 >= CHUNK, t2 % CHUNK == s2))
    diag_blocks = (_iota((LANES, LANES), 0) < RWKV_HEAD) == (_iota((LANES, LANES), 1) < RWKV_HEAD)

    pair_cols = [slice(j * LANES, (j + 1) * LANES) for j in range(npair)]
    all_units = [(slice(c * CHUNK, (c + 1) * CHUNK), cols) for c in range(n // CHUNK) for cols in pair_cols]
    for u0 in range(0, len(all_units), UNIT_GROUP):
        units = all_units[u0:u0 + UNIT_GROUP]
        gms = [_dot_nt(jnp.concatenate([at_s[u], rt_s[u]], axis=0),
                       jnp.concatenate([_stack_heads(bt_s[u]), _stack_heads(kt_s[u])], axis=0)) for u in units]
        for u, gm in zip(units, gms):
            mrb_s[u] = jnp.where(incl, gm[CHUNK:, 0:LANES], 0.0)
        vsts = [_stack_heads(v_s[u]) for u in units]
        avs = [_dot(jnp.where(strict_incl, gm[:, LANES:], 0.0), vst) for gm, vst in zip(gms, vsts)]
        akvs = [av[0:CHUNK] for av in avs]
        for u, av in zip(units, avs):
            y_s[u] = av[CHUNK:]
        tinvs = _unit_lower_inverse([jnp.where(strict, -gm[0:CHUNK, 0:LANES], 0.0) for gm in gms])
        wus = [_dot(t, jnp.concatenate([_stack_heads(at_s[u]), _stack_heads(akv)], axis=1))
               for u, t, akv in zip(units, tinvs, akvs)]
        for u, wu in zip(units, wus):
            wh_s[u] = wu[:, 0:LANES]
            uh_s[u] = wu[:, LANES:]

    def chunk_step(c, carry):
        starts = [pl.multiple_of(c * CHUNK + b * tb, CHUNK) for b in range(nb)]
        chains = [(b * npair + j, pl.ds(starts[b], CHUNK), cols) for b in range(nb) for j, cols in enumerate(pair_cols)]
        ss = [s_ref[si] for si, _, _ in chains]
        xs = [_dot_nt(jnp.concatenate([wh_s[rows, cols], rt_s[rows, cols]], axis=0), s)
              for (_, rows, cols), s in zip(chains, ss)]
        us = [x[0:CHUNK] + uh_s[rows, cols] for (_, rows, cols), x in zip(chains, xs)]
        upds = [_dot_tn(jnp.concatenate([u, v_s[rows, cols]], axis=0),
                        jnp.concatenate([bt_s[rows, cols], kt_s[rows, cols]], axis=0))
                for (_, rows, cols), u in zip(chains, us)]
        ys = [_dot(mrb_s[rows, cols], _stack_heads(u)) for (_, rows, cols), u in zip(chains, us)]
        for (si, rows, cols), s, x, upd, y in zip(chains, ss, xs, upds, ys):
            s_ref[si] = (s + jnp.where(diag_blocks, upd, 0.0)) * gc_s[pl.ds(starts[si // npair], 1), cols]
            y_s[rows, cols] = y_s[rows, cols] + x[CHUNK:] + y
        return carry

    lax.fori_loop(0, tb // CHUNK, chunk_step, 0)

    y = y_s[...]
    inv_head = 1.0 / RWKV_HEAD
    mean = _segsum(y, ones) * inv_head
    yc = y - mean
    var = _segsum(yc * yc, ones) * inv_head
    yn = yc * lax.rsqrt(var + RWKV_LN_EPS) * lnw_ref[...] + lnb_ref[...]
    bonus = _segsum(r * k * rk_ref[...], ones) * v
    _store_rows(o_ref, (yn + bonus) * g)


def rwkv_mix(p, batch, mu, w0, w2, a0, a2, g2, k_k, k_a, r_k, ln_w, ln_b):
    n, cols = p.shape
    w = w0.shape[0]
    t = n // batch
    tb = min(SEQ_BLOCK, t)
    rank = w2.shape[0]
    zeros = jnp.zeros((LANES - rank, w), f32)
    w2p = jnp.concatenate([w2, zeros], axis=0).astype(bf16)
    a2p = jnp.concatenate([zeros, a2], axis=0).astype(bf16)
    row = lambda v: v.reshape(1, -1)
    consts = [row(mu), row(w0), w2p, row(a0), a2p, g2.astype(bf16), row(k_k), row(k_a), row(r_k), row(ln_w),
              row(ln_b)]
    blk = lambda: pltpu.VMEM((batch * tb, w), f32)
    out = pl.pallas_call(
        _rwkv_body,
        grid=(t // tb,),
        in_specs=_seq_specs(batch, tb, [cols]) + [_const_spec(c.shape) for c in consts],
        out_specs=_seq_specs(batch, tb, [w])[0],
        out_shape=jax.ShapeDtypeStruct((batch, t, w), bf16),
        scratch_shapes=[pltpu.VMEM((batch, SUBLANES, cols), f32),
                        pltpu.VMEM((batch * w // LANES, LANES, LANES), f32)] + [blk() for _ in range(10)],
        compiler_params=_cparams(("arbitrary",)),
        name="rwkv_mix",
    )(p.reshape(batch, t, cols), *consts)
    return out.reshape(n, w)


GATE_LANES = LANES


def _silu(x):
    h = 0.5 * x
    return h * jnp.tanh(h) + h


def _expand_gate(gates, first, nheads, terms=2):
    sel = (_iota((GATE_LANES, nheads * LANES), 0) - first == _iota((GATE_LANES, nheads * LANES), 1) // LANES)
    return _dot_split(gates, sel.astype(bf16), terms)


def _gate_row(v, first):
    return jnp.zeros((1, GATE_LANES), f32).at[0, first:first + v.shape[0]].set(v)


def _pair_cols(x_ref, rows, pair):
    c = 2 * pair * LANES
    return x_ref[rows, c:c + LANES], x_ref[rows, c + LANES:c + 2 * LANES]


def _pair_select(x0, x1):
    return jnp.where(_iota(x0.shape, 1) < CHUNK, x0, x1)


def _pair_row_form(x0, x1):
    return jnp.transpose(jnp.concatenate([x0, x1], axis=0))[0:CHUNK]


def _pair_block_rhs(x0, x1):
    return jnp.concatenate([jnp.concatenate([x0, jnp.zeros_like(x0)], axis=1),
                            jnp.concatenate([jnp.zeros_like(x1), x1], axis=1)], axis=0)


def _head_sumsq(x):
    return jnp.concatenate(
        [jnp.broadcast_to(jnp.sum(jnp.square(x[:, j:j + LANES]), axis=-1, keepdims=True), (x.shape[0], LANES))
         for j in range(0, x.shape[1], LANES)], axis=1)


def _head_rms(o, g):
    return o * lax.rsqrt(_head_sumsq(o) * (1.0 / LANES) + NORM_EPS) * g


def _gdn_body(p_ref, z_ref, gate_ref, cw_ref, alog_ref, dtb_ref, ng_ref, o_ref,
              prev_ref, s_ref, q_s, k_s, kb_s, vb_s, qd_s, kbg_s, kd_s, gc_s, gl_s, o_s, u_s, w_s, qk_s):
    nb, tb = p_ref.shape[0], p_ref.shape[1]
    n = nb * tb
    w = o_ref.shape[2]
    nheads = w // LANES

    @pl.when(pl.program_id(0) == 0)
    def _():
        prev_ref[...] = jnp.zeros_like(prev_ref)
        s_ref[...] = jnp.zeros_like(s_ref)

    qkv = _silu(_per_batch(lambda x, prev: _causal_conv(x, prev, cw_ref[...]), _load_rows(p_ref), prev_ref))
    q, k, v = qkv[:, 0:w], qkv[:, w:2 * w], qkv[:, 2 * w:]
    q = q * lax.rsqrt(_head_sumsq(q) + 1e-6) * (LANES ** -0.5)
    k = k * lax.rsqrt(_head_sumsq(k) + 1e-6)
    gates = _load_rows(gate_ref)
    beta = _expand_gate(_sigmoid(gates), 0, nheads)
    gc_c, ge_c = _chunk_cumsum(-jnp.exp(alog_ref[...]) * _softplus(gates + dtb_ref[...]))
    eg = _expand_gate(jnp.exp(gc_c), nheads, nheads)
    kb = k * beta
    q_s[...] = q
    k_s[...] = k
    kb_s[...] = kb
    vb_s[...] = v * beta
    qd_s[...] = q * eg
    kbg_s[...] = kb * eg
    kd_s[...] = k * _expand_gate(jnp.exp(ge_c - gc_c), nheads, nheads)
    gc_s[...] = _expand_gate(gc_c, nheads, nheads, terms=3)
    gl_s[...] = _expand_gate(jnp.exp(ge_c), nheads, nheads)

    tt = _iota((CHUNK, LANES), 0)
    ss = _iota((CHUNK, LANES), 1) % CHUNK
    strict = tt > ss
    incl = tt >= ss

    all_units = [(slice(c * CHUNK, (c + 1) * CHUNK), pair) for c in range(n // CHUNK) for pair in range(nheads // 2)]
    for u0 in range(0, len(all_units), UNIT_GROUP):
        units = all_units[u0:u0 + UNIT_GROUP]
        xs = [_dot_nt(jnp.concatenate(_pair_cols(kb_s, rows, pair) + _pair_cols(q_s, rows, pair), axis=0),
                      jnp.concatenate(_pair_cols(k_s, rows, pair), axis=0)) for rows, pair in units]
        amats = []
        for (rows, pair), x in zip(units, xs):
            g0, g1 = _pair_cols(gc_s, rows, pair)
            dec = jnp.exp(jnp.minimum(_pair_select(g0, g1) - _pair_row_form(g0, g1), 0.0))
            amats.append(jnp.where(strict, _pair_select(x[0:CHUNK], x[CHUNK:2 * CHUNK]) * dec, 0.0))
            qk_s[rows, pair * LANES:(pair + 1) * LANES] = jnp.where(
                incl, _pair_select(x[2 * CHUNK:3 * CHUNK], x[3 * CHUNK:]) * dec, 0.0)
        tinvs = _unit_lower_inverse(amats)
        uws = []
        for (rows, pair), tinv in zip(units, tinvs):
            vb0, vb1 = _pair_cols(vb_s, rows, pair)
            kg0, kg1 = _pair_cols(kbg_s, rows, pair)
            uws.append(_dot(tinv, _pair_block_rhs(jnp.concatenate([vb0, kg0], axis=1),
                                                  jnp.concatenate([vb1, kg1], axis=1))))
        for (rows, pair), uw in zip(units, uws):
            for h in range(2):
                cols = slice((2 * pair + h) * LANES, (2 * pair + h + 1) * LANES)
                u_s[rows, cols] = uw[:, 2 * h * LANES:(2 * h + 1) * LANES]
                w_s[rows, cols] = uw[:, (2 * h + 1) * LANES:(2 * h + 2) * LANES]

    head_cols = [slice(h * LANES, (h + 1) * LANES) for h in range(nheads)]

    def chunk_step(c, carry):
        starts = [pl.multiple_of(c * CHUNK + b * tb, CHUNK) for b in range(nb)]
        chains = [(b, h, pl.ds(starts[b], CHUNK), cols) for b in range(nb) for h, cols in enumerate(head_cols)]
        ss = [s_ref[b * nheads + h] for b, h, _, _ in chains]
        x2s = [_dot(jnp.concatenate([w_s[rows, cols], qd_s[rows, cols]], axis=0), s)
               for (_, _, rows, cols), s in zip(chains, ss)]
        vns = [u_s[rows, cols] - x2[0:CHUNK] for (_, _, rows, cols), x2 in zip(chains, x2s)]
        upds = [_dot_tn(kd_s[rows, cols], vn) for (_, _, rows, cols), vn in zip(chains, vns)]
        os = [_dot(qk_s[pl.ds(starts[b], CHUNK), pair * LANES:(pair + 1) * LANES],
                   _pair_block_rhs(vns[b * nheads + 2 * pair], vns[b * nheads + 2 * pair + 1]))
              for b in range(nb) for pair in range(nheads // 2)]
        for ci, (b, h, rows, cols) in enumerate(chains):
            s_ref[ci] = ss[ci] * gl_s[pl.ds(starts[b], 1), cols] + upds[ci]
            o_s[rows, cols] = x2s[ci][CHUNK:] + os[ci // 2][:, (h % 2) * LANES:(h % 2 + 1) * LANES]
        return carry

    lax.fori_loop(0, tb // CHUNK, chunk_step, 0)
    _store_rows(o_ref, _head_rms(o_s[...], ng_ref[...]) * _silu(_load_rows(z_ref)))


def gdn_mix(p_qkv, p_z, p_gates, batch, conv_w, a_log, dt_bias, norm_g):
    n, cols = p_qkv.shape
    w = p_z.shape[1]
    nheads = w // LANES
    t = n // batch
    tb = min(SEQ_BLOCK, t)
    consts = [conv_w, _gate_row(a_log, nheads), _gate_row(dt_bias, nheads), jnp.tile(norm_g, nheads).reshape(1, w)]
    blk = lambda: pltpu.VMEM((batch * tb, w), f32)
    out = pl.pallas_call(
        _gdn_body,
        grid=(t // tb,),
        in_specs=_seq_specs(batch, tb, [cols, w, GATE_LANES]) + [_const_spec(c.shape) for c in consts],
        out_specs=_seq_specs(batch, tb, [w])[0],
        out_shape=jax.ShapeDtypeStruct((batch, t, w), bf16),
        scratch_shapes=[pltpu.VMEM((batch, SUBLANES, cols), f32), pltpu.VMEM((batch * nheads, LANES, LANES), f32)]
                       + [blk() for _ in range(12)] + [pltpu.VMEM((batch * tb, w // 2), f32)],
        compiler_params=_cparams(("arbitrary",)),
        name="gdn_mix",
    )(p_qkv.reshape(batch, t, cols), p_z.reshape(batch, t, w), p_gates.reshape(batch, t, GATE_LANES), *consts)
    return out.reshape(n, w)


def _cummax_rows(x):
    out = []
    for c in range(x.shape[0] // CHUNK):
        y = x[c * CHUNK:(c + 1) * CHUNK]
        s = 1
        while s < CHUNK:
            y = jnp.maximum(y, _scan_shift(y, s, -jnp.inf))
            s *= 2
        out.append(y)
    return jnp.concatenate(out, axis=0)


def _mlstm_body(pqk_ref, v_ref, og_ref, gate_ref, cw_ref, igb_ref, fgb_ref, ng_ref, o_ref,
                prev_ref, s_ref, m_ref, q_s, k_s, cc_s, mx_s, wi_s, ws_s, ed_s, h_s, v_s, dp_s, intra_s, upd_s):
    nb, tb = pqk_ref.shape[0], pqk_ref.shape[1]
    n = nb * tb
    w = o_ref.shape[2]
    nheads = w // LANES
    hq = pqk_ref.shape[2] // 2

    @pl.when(pl.program_id(0) == 0)
    def _():
        prev_ref[...] = jnp.zeros_like(prev_ref)
        s_ref[...] = jnp.zeros_like(s_ref)
        m_ref[...] = jnp.zeros_like(m_ref)

    qk = _silu(_per_batch(lambda x, prev: _causal_conv(x, prev, cw_ref[...]), _load_rows(pqk_ref), prev_ref))
    q_s[...] = qk[:, 0:hq] * ((hq // nheads) ** -0.5)
    k_s[...] = qk[:, hq:]
    v_s[...] = _load_rows(v_ref)
    gates = _load_rows(gate_ref)
    first = 2 * nheads
    i_pre = gates + igb_ref[...]
    f_pre = pltpu.roll(gates, GATE_LANES - nheads, axis=1) + fgb_ref[...]
    b, b_end = _chunk_cumsum(-_softplus(-f_pre))
    cc = i_pre - b
    cm = _cummax_rows(cc)
    te = b_end + cc
    nc = n // CHUNK
    ncb = tb // CHUNK
    mx, w_inter, wk_scale, dprev = [], [], [], []
    for c in range(nc):
        bi = c // ncb
        m = m_ref[bi * SUBLANES:bi * SUBLANES + 1, :] if c % ncb == 0 else m
        rows = slice(c * CHUNK, (c + 1) * CHUNK)
        b_last = b_end[c * CHUNK:c * CHUNK + 1]
        m_new = jnp.maximum(b_last + m, jnp.max(te[rows], axis=0, keepdims=True))
        mx_c = jnp.maximum(m, cm[rows])
        mx.append(mx_c)
        w_inter.append(jnp.exp(m - mx_c))
        wk_scale.append(jnp.exp(te[rows] - m_new))
        dprev.append(jnp.broadcast_to(jnp.exp(b_last + m - m_new), (SUBLANES, GATE_LANES)))
        m = m_new
        if c % ncb == ncb - 1:
            m_ref[bi * SUBLANES:bi * SUBLANES + 1, :] = m
    mx = jnp.concatenate(mx, axis=0)
    cc_s[...] = _expand_gate(cc, first, nheads, terms=3)
    mx_s[...] = _expand_gate(mx, first, nheads, terms=3)
    wi_s[...] = _expand_gate(jnp.concatenate(w_inter, axis=0), first, nheads)
    ws_s[...] = _expand_gate(jnp.concatenate(wk_scale, axis=0), first, nheads)
    ed_s[...] = jnp.exp(-_expand_gate(b + mx, first, nheads, terms=3))
    dp_s[...] = _expand_gate(jnp.concatenate(dprev, axis=0), first, nheads)

    incl = _iota((CHUNK, LANES), 0) >= _iota((CHUNK, LANES), 1) % CHUNK
    rows128 = _iota((LANES, 4 * LANES), 0) < CHUNK
    lanes512 = _iota((LANES, 4 * LANES), 1) < 2 * LANES
    diag_blocks = rows128 == lanes512
    ones_v = jnp.ones((CHUNK, LANES), f32)
    npair = nheads // 2
    pair_cols = [slice(pair * LANES, (pair + 1) * LANES) for pair in range(npair)]

    def vext(rows, pair):
        v0, v1 = _pair_cols(v_s, rows, pair)
        return jnp.concatenate([v0, ones_v, v1, ones_v], axis=1)

    units = [(c, slice(c * CHUNK, (c + 1) * CHUNK), pair) for c in range(nc) for pair in range(npair)]
    qks = [_dot_nt(q_s[rows, pair_cols[pair]], _stack_heads(k_s[rows, pair_cols[pair]])) for _, rows, pair in units]
    wms = []
    for (_, rows, pair), qk_u in zip(units, qks):
        expo = _pair_row_form(*_pair_cols(cc_s, rows, pair)) - _pair_select(*_pair_cols(mx_s, rows, pair))
        wms.append(jnp.where(incl, jnp.exp(jnp.minimum(expo, 0.0)) * qk_u, 0.0))
    for (_, rows, pair), wm in zip(units, wms):
        ve = vext(rows, pair)
        intra_s[rows, 4 * pair * LANES:4 * (pair + 1) * LANES] = _dot(
            wm, jnp.where(diag_blocks, jnp.concatenate([ve, ve], axis=0), 0.0))
    for c, rows, pair in units:
        wk = k_s[rows, pair_cols[pair]] * _pair_select(*_pair_cols(ws_s, rows, pair))
        upd_s[c, pair] = jnp.where(diag_blocks, _dot_tn(wk, vext(rows, pair)), 0.0)

    def chunk_step(c, carry):
        chains = [(b, pair) for b in range(nb) for pair in range(npair)]
        ss = [s_ref[b * npair + pair] for b, pair in chains]
        inters = [_dot(q_s[pl.ds(pl.multiple_of(c * CHUNK + b * tb, CHUNK), CHUNK), pair_cols[pair]], s)
                  for (b, pair), s in zip(chains, ss)]
        for (b, pair), s, inter in zip(chains, ss, inters):
            g = c + b * ncb
            rows = pl.ds(pl.multiple_of(c * CHUNK + b * tb, CHUNK), CHUNK)
            dp = _pair_cols(dp_s, pl.ds(pl.multiple_of(g * SUBLANES, SUBLANES), 1), pair)
            s_ref[b * npair + pair] = s * jnp.concatenate([dp[0], dp[0], dp[1], dp[1]], axis=1) + upd_s[g, pair]
            for h in range(2):
                cols = slice((2 * pair + h) * LANES, (2 * pair + h + 1) * LANES)
                c0 = 2 * h * LANES
                i0 = 4 * pair * LANES + c0
                wi = wi_s[rows, cols]
                num = wi * inter[:, c0:c0 + LANES] + intra_s[rows, i0:i0 + LANES]
                den = wi * inter[:, c0 + LANES:c0 + 2 * LANES] + intra_s[rows, i0 + LANES:i0 + 2 * LANES]
                h_s[rows, cols] = num / jnp.maximum(jnp.abs(den), ed_s[rows, cols])
        return carry

    lax.fori_loop(0, ncb, chunk_step, 0)
    _store_rows(o_ref, _head_rms(h_s[...], ng_ref[...]) * _sigmoid(_load_rows(og_ref)))


def mlstm_mix(p_qk, p_v, p_og, p_gates, batch, conv_w, ig_b, fg_b, norm_g):
    n, cols = p_qk.shape
    w = p_v.shape[1]
    nheads = w // LANES
    t = n // batch
    tb = min(SEQ_BLOCK, t)
    rows = batch * tb
    consts = [conv_w, _gate_row(ig_b, 2 * nheads), _gate_row(fg_b, 2 * nheads),
              jnp.tile(norm_g, nheads).reshape(1, w)]
    blk = lambda: pltpu.VMEM((rows, w), f32)
    out = pl.pallas_call(
        _mlstm_body,
        grid=(t // tb,),
        in_specs=_seq_specs(batch, tb, [cols, w, w, GATE_LANES]) + [_const_spec(c.shape) for c in consts],
        out_specs=_seq_specs(batch, tb, [w])[0],
        out_shape=jax.ShapeDtypeStruct((batch, t, w), bf16),
        scratch_shapes=[pltpu.VMEM((batch, SUBLANES, cols), f32),
                        pltpu.VMEM((batch * nheads // 2, LANES, 4 * LANES), f32),
                        pltpu.VMEM((batch * SUBLANES, GATE_LANES), f32), pltpu.VMEM((rows, cols // 2), f32),
                        pltpu.VMEM((rows, cols // 2), f32)] + [blk() for _ in range(7)]
                       + [pltpu.VMEM((rows // CHUNK * SUBLANES, w), f32), pltpu.VMEM((rows, 2 * w), f32),
                          pltpu.VMEM((rows // CHUNK, nheads // 2, LANES, 4 * LANES), f32)],
        compiler_params=_cparams(("arbitrary",)),
        name="mlstm_mix",
    )(p_qk.reshape(batch, t, cols), p_v.reshape(batch, t, w), p_og.reshape(batch, t, w),
      p_gates.reshape(batch, t, GATE_LANES), *consts)
    return out.reshape(n, w)


def kernel(x, norm_mix_g, norm_mlp_g, mlp_up, mlp_down, final_g, ev_w_in, ev_w_out, lru_conv_w, lru_conv_b, lru_wa,
           lru_ba, lru_wx, lru_bx, lru_lambda, rwkv_mu, rwkv_w0, rwkv_w2, rwkv_a0, rwkv_a2, rwkv_g2, rwkv_kk,
           rwkv_ka, rwkv_rk, rwkv_lnw, rwkv_lnb, od_w_in, od_w_out, gdn_conv_w, gdn_a_log, gdn_dt_bias, gdn_norm_g,
           mlstm_conv_w, mlstm_ig_b, mlstm_fg_b, mlstm_norm_g):
    batch, seq, d = x.shape
    depth = norm_mix_g.shape[0]
    xs = x.reshape(batch * seq, d)
    ev_w, od_w, ev_wo, od_wo, up_w, down_w = ev_w_in, od_w_in, ev_w_out, od_w_out, mlp_up, mlp_down
    for l in range(depth):
        if l % 2 == 0:
            e = l // 2
            lru_cols = 2 * lru_lambda.shape[1]
            p_lru, p_rwkv = norm_proj(xs, norm_mix_g[l], [(ev_w, _layer_spec(ev_w, e))],
                                      [(lru_cols, ev_w.shape[2] - lru_cols)])
            ya = lru_mix(p_lru, batch, lru_conv_w[e], lru_conv_b[e], lru_wa[e], lru_ba[e], lru_wx[e], lru_bx[e],
                         lru_lambda[e])
            yb = rwkv_mix(p_rwkv, batch, rwkv_mu[e], rwkv_w0[e], rwkv_w2[e], rwkv_a0[e], rwkv_a2[e], rwkv_g2[e],
                          rwkv_kk[e], rwkv_ka[e], rwkv_rk[e], rwkv_lnw[e], rwkv_lnb[e])
            w_out = (ev_wo, e)
        else:
            o = l // 2
            weights, widths = _odd_in_weights(od_w, o, gdn_conv_w.shape[2], gdn_norm_g.shape[1] * gdn_a_log.shape[1],
                                              mlstm_conv_w.shape[2], mlstm_norm_g.shape[1] * mlstm_ig_b.shape[1],
                                              gdn_a_log.shape[1], mlstm_ig_b.shape[1])
            p_qkv, p_z, p_mqk, p_mv, p_mog, p_gates = norm_proj(xs, norm_mix_g[l], weights, widths)
            ya = gdn_mix(p_qkv, p_z, p_gates, batch, gdn_conv_w[o], gdn_a_log[o], gdn_dt_bias[o], gdn_norm_g[o])
            yb = mlstm_mix(p_mqk, p_mv, p_mog, p_gates, batch, mlstm_conv_w[o], mlstm_ig_b[o], mlstm_fg_b[o],
                           mlstm_norm_g[o])
            w_out = (od_wo, o)
        xs = out_mlp(ya, yb, xs, w_out, norm_mlp_g[l], (up_w, l), (down_w, l), final_g, l == depth - 1)
    return xs.reshape(batch, seq, d)


def _odd_in_weights(w_all, layer, gdn_qkv, gdn_w, mlstm_qk, mlstm_w, gdn_heads, mlstm_heads):
    w = w_all[layer]
    lead = gdn_qkv + gdn_w
    m0 = lead + 2 * gdn_heads
    m1 = m0 + mlstm_qk + 2 * mlstm_w
    ngate = 2 * gdn_heads + 2 * mlstm_heads
    gates = jnp.concatenate([w[:, lead:m0], w[:, m1:m1 + 2 * mlstm_heads],
                             jnp.zeros((w.shape[0], GATE_LANES - ngate), w.dtype)], axis=1)
    weights = [(part, _const_spec(part.shape)) for part in (w[:, 0:lead], w[:, m0:m1], gates)]
    return weights, [(gdn_qkv, gdn_w), (mlstm_qk, mlstm_w, mlstm_w), (GATE_LANES,)]
```

```python
import functools
import math

import jax
import jax.numpy as jnp
from jax import lax
from jax.experimental import pallas as pl
from jax.experimental.pallas import tpu as pltpu

f32 = jnp.float32
bf16 = jnp.bfloat16

LANES = 128
SUBLANES = 8
VMEM_LIMIT = 56 * 1024 * 1024

NORM_EPS = 1e-6
CONV_W = 4
LRU_C = 8.0
RWKV_HEAD = 64
RWKV_LN_EPS = 64e-5
CHUNK = 64

ROW_BLOCK = 512
SEQ_BLOCK = 256
LRU_BLOCK = 128


def _cparams(sem):
    return pltpu.CompilerParams(dimension_semantics=sem, vmem_limit_bytes=VMEM_LIMIT)


def _const_spec(shape):
    nd = len(shape)
    return pl.BlockSpec(shape, lambda *_: (0,) * nd)


def _layer_spec(w, layer, cols=None):
    return pl.BlockSpec((None, w.shape[1], w.shape[2] if cols is None else cols), lambda *_: (layer, 0, 0))


def _rms(x, g):
    return x * lax.rsqrt(jnp.mean(x * x, axis=-1, keepdims=True) + NORM_EPS) * g


def _softplus(x):
    return jnp.maximum(x, 0.0) + jnp.log(1.0 + jnp.exp(-jnp.abs(x)))


def _sigmoid(x):
    return 0.5 * jnp.tanh(0.5 * x) + 0.5


def _norm_proj_body(widths, x_ref, g_ref, *refs):
    w_refs, o_refs = refs[:len(widths)], list(refs[len(widths):])
    h = _rms(x_ref[...], g_ref[...]).astype(bf16)
    for w_ref, group in zip(w_refs, widths):
        c0 = 0
        for wd in group:
            o_ref = o_refs.pop(0)
            w = w_ref[:, c0:c0 + wd].astype(bf16)
            o_ref[...] = jnp.dot(h, w, preferred_element_type=f32).astype(o_ref.dtype)
            c0 += wd


def norm_proj(x, g, weights, widths):
    n, d = x.shape
    tm = min(ROW_BLOCK, n)
    flat = [wd for group in widths for wd in group]
    return pl.pallas_call(
        functools.partial(_norm_proj_body, widths),
        grid=(n // tm,),
        in_specs=[pl.BlockSpec((tm, d), lambda i: (i, 0)), _const_spec((1, d))] + [spec for _, spec in weights],
        out_specs=[pl.BlockSpec((tm, wd), lambda i: (i, 0)) for wd in flat],
        out_shape=[jax.ShapeDtypeStruct((n, wd), f32) for wd in flat],
        compiler_params=_cparams(("parallel",)),
        name="norm_proj",
    )(x, g.reshape(1, d), *[w for w, _ in weights])


FF_BLOCK = 1024


def _out_mlp_body(final, ya_ref, yb_ref, x_ref, wo_ref, g_ref, wu_ref, wd_ref, gf_ref, o_ref):
    half = ya_ref.shape[1]
    y = jnp.dot(ya_ref[...], wo_ref[0:half, :].astype(bf16), preferred_element_type=f32)
    y = y + jnp.dot(yb_ref[...], wo_ref[half:, :].astype(bf16), preferred_element_type=f32)
    x1 = x_ref[...] + y
    h = _rms(x1, g_ref[...]).astype(bf16)
    acc = x1
    for c in range(wu_ref.shape[1] // FF_BLOCK):
        u = jnp.dot(h, wu_ref[:, c * FF_BLOCK:(c + 1) * FF_BLOCK].astype(bf16), preferred_element_type=f32)
        u = jnp.square(jnp.maximum(u, 0.0)).astype(bf16)
        acc = acc + jnp.dot(u, wd_ref[c * FF_BLOCK:(c + 1) * FF_BLOCK, :].astype(bf16), preferred_element_type=f32)
    if final:
        acc = _rms(acc, gf_ref[...])
    o_ref[...] = acc


def out_mlp(ya, yb, x, wo, g, wu, wd, gf, final):
    n, d = x.shape
    half = ya.shape[1]
    tm = min(ROW_BLOCK, n)
    row = lambda i: (i, 0)
    return pl.pallas_call(
        functools.partial(_out_mlp_body, final),
        grid=(n // tm,),
        in_specs=[pl.BlockSpec((tm, half), row), pl.BlockSpec((tm, half), row), pl.BlockSpec((tm, d), row),
                  _layer_spec(*wo), _const_spec((1, d)), _layer_spec(*wu), _layer_spec(*wd), _const_spec((1, d))],
        out_specs=pl.BlockSpec((tm, d), row),
        out_shape=jax.ShapeDtypeStruct((n, d), f32),
        compiler_params=_cparams(("parallel",)),
        name="out_mlp",
    )(ya, yb, x, wo[0], g.reshape(1, d), wu[0], wd[0], gf.reshape(1, d))


def _seq_specs(batch, tb, widths):
    return [pl.BlockSpec((batch, tb, wd), lambda i: (0, i, 0)) for wd in widths]


def _load_rows(ref):
    return jnp.concatenate([ref[b] for b in range(ref.shape[0])], axis=0)


def _store_rows(ref, x):
    tb = ref.shape[1]
    for b in range(ref.shape[0]):
        ref[b] = x[b * tb:(b + 1) * tb].astype(ref.dtype)


def _per_batch(fn, x, prev_ref):
    nb = prev_ref.shape[0]
    tb = x.shape[0] // nb
    out = []
    for b in range(nb):
        xb = x[b * tb:(b + 1) * tb]
        out.append(fn(xb, prev_ref[b]))
        prev_ref[b] = xb[tb - SUBLANES:]
    return jnp.concatenate(out, axis=0)


def _shift_rows(x, prev, k):
    rolled = pltpu.roll(x, k, axis=0)
    head = pltpu.roll(prev, k, axis=0)
    rows = lax.broadcasted_iota(jnp.int32, (SUBLANES, x.shape[1]), 0)
    top = jnp.where(rows < k, head, rolled[0:SUBLANES])
    return jnp.concatenate([top, rolled[SUBLANES:]], axis=0)


def _causal_conv(x, prev, w):
    y = x * w[CONV_W - 1:CONV_W]
    for k in range(1, CONV_W):
        y = y + _shift_rows(x, prev, k) * w[CONV_W - 1 - k:CONV_W - k]
    return y


def _scan_shift(x, s, fill):
    n, c = x.shape
    if s % SUBLANES == 0:
        return jnp.concatenate([jnp.full((s, c), fill, x.dtype), x[:n - s]], axis=0)
    rolled = pltpu.roll(x, s, axis=0)
    rows = lax.broadcasted_iota(jnp.int32, (SUBLANES, c), 0)
    top = jnp.where(rows < s, fill, rolled[0:SUBLANES])
    return jnp.concatenate([top, rolled[SUBLANES:]], axis=0)


def _linear_scan(a, u):
    s = 1
    while s < a.shape[0]:
        u = a * _scan_shift(u, s, 0.0) + u
        a = a * _scan_shift(a, s, 1.0)
        s *= 2
    return a, u


def _lru_body(p_ref, cw_ref, cb_ref, wg_ref, bg_ref, lam_ref, o_ref, xprev_ref, h_ref):
    nb, tb = p_ref.shape[0], p_ref.shape[1]
    w = o_ref.shape[2]
    step = pl.program_id(0)

    @pl.when(step == 0)
    def _():
        xprev_ref[...] = jnp.zeros_like(xprev_ref)
        h_ref[...] = jnp.zeros_like(h_ref)

    gate_in = jnp.concatenate([p_ref[b, :, 0:w] for b in range(nb)], axis=0)
    rec = jnp.concatenate([p_ref[b, :, w:] for b in range(nb)], axis=0)
    xc = _per_batch(lambda x, prev: _causal_conv(x, prev, cw_ref[...]), rec, xprev_ref) + cb_ref[...]
    gates = jnp.dot(xc.astype(bf16), wg_ref[...], preferred_element_type=f32) + bg_ref[...]
    r_gate = _sigmoid(gates[:, 0:w])
    i_gate = _sigmoid(gates[:, w:])
    log_a = (-LRU_C) * r_gate * _softplus(-lam_ref[...])
    a = jnp.exp(log_a)
    mult = jnp.sqrt(1.0 - jnp.exp(2.0 * log_a))
    rows = lax.broadcasted_iota(jnp.int32, (nb * tb, w), 0)
    mult = jnp.where(jnp.logical_and(rows % tb == 0, step == 0), 1.0, mult)
    u = i_gate * xc * mult
    hs = []
    for b in range(nb):
        a_cum, h = _linear_scan(a[b * tb:(b + 1) * tb], u[b * tb:(b + 1) * tb])
        h = h + a_cum * h_ref[b * SUBLANES:b * SUBLANES + 1, :]
        h_ref[b * SUBLANES:b * SUBLANES + 1, :] = h[tb - 1:tb]
        hs.append(h)
    _store_rows(o_ref, jnp.concatenate(hs, axis=0) * jax.nn.gelu(gate_in, approximate=True))


def _block_diag(wb):
    nb, d, e = wb.shape
    eye = jnp.eye(nb, dtype=wb.dtype)
    return (wb[:, :, None, :] * eye[:, None, :, None]).reshape(nb * d, nb * e)


def lru_mix(p, batch, conv_w, conv_b, wa, ba, wx, bx, lam):
    n, c2 = p.shape
    w = c2 // 2
    t = n // batch
    tb = min(LRU_BLOCK, t)
    wg = jnp.concatenate([_block_diag(wa), _block_diag(wx)], axis=1).astype(bf16)
    bg = jnp.concatenate([ba, bx]).reshape(1, 2 * w)
    out = pl.pallas_call(
        _lru_body,
        grid=(t // tb,),
        in_specs=_seq_specs(batch, tb, [c2]) + [_const_spec((CONV_W, w)), _const_spec((1, w)),
                                                _const_spec((w, 2 * w)), _const_spec((1, 2 * w)), _const_spec((1, w))],
        out_specs=_seq_specs(batch, tb, [w])[0],
        out_shape=jax.ShapeDtypeStruct((batch, t, w), bf16),
        scratch_shapes=[pltpu.VMEM((batch, SUBLANES, w), f32), pltpu.VMEM((batch * SUBLANES, w), f32)],
        compiler_params=_cparams(("arbitrary",)),
        name="lru_mix",
    )(p.reshape(batch, t, c2), conv_w, conv_b.reshape(1, w), wg, bg, lam.reshape(1, w))
    return out.reshape(n, w)


UNIT_GROUP = 32


def _dot(a, b):
    return jnp.dot(a.astype(bf16), b.astype(bf16), preferred_element_type=f32)


def _dot_nt(a, b):
    return lax.dot_general(a.astype(bf16), b.astype(bf16), (((1,), (1,)), ((), ())), preferred_element_type=f32)


def _dot_tn(a, b):
    return lax.dot_general(a.astype(bf16), b.astype(bf16), (((0,), (0,)), ((), ())), preferred_element_type=f32)


def _dot_split(x, e, terms=2):
    out = None
    for _ in range(terms):
        hi = x.astype(bf16)
        x = x - hi.astype(f32)
        part = jnp.dot(hi, e, preferred_element_type=f32)
        out = part if out is None else out + part
    return out


def _iota(shape, axis):
    return lax.broadcasted_iota(jnp.int32, shape, axis)


def _tri_incl():
    return (_iota((CHUNK, CHUNK), 0) >= _iota((CHUNK, CHUNK), 1)).astype(bf16)


def _chunk_cumsum(x):
    tri = _tri_incl()
    cs, ce = [], []
    for c in range(x.shape[0] // CHUNK):
        rest = x[c * CHUNK:(c + 1) * CHUNK]
        cs_c = None
        for _ in range(3):
            hi = rest.astype(bf16)
            rest = rest - hi.astype(f32)
            part = jnp.dot(tri, hi, preferred_element_type=f32)
            cs_c = part if cs_c is None else cs_c + part
        cs.append(cs_c)
        ce.append(jnp.broadcast_to(cs_c[CHUNK - 1:CHUNK], cs_c.shape))
    return jnp.concatenate(cs, axis=0), jnp.concatenate(ce, axis=0)


def _head_pair_ones(head):
    return (_iota((LANES, LANES), 0) // head == _iota((LANES, LANES), 1) // head).astype(bf16)


def _segsum(x, e):
    return jnp.concatenate([_dot_split(x[:, j:j + LANES], e) for j in range(0, x.shape[1], LANES)], axis=1)


def _stack_heads(x):
    lo = _iota(x.shape, 1) < RWKV_HEAD
    return jnp.concatenate([jnp.where(lo, x, 0.0), jnp.where(lo, 0.0, x)], axis=0)


def _unit_lower_inverse(amats):
    shape = amats[0].shape
    ii = _iota(shape, 0)
    jj = _iota(shape, 1) % CHUNK
    eye = (ii == jj).astype(f32)
    first = jnp.logical_and(ii // 2 == jj // 2, ii > jj)
    ts = [eye - jnp.where(first, a, 0.0) for a in amats]
    abf = [a.astype(bf16) for a in amats]
    b = 2
    while b < CHUNK:
        off = jnp.logical_and(ii // (2 * b) == jj // (2 * b), ii // b > jj // b)
        xs = [_dot(a, _stack_heads(t)) for a, t in zip(abf, ts)]
        ts = [t - jnp.where(off, _dot(t, _stack_heads(x)), 0.0) for t, x in zip(ts, xs)]
        b *= 2
    return ts


def _rwkv_body(p_ref, mu_ref, w0_ref, w2_ref, a0_ref, a2_ref, g2_ref, kk_ref, ka_ref, rk_ref, lnw_ref, lnb_ref,
               o_ref, prev_ref, s_ref, at_s, rt_s, bt_s, kt_s, gc_s, v_s, y_s, wh_s, uh_s, mrb_s):
    nb, tb = p_ref.shape[0], p_ref.shape[1]
    n = nb * tb
    w = o_ref.shape[2]
    npair = w // LANES

    @pl.when(pl.program_id(0) == 0)
    def _():
        prev_ref[...] = jnp.zeros_like(prev_ref)
        s_ref[...] = jnp.zeros_like(s_ref)

    p = _load_rows(p_ref)
    pf = p + mu_ref[...] * (_per_batch(lambda x, prev: _shift_rows(x, prev, 1), p, prev_ref) - p)
    r, k, v = pf[:, 0:w], pf[:, w:2 * w], pf[:, 2 * w:3 * w]
    lowrank = pf[:, 3 * w:3 * w + LANES]
    xg = pf[:, 3 * w + LANES:]
    lw = (-math.exp(-0.5)) * _sigmoid(w0_ref[...] + _dot(jnp.tanh(lowrank), w2_ref[...]))
    a = _sigmoid(a0_ref[...] + _dot(lowrank, a2_ref[...]))
    g = _dot(_sigmoid(xg), g2_ref[...])
    ones = _head_pair_ones(RWKV_HEAD)
    kk = k * kk_ref[...]
    kk = kk * lax.rsqrt(_segsum(kk * kk, ones) + 1e-6)
    k = k * (1.0 + (a - 1.0) * ka_ref[...])
    cs, ce = _chunk_cumsum(lw)
    inv_g = jnp.exp(-cs)
    at_s[...] = kk * jnp.exp(cs - lw)
    rt_s[...] = r * jnp.exp(cs)
    bt_s[...] = -(kk * a) * inv_g
    kt_s[...] = k * inv_g
    gc_s[...] = jnp.exp(ce)
    v_s[...] = v

    tt = _iota((CHUNK, LANES), 0)
    ss = _iota((CHUNK, LANES), 1) % CHUNK
    strict = tt > ss
    incl = tt >= ss
    t2 = _iota((2 * CHUNK, LANES), 0)
    s2 = _iota((2 * CHUNK, LANES), 1) % CHUNK
    strict_incl = jnp.logical_or(t2 % CHUNK > s2, jnp.logical_and(t2 >= CHUNK, t2 % CHUNK == s2))
    diag_blocks = (_iota((LANES, LANES), 0) < RWKV_HEAD) == (_iota((LANES, LANES), 1) < RWKV_HEAD)

    pair_cols = [slice(j * LANES, (j + 1) * LANES) for j in range(npair)]
    all_units = [(slice(c * CHUNK, (c + 1) * CHUNK), cols) for c in range(n // CHUNK) for cols in pair_cols]
    for u0 in range(0, len(all_units), UNIT_GROUP):
        units = all_units[u0:u0 + UNIT_GROUP]
        gms = [_dot_nt(jnp.concatenate([at_s[u], rt_s[u]], axis=0),
                       jnp.concatenate([_stack_heads(bt_s[u]), _stack_heads(kt_s[u])], axis=0)) for u in units]
        for u, gm in zip(units, gms):
            mrb_s[u] = jnp.where(incl, gm[CHUNK:, 0:LANES], 0.0)
        vsts = [_stack_heads(v_s[u]) for u in units]
        avs = [_dot(jnp.where(strict_incl, gm[:, LANES:], 0.0), vst) for gm, vst in zip(gms, vsts)]
        akvs = [av[0:CHUNK] for av in avs]
        for u, av in zip(units, avs):
            y_s[u] = av[CHUNK:]
        tinvs = _unit_lower_inverse([jnp.where(strict, -gm[0:CHUNK, 0:LANES], 0.0) for gm in gms])
        wus = [_dot(t, jnp.concatenate([_stack_heads(at_s[u]), _stack_heads(akv)], axis=1))
               for u, t, akv in zip(units, tinvs, akvs)]
        for u, wu in zip(units, wus):
            wh_s[u] = wu[:, 0:LANES]
            uh_s[u] = wu[:, LANES:]

    def chunk_step(c, carry):
        starts = [pl.multiple_of(c * CHUNK + b * tb, CHUNK) for b in range(nb)]
        chains = [(b * npair + j, pl.ds(starts[b], CHUNK), cols) for b in range(nb) for j, cols in enumerate(pair_cols)]
        ss = [s_ref[si] for si, _, _ in chains]
        xs = [_dot_nt(jnp.concatenate([wh_s[rows, cols], rt_s[rows, cols]], axis=0), s)
              for (_, rows, cols), s in zip(chains, ss)]
        us = [x[0:CHUNK] + uh_s[rows, cols] for (_, rows, cols), x in zip(chains, xs)]
        upds = [_dot_tn(jnp.concatenate([u, v_s[rows, cols]], axis=0),
                        jnp.concatenate([bt_s[rows, cols], kt_s[rows, cols]], axis=0))
                for (_, rows, cols), u in zip(chains, us)]
        ys = [_dot(mrb_s[rows, cols], _stack_heads(u)) for (_, rows, cols), u in zip(chains, us)]
        for (si, rows, cols), s, x, upd, y in zip(chains, ss, xs, upds, ys):
            s_ref[si] = (s + jnp.where(diag_blocks, upd, 0.0)) * gc_s[pl.ds(starts[si // npair], 1), cols]
            y_s[rows, cols] = y_s[rows, cols] + x[CHUNK:] + y
        return carry

    lax.fori_loop(0, tb // CHUNK, chunk_step, 0)

    y = y_s[...]
    inv_head = 1.0 / RWKV_HEAD
    mean = _segsum(y, ones) * inv_head
    yc = y - mean
    var = _segsum(yc * yc, ones) * inv_head
    yn = yc * lax.rsqrt(var + RWKV_LN_EPS) * lnw_ref[...] + lnb_ref[...]
    bonus = _segsum(r * k * rk_ref[...], ones) * v
    _store_rows(o_ref, (yn + bonus) * g)


def rwkv_mix(p, batch, mu, w0, w2, a0, a2, g2, k_k, k_a, r_k, ln_w, ln_b):
    n, cols = p.shape
    w = w0.shape[0]
    t = n // batch
    tb = min(SEQ_BLOCK, t)
    rank = w2.shape[0]
    zeros = jnp.zeros((LANES - rank, w), f32)
    w2p = jnp.concatenate([w2, zeros], axis=0).astype(bf16)
    a2p = jnp.concatenate([zeros, a2], axis=0).astype(bf16)
    row = lambda v: v.reshape(1, -1)
    consts = [row(mu), row(w0), w2p, row(a0), a2p, g2.astype(bf16), row(k_k), row(k_a), row(r_k), row(ln_w),
              row(ln_b)]
    blk = lambda: pltpu.VMEM((batch * tb, w), f32)
    out = pl.pallas_call(
        _rwkv_body,
        grid=(t // tb,),
        in_specs=_seq_specs(batch, tb, [cols]) + [_const_spec(c.shape) for c in consts],
        out_specs=_seq_specs(batch, tb, [w])[0],
        out_shape=jax.ShapeDtypeStruct((batch, t, w), bf16),
        scratch_shapes=[pltpu.VMEM((batch, SUBLANES, cols), f32),
                        pltpu.VMEM((batch * w // LANES, LANES, LANES), f32)] + [blk() for _ in range(10)],
        compiler_params=_cparams(("arbitrary",)),
        name="rwkv_mix",
    )(p.reshape(batch, t, cols), *consts)
    return out.reshape(n, w)


GATE_LANES = LANES


def _silu(x):
    h = 0.5 * x
    return h * jnp.tanh(h) + h


def _expand_gate(gates, first, nheads, terms=2):
    sel = (_iota((GATE_LANES, nheads * LANES), 0) - first == _iota((GATE_LANES, nheads * LANES), 1) // LANES)
    return _dot_split(gates, sel.astype(bf16), terms)


def _gate_row(v, first):
    return jnp.zeros((1, GATE_LANES), f32).at[0, first:first + v.shape[0]].set(v)


def _pair_cols(x_ref, rows, pair):
    c = 2 * pair * LANES
    return x_ref[rows, c:c + LANES], x_ref[rows, c + LANES:c + 2 * LANES]


def _pair_select(x0, x1):
    return jnp.where(_iota(x0.shape, 1) < CHUNK, x0, x1)


def _pair_row_form(x0, x1):
    return jnp.transpose(jnp.concatenate([x0, x1], axis=0))[0:CHUNK]


def _pair_block_rhs(x0, x1):
    return jnp.concatenate([jnp.concatenate([x0, jnp.zeros_like(x0)], axis=1),
                            jnp.concatenate([jnp.zeros_like(x1), x1], axis=1)], axis=0)


def _head_sumsq(x):
    return jnp.concatenate(
        [jnp.broadcast_to(jnp.sum(jnp.square(x[:, j:j + LANES]), axis=-1, keepdims=True), (x.shape[0], LANES))
         for j in range(0, x.shape[1], LANES)], axis=1)


def _head_rms(o, g):
    return o * lax.rsqrt(_head_sumsq(o) * (1.0 / LANES) + NORM_EPS) * g


def _gdn_body(p_ref, z_ref, gate_ref, cw_ref, alog_ref, dtb_ref, ng_ref, o_ref,
              prev_ref, s_ref, q_s, k_s, kb_s, vb_s, qd_s, kbg_s, kd_s, gc_s, gl_s, o_s, u_s, w_s, qk_s):
    nb, tb = p_ref.shape[0], p_ref.shape[1]
    n = nb * tb
    w = o_ref.shape[2]
    nheads = w // LANES

    @pl.when(pl.program_id(0) == 0)
    def _():
        prev_ref[...] = jnp.zeros_like(prev_ref)
        s_ref[...] = jnp.zeros_like(s_ref)

    qkv = _silu(_per_batch(lambda x, prev: _causal_conv(x, prev, cw_ref[...]), _load_rows(p_ref), prev_ref))
    q, k, v = qkv[:, 0:w], qkv[:, w:2 * w], qkv[:, 2 * w:]
    q = q * lax.rsqrt(_head_sumsq(q) + 1e-6) * (LANES ** -0.5)
    k = k * lax.rsqrt(_head_sumsq(k) + 1e-6)
    gates = _load_rows(gate_ref)
    beta = _expand_gate(_sigmoid(gates), 0, nheads)
    gc_c, ge_c = _chunk_cumsum(-jnp.exp(alog_ref[...]) * _softplus(gates + dtb_ref[...]))
    eg = _expand_gate(jnp.exp(gc_c), nheads, nheads)
    kb = k * beta
    q_s[...] = q
    k_s[...] = k
    kb_s[...] = kb
    vb_s[...] = v * beta
    qd_s[...] = q * eg
    kbg_s[...] = kb * eg
    kd_s[...] = k * _expand_gate(jnp.exp(ge_c - gc_c), nheads, nheads)
    gc_s[...] = _expand_gate(gc_c, nheads, nheads, terms=3)
    gl_s[...] = _expand_gate(jnp.exp(ge_c), nheads, nheads)

    tt = _iota((CHUNK, LANES), 0)
    ss = _iota((CHUNK, LANES), 1) % CHUNK
    strict = tt > ss
    incl = tt >= ss

    all_units = [(slice(c * CHUNK, (c + 1) * CHUNK), pair) for c in range(n // CHUNK) for pair in range(nheads // 2)]
    for u0 in range(0, len(all_units), UNIT_GROUP):
        units = all_units[u0:u0 + UNIT_GROUP]
        xs = [_dot_nt(jnp.concatenate(_pair_cols(kb_s, rows, pair) + _pair_cols(q_s, rows, pair), axis=0),
                      jnp.concatenate(_pair_cols(k_s, rows, pair), axis=0)) for rows, pair in units]
        amats = []
        for (rows, pair), x in zip(units, xs):
            g0, g1 = _pair_cols(gc_s, rows, pair)
            dec = jnp.exp(jnp.minimum(_pair_select(g0, g1) - _pair_row_form(g0, g1), 0.0))
            amats.append(jnp.where(strict, _pair_select(x[0:CHUNK], x[CHUNK:2 * CHUNK]) * dec, 0.0))
            qk_s[rows, pair * LANES:(pair + 1) * LANES] = jnp.where(
                incl, _pair_select(x[2 * CHUNK:3 * CHUNK], x[3 * CHUNK:]) * dec, 0.0)
        tinvs = _unit_lower_inverse(amats)
        uws = []
        for (rows, pair), tinv in zip(units, tinvs):
            vb0, vb1 = _pair_cols(vb_s, rows, pair)
            kg0, kg1 = _pair_cols(kbg_s, rows, pair)
            uws.append(_dot(tinv, _pair_block_rhs(jnp.concatenate([vb0, kg0], axis=1),
                                                  jnp.concatenate([vb1, kg1], axis=1))))
        for (rows, pair), uw in zip(units, uws):
            for h in range(2):
                cols = slice((2 * pair + h) * LANES, (2 * pair + h + 1) * LANES)
                u_s[rows, cols] = uw[:, 2 * h * LANES:(2 * h + 1) * LANES]
                w_s[rows, cols] = uw[:, (2 * h + 1) * LANES:(2 * h + 2) * LANES]

    head_cols = [slice(h * LANES, (h + 1) * LANES) for h in range(nheads)]

    def chunk_step(c, carry):
        starts = [pl.multiple_of(c * CHUNK + b * tb, CHUNK) for b in range(nb)]
        chains = [(b, h, pl.ds(starts[b], CHUNK), cols) for b in range(nb) for h, cols in enumerate(head_cols)]
        ss = [s_ref[b * nheads + h] for b, h, _, _ in chains]
        x2s = [_dot(jnp.concatenate([w_s[rows, cols], qd_s[rows, cols]], axis=0), s)
               for (_, _, rows, cols), s in zip(chains, ss)]
        vns = [u_s[rows, cols] - x2[0:CHUNK] for (_, _, rows, cols), x2 in zip(chains, x2s)]
        upds = [_dot_tn(kd_s[rows, cols], vn) for (_, _, rows, cols), vn in zip(chains, vns)]
        os = [_dot(qk_s[pl.ds(starts[b], CHUNK), pair * LANES:(pair + 1) * LANES],
                   _pair_block_rhs(vns[b * nheads + 2 * pair], vns[b * nheads + 2 * pair + 1]))
              for b in range(nb) for pair in range(nheads // 2)]
        for ci, (b, h, rows, cols) in enumerate(chains):
            s_ref[ci] = ss[ci] * gl_s[pl.ds(starts[b], 1), cols] + upds[ci]
            o_s[rows, cols] = x2s[ci][CHUNK:] + os[ci // 2][:, (h % 2) * LANES:(h % 2 + 1) * LANES]
        return carry

    lax.fori_loop(0, tb // CHUNK, chunk_step, 0)
    _store_rows(o_ref, _head_rms(o_s[...], ng_ref[...]) * _silu(_load_rows(z_ref)))


def gdn_mix(p_qkv, p_z, p_gates, batch, conv_w, a_log, dt_bias, norm_g):
    n, cols = p_qkv.shape
    w = p_z.shape[1]
    nheads = w // LANES
    t = n // batch
    tb = min(SEQ_BLOCK, t)
    consts = [conv_w, _gate_row(a_log, nheads), _gate_row(dt_bias, nheads), jnp.tile(norm_g, nheads).reshape(1, w)]
    blk = lambda: pltpu.VMEM((batch * tb, w), f32)
    out = pl.pallas_call(
        _gdn_body,
        grid=(t // tb,),
        in_specs=_seq_specs(batch, tb, [cols, w, GATE_LANES]) + [_const_spec(c.shape) for c in consts],
        out_specs=_seq_specs(batch, tb, [w])[0],
        out_shape=jax.ShapeDtypeStruct((batch, t, w), bf16),
        scratch_shapes=[pltpu.VMEM((batch, SUBLANES, cols), f32), pltpu.VMEM((batch * nheads, LANES, LANES), f32)]
                       + [blk() for _ in range(12)] + [pltpu.VMEM((batch * tb, w // 2), f32)],
        compiler_params=_cparams(("arbitrary",)),
        name="gdn_mix",
    )(p_qkv.reshape(batch, t, cols), p_z.reshape(batch, t, w), p_gates.reshape(batch, t, GATE_LANES), *consts)
    return out.reshape(n, w)


def _cummax_rows(x):
    out = []
    for c in range(x.shape[0] // CHUNK):
        y = x[c * CHUNK:(c + 1) * CHUNK]
        s = 1
        while s < CHUNK:
            y = jnp.maximum(y, _scan_shift(y, s, -jnp.inf))
            s *= 2
        out.append(y)
    return jnp.concatenate(out, axis=0)


def _mlstm_body(pqk_ref, v_ref, og_ref, gate_ref, cw_ref, igb_ref, fgb_ref, ng_ref, o_ref,
                prev_ref, s_ref, m_ref, q_s, k_s, cc_s, mx_s, wi_s, ws_s, ed_s, h_s, v_s, dp_s, intra_s, upd_s):
    nb, tb = pqk_ref.shape[0], pqk_ref.shape[1]
    n = nb * tb
    w = o_ref.shape[2]
    nheads = w // LANES
    hq = pqk_ref.shape[2] // 2

    @pl.when(pl.program_id(0) == 0)
    def _():
        prev_ref[...] = jnp.zeros_like(prev_ref)
        s_ref[...] = jnp.zeros_like(s_ref)
        m_ref[...] = jnp.zeros_like(m_ref)

    qk = _silu(_per_batch(lambda x, prev: _causal_conv(x, prev, cw_ref[...]), _load_rows(pqk_ref), prev_ref))
    q_s[...] = qk[:, 0:hq] * ((hq // nheads) ** -0.5)
    k_s[...] = qk[:, hq:]
    v_s[...] = _load_rows(v_ref)
    gates = _load_rows(gate_ref)
    first = 2 * nheads
    i_pre = gates + igb_ref[...]
    f_pre = pltpu.roll(gates, GATE_LANES - nheads, axis=1) + fgb_ref[...]
    b, b_end = _chunk_cumsum(-_softplus(-f_pre))
    cc = i_pre - b
    cm = _cummax_rows(cc)
    te = b_end + cc
    nc = n // CHUNK
    ncb = tb // CHUNK
    mx, w_inter, wk_scale, dprev = [], [], [], []
    for c in range(nc):
        bi = c // ncb
        m = m_ref[bi * SUBLANES:bi * SUBLANES + 1, :] if c % ncb == 0 else m
        rows = slice(c * CHUNK, (c + 1) * CHUNK)
        b_last = b_end[c * CHUNK:c * CHUNK + 1]
        m_new = jnp.maximum(b_last + m, jnp.max(te[rows], axis=0, keepdims=True))
        mx_c = jnp.maximum(m, cm[rows])
        mx.append(mx_c)
        w_inter.append(jnp.exp(m - mx_c))
        wk_scale.append(jnp.exp(te[rows] - m_new))
        dprev.append(jnp.broadcast_to(jnp.exp(b_last + m - m_new), (SUBLANES, GATE_LANES)))
        m = m_new
        if c % ncb == ncb - 1:
            m_ref[bi * SUBLANES:bi * SUBLANES + 1, :] = m
    mx = jnp.concatenate(mx, axis=0)
    cc_s[...] = _expand_gate(cc, first, nheads, terms=3)
    mx_s[...] = _expand_gate(mx, first, nheads, terms=3)
    wi_s[...] = _expand_gate(jnp.concatenate(w_inter, axis=0), first, nheads)
    ws_s[...] = _expand_gate(jnp.concatenate(wk_scale, axis=0), first, nheads)
    ed_s[...] = jnp.exp(-_expand_gate(b + mx, first, nheads, terms=3))
    dp_s[...] = _expand_gate(jnp.concatenate(dprev, axis=0), first, nheads)

    incl = _iota((CHUNK, LANES), 0) >= _iota((CHUNK, LANES), 1) % CHUNK
    row_h1 = _iota((LANES, 3 * LANES), 0) >= CHUNK
    lane3 = _iota((LANES, 3 * LANES), 1)
    lane_h1 = jnp.logical_or(jnp.logical_and(lane3 >= LANES, lane3 < 2 * LANES), lane3 >= 2 * LANES + CHUNK)
    diag_blocks = row_h1 == lane_h1
    ones_v = jnp.ones((CHUNK, LANES), f32)
    npair = nheads // 2
    pair_cols = [slice(pair * LANES, (pair + 1) * LANES) for pair in range(npair)]

    def vext(rows, pair):
        v0, v1 = _pair_cols(v_s, rows, pair)
        return jnp.concatenate([v0, v1, ones_v], axis=1)

    units = [(c, slice(c * CHUNK, (c + 1) * CHUNK), pair) for c in range(nc) for pair in range(npair)]
    qks = [_dot_nt(q_s[rows, pair_cols[pair]], _stack_heads(k_s[rows, pair_cols[pair]])) for _, rows, pair in units]
    wms = []
    for (_, rows, pair), qk_u in zip(units, qks):
        expo = _pair_row_form(*_pair_cols(cc_s, rows, pair)) - _pair_select(*_pair_cols(mx_s, rows, pair))
        wms.append(jnp.where(incl, jnp.exp(jnp.minimum(expo, 0.0)) * qk_u, 0.0))
    for (_, rows, pair), wm in zip(units, wms):
        ve = vext(rows, pair)
        intra_s[rows, 3 * pair * LANES:3 * (pair + 1) * LANES] = _dot(
            wm, jnp.where(diag_blocks, jnp.concatenate([ve, ve], axis=0), 0.0))
    for c, rows, pair in units:
        wk = k_s[rows, pair_cols[pair]] * _pair_select(*_pair_cols(ws_s, rows, pair))
        upd_s[c, pair] = jnp.where(diag_blocks, _dot_tn(wk, vext(rows, pair)), 0.0)

    def chunk_step(c, carry):
        chains = [(b, pair) for b in range(nb) for pair in range(npair)]
        ss = [s_ref[b * npair + pair] for b, pair in chains]
        inters = [_dot(q_s[pl.ds(pl.multiple_of(c * CHUNK + b * tb, CHUNK), CHUNK), pair_cols[pair]], s)
                  for (b, pair), s in zip(chains, ss)]
        for (b, pair), s, inter in zip(chains, ss, inters):
            g = c + b * ncb
            rows = pl.ds(pl.multiple_of(c * CHUNK + b * tb, CHUNK), CHUNK)
            dp = _pair_cols(dp_s, pl.ds(pl.multiple_of(g * SUBLANES, SUBLANES), 1), pair)
            s_ref[b * npair + pair] = (s * jnp.concatenate([dp[0], dp[1], _pair_select(dp[0], dp[1])], axis=1)
                                       + upd_s[g, pair])
            wis = _pair_cols(wi_s, rows, pair)
            i0 = 3 * pair * LANES
            den = (_pair_select(wis[0], wis[1]) * inter[:, 2 * LANES:]
                   + intra_s[rows, i0 + 2 * LANES:i0 + 3 * LANES])
            den_swapped = pltpu.roll(den, CHUNK, axis=1)
            lo = _iota(den.shape, 1) < CHUNK
            dens = (jnp.where(lo, den, den_swapped), jnp.where(lo, den_swapped, den))
            for h in range(2):
                cols = slice((2 * pair + h) * LANES, (2 * pair + h + 1) * LANES)
                num = wis[h] * inter[:, h * LANES:(h + 1) * LANES] + intra_s[rows, i0 + h * LANES:i0 + (h + 1) * LANES]
                h_s[rows, cols] = num / jnp.maximum(jnp.abs(dens[h]), ed_s[rows, cols])
        return carry

    lax.fori_loop(0, ncb, chunk_step, 0)
    _store_rows(o_ref, _head_rms(h_s[...], ng_ref[...]) * _sigmoid(_load_rows(og_ref)))


def mlstm_mix(p_qk, p_v, p_og, p_gates, batch, conv_w, ig_b, fg_b, norm_g):
    n, cols = p_qk.shape
    w = p_v.shape[1]
    nheads = w // LANES
    t = n // batch
    tb = min(SEQ_BLOCK, t)
    rows = batch * tb
    consts = [conv_w, _gate_row(ig_b, 2 * nheads), _gate_row(fg_b, 2 * nheads),
              jnp.tile(norm_g, nheads).reshape(1, w)]
    blk = lambda: pltpu.VMEM((rows, w), f32)
    out = pl.pallas_call(
        _mlstm_body,
        grid=(t // tb,),
        in_specs=_seq_specs(batch, tb, [cols, w, w, GATE_LANES]) + [_const_spec(c.shape) for c in consts],
        out_specs=_seq_specs(batch, tb, [w])[0],
        out_shape=jax.ShapeDtypeStruct((batch, t, w), bf16),
        scratch_shapes=[pltpu.VMEM((batch, SUBLANES, cols), f32),
                        pltpu.VMEM((batch * nheads // 2, LANES, 3 * LANES), f32),
                        pltpu.VMEM((batch * SUBLANES, GATE_LANES), f32), pltpu.VMEM((rows, cols // 2), f32),
                        pltpu.VMEM((rows, cols // 2), f32)] + [blk() for _ in range(7)]
                       + [pltpu.VMEM((rows // CHUNK * SUBLANES, w), f32), pltpu.VMEM((rows, 3 * w // 2), f32),
                          pltpu.VMEM((rows // CHUNK, nheads // 2, LANES, 3 * LANES), f32)],
        compiler_params=_cparams(("arbitrary",)),
        name="mlstm_mix",
    )(p_qk.reshape(batch, t, cols), p_v.reshape(batch, t, w), p_og.reshape(batch, t, w),
      p_gates.reshape(batch, t, GATE_LANES), *consts)
    return out.reshape(n, w)


def kernel(x, norm_mix_g, norm_mlp_g, mlp_up, mlp_down, final_g, ev_w_in, ev_w_out, lru_conv_w, lru_conv_b, lru_wa,
           lru_ba, lru_wx, lru_bx, lru_lambda, rwkv_mu, rwkv_w0, rwkv_w2, rwkv_a0, rwkv_a2, rwkv_g2, rwkv_kk,
           rwkv_ka, rwkv_rk, rwkv_lnw, rwkv_lnb, od_w_in, od_w_out, gdn_conv_w, gdn_a_log, gdn_dt_bias, gdn_norm_g,
           mlstm_conv_w, mlstm_ig_b, mlstm_fg_b, mlstm_norm_g):
    batch, seq, d = x.shape
    depth = norm_mix_g.shape[0]
    xs = x.reshape(batch * seq, d)
    ev_w, od_w, ev_wo, od_wo, up_w, down_w = ev_w_in, od_w_in, ev_w_out, od_w_out, mlp_up, mlp_down
    for l in range(depth):
        if l % 2 == 0:
            e = l // 2
            lru_cols = 2 * lru_lambda.shape[1]
            p_lru, p_rwkv = norm_proj(xs, norm_mix_g[l], [(ev_w, _layer_spec(ev_w, e))],
                                      [(lru_cols, ev_w.shape[2] - lru_cols)])
            ya = lru_mix(p_lru, batch, lru_conv_w[e], lru_conv_b[e], lru_wa[e], lru_ba[e], lru_wx[e], lru_bx[e],
                         lru_lambda[e])
            yb = rwkv_mix(p_rwkv, batch, rwkv_mu[e], rwkv_w0[e], rwkv_w2[e], rwkv_a0[e], rwkv_a2[e], rwkv_g2[e],
                          rwkv_kk[e], rwkv_ka[e], rwkv_rk[e], rwkv_lnw[e], rwkv_lnb[e])
            w_out = (ev_wo, e)
        else:
            o = l // 2
            weights, widths = _odd_in_weights(od_w, o, gdn_conv_w.shape[2], gdn_norm_g.shape[1] * gdn_a_log.shape[1],
                                              mlstm_conv_w.shape[2], mlstm_norm_g.shape[1] * mlstm_ig_b.shape[1],
                                              gdn_a_log.shape[1], mlstm_ig_b.shape[1])
            p_qkv, p_z, p_mqk, p_mv, p_mog, p_gates = norm_proj(xs, norm_mix_g[l], weights, widths)
            ya = gdn_mix(p_qkv, p_z, p_gates, batch, gdn_conv_w[o], gdn_a_log[o], gdn_dt_bias[o], gdn_norm_g[o])
            yb = mlstm_mix(p_mqk, p_mv, p_mog, p_gates, batch, mlstm_conv_w[o], mlstm_ig_b[o], mlstm_fg_b[o],
                           mlstm_norm_g[o])
            w_out = (od_wo, o)
        xs = out_mlp(ya, yb, xs, w_out, norm_mlp_g[l], (up_w, l), (down_w, l), final_g, l == depth - 1)
    return xs.reshape(batch, seq, d)


def _odd_in_weights(w_all, layer, gdn_qkv, gdn_w, mlstm_qk, mlstm_w, gdn_heads, mlstm_heads):
    w = w_all[layer]
    lead = gdn_qkv + gdn_w
    m0 = lead + 2 * gdn_heads
    m1 = m0 + mlstm_qk + 2 * mlstm_w
    ngate = 2 * gdn_heads + 2 * mlstm_heads
    gates = jnp.concatenate([w[:, lead:m0], w[:, m1:m1 + 2 * mlstm_heads],
                             jnp.zeros((w.shape[0], GATE_LANES - ngate), w.dtype)], axis=1)
    weights = [(part, _const_spec(part.shape)) for part in (w[:, 0:lead], w[:, m0:m1], gates)]
    return weights, [(gdn_qkv, gdn_w), (mlstm_qk, mlstm_w, mlstm_w), (GATE_LANES,)]
```
